```python
import math
import jax
import jax.numpy as jnp
from jax import lax
import numpy as np

D_MODEL = 2048
BATCH = 2
SEQ = 4096
DEPTH = 2

CTX_LEN = 256
GRID_W = 64
EPS = 1e-6

N_HEADS = D_MODEL // 128
Q_LORA = 512
KV_LORA = 512
NOPE_DIM = 128
ROPE_DIM = 64
V_DIM = 128
ROPE_BASE = 10000.0
Q_BLOCK = 128
SM_SCALE = (NOPE_DIM + ROPE_DIM) ** -0.5

D_INNER = 2 * D_MODEL
SSM_HEAD_DIM = 64
SSM_HEADS = D_INNER // SSM_HEAD_DIM
SSM_GROUPS = 8
D_STATE = 128
CONV_K = 5
CONV_DIM = D_INNER + 2 * SSM_GROUPS * D_STATE
CHUNK = 128

PROJ_SIZES = (Q_LORA, KV_LORA, ROPE_DIM, D_INNER, CONV_DIM, 2 * SSM_HEADS, D_MODEL, D_MODEL)
IN_COLS = sum(PROJ_SIZES)

D_FF = 256 * ((8 * D_MODEL // 3 + 255) // 256)
N_EXPERTS = 8
TOP_K = 2
D_FF_EXPERT = 7 * D_MODEL // 2
MOE_BLOCK = 128
N_DENSE = (DEPTH + 1) // 2
N_MOE = DEPTH // 2

kernel_name = 'hybrid_mla_ssd_moe_dit_block'


def rms_norm(x, g):
    xf = x.astype(jnp.float32)
    y = xf * lax.rsqrt(jnp.mean(xf * xf, axis=-1, keepdims=True) + EPS)
    return y.astype(x.dtype) * g


def modulate(h, shift, scale):
    return h * (1 + scale) + shift


def swiglu(t, w1, w3, w2):
    return (jax.nn.silu(t @ w1) * (t @ w3)) @ w2


def split_proj(p):
    idx = np.cumsum(PROJ_SIZES)[:-1]
    return jnp.split(p, [int(i) for i in idx], axis=-1)


def axial_rope_tables(n):
    rows = n // GRID_W
    row = jnp.repeat(jnp.arange(rows, dtype=jnp.float32), GRID_W)
    col = jnp.tile(jnp.arange(GRID_W, dtype=jnp.float32), rows)
    axis_dim = ROPE_DIM // 2
    inv = ROPE_BASE ** (-jnp.arange(0, axis_dim, 2, dtype=jnp.float32) / axis_dim)
    ang = jnp.stack([row[:, None] * inv, col[:, None] * inv], axis=1)
    return jnp.cos(ang), jnp.sin(ang)


def apply_axial_rope(t, cos, sin):
    B, n, H, _ = t.shape
    tr = t.reshape(B, n, H, 2, 2, ROPE_DIM // 4)
    t1, t2 = tr[..., 0, :], tr[..., 1, :]
    cs = cos[None, :, None].astype(t.dtype)
    sn = sin[None, :, None].astype(t.dtype)
    out = jnp.stack([t1 * cs - t2 * sn, t2 * cs + t1 * sn], axis=-2)
    return out.reshape(B, n, H, ROPE_DIM)


def mla_qkv(c_q, c_kv, q_norm, w_uq, kv_norm, w_ukv):
    B, N = c_q.shape[:2]
    q = (rms_norm(c_q, q_norm) @ w_uq).reshape(B, N, N_HEADS, NOPE_DIM + ROPE_DIM)
    kv = (rms_norm(c_kv, kv_norm) @ w_ukv).reshape(B, N, N_HEADS, NOPE_DIM + V_DIM)
    return q[..., :NOPE_DIM], q[..., NOPE_DIM:], kv[..., :NOPE_DIM], kv[..., NOPE_DIM:]


def mla_attend(qn, qr, kn, kr, v):
    s = jnp.einsum('bqhd,bkhd->bhqk', qn, kn) + jnp.einsum('bqhr,bkr->bhqk', qr, kr)
    p = jax.nn.softmax(s.astype(jnp.float32) * SM_SCALE, axis=-1).astype(v.dtype)
    return jnp.einsum('bhqk,bkhd->bqhd', p, v)


def blocked_attend(qn, qr, kn, kr, v):
    B, Q, H, _ = qn.shape
    nb = Q // Q_BLOCK
    blk = lambda t: jnp.moveaxis(t.reshape(B, nb, Q_BLOCK, *t.shape[2:]), 1, 0)
    o = lax.map(lambda qs: mla_attend(qs[0], qs[1], kn, kr, v), (blk(qn), blk(qr)))
    return jnp.moveaxis(o, 0, 1).reshape(B, Q, H * V_DIM)


def dwconv_silu(u, w, b):
    y = lax.conv_general_dilated(u, w[:, None, :], window_strides=(1,),
                                 padding=[(CONV_K // 2, CONV_K // 2)],
                                 dimension_numbers=('NWC', 'WIO', 'NWC'),
                                 feature_group_count=u.shape[-1])
    return jax.nn.silu(y + b)


def ssd_inputs(xbc_raw, dt_raw, conv_w, conv_b, dt_bias):
    B, N, _ = xbc_raw.shape
    xbc = dwconv_silu(xbc_raw, conv_w, conv_b)
    gs = SSM_GROUPS * D_STATE
    xs = xbc[..., :D_INNER].reshape(B, N, SSM_HEADS, SSM_HEAD_DIM)
    bm = xbc[..., D_INNER:D_INNER + gs].reshape(B, N, SSM_GROUPS, D_STATE)
    cm = xbc[..., D_INNER + gs:].reshape(B, N, SSM_GROUPS, D_STATE)
    dt = jax.nn.softplus(dt_raw.astype(jnp.float32).reshape(B, N, 2, SSM_HEADS) + dt_bias.astype(jnp.float32))
    return xs, bm, cm, dt


def ssd_scan(x, dt, A, bm, cm, init_state, with_output):
    Bsz, N, H, P = x.shape
    G, DS = bm.shape[2], bm.shape[3]
    E = H // G
    nc = N // CHUNK
    dtype = x.dtype
    xc = (x * dt[..., None].astype(dtype)).reshape(Bsz, nc, CHUNK, G, E, P)
    a = jnp.moveaxis((dt * A).reshape(Bsz, nc, CHUNK, G, E), 2, -1)
    a_cum = jnp.cumsum(a, axis=-1)
    bc = bm.reshape(Bsz, nc, CHUNK, G, DS)
    cc = cm.reshape(Bsz, nc, CHUNK, G, DS)
    to_end = jnp.exp(a_cum[..., -1:] - a_cum).astype(dtype)
    states = jnp.einsum('bcsgn,bcges,bcsgep->bcgepn', bc, to_end, xc)
    chunk_decay = jnp.exp(a_cum[..., -1]).astype(dtype)

    def step(carry, inp):
        st, dec = inp
        return carry * dec[..., None, None] + st, carry

    final, prev = lax.scan(step, init_state.reshape(Bsz, G, E, P, DS),
                           (jnp.moveaxis(states, 1, 0), jnp.moveaxis(chunk_decay, 1, 0)))
    final = final.reshape(Bsz, H, P, DS)
    if not with_output:
        return None, final
    prev = jnp.moveaxis(prev, 0, 1)
    lower = jnp.tril(jnp.ones((CHUNK, CHUNK), dtype=bool))
    seg = jnp.where(lower, a_cum[..., :, None] - a_cum[..., None, :], -jnp.inf)
    decay = jnp.exp(seg).astype(dtype)
    cb = jnp.einsum('bclgn,bcsgn->bcgls', cc, bc)
    y_diag = jnp.einsum('bcgls,bcgels,bcsgep->bclgep', cb, decay, xc)
    y_off = jnp.einsum('bclgn,bcgepn,bcgel->bclgep', cc, prev, jnp.exp(a_cum).astype(dtype))
    return (y_diag + y_off).reshape(Bsz, N, H, P), final


def gated_group_rmsnorm(y, z, w):
    B, N, _ = y.shape
    u = (y * jax.nn.silu(z)).reshape(B, N, SSM_GROUPS, D_INNER // SSM_GROUPS).astype(jnp.float32)
    u = u * lax.rsqrt(jnp.mean(u * u, axis=-1, keepdims=True) + EPS)
    return u.reshape(B, N, D_INNER).astype(y.dtype) * w


def merge_branches(att, ssm, z, g_a, g_b, ssm_norm, w_oa, w_ob, w_out):
    B, N = att.shape[:2]
    o_a = att.reshape(B, N, -1) @ w_oa
    o_b = gated_group_rmsnorm(ssm.reshape(B, N, D_INNER), z, ssm_norm) @ w_ob
    return (jax.nn.sigmoid(g_a) * o_a + jax.nn.sigmoid(g_b) * o_b) @ w_out


def token_mixer(h_ctx, h_lat, cos, sin, w_in, q_norm, w_uq, kv_norm, w_ukv, conv_w, conv_b,
                a_log, dt_bias, d_skip, ssm_norm, w_oa, w_ob, w_out, need_ctx):
    cq_c, ckv_c, kr_c, z_c, xbc_c, dtr_c, ga_c, gb_c = split_proj(h_ctx @ w_in)
    cq_l, ckv_l, kr_l, z_l, xbc_l, dtr_l, ga_l, gb_l = split_proj(h_lat @ w_in)
    qn_c, qr_c, kn_c, v_c = mla_qkv(cq_c, ckv_c, q_norm, w_uq, kv_norm, w_ukv)
    qn_l, qr_l, kn_l, v_l = mla_qkv(cq_l, ckv_l, q_norm, w_uq, kv_norm, w_ukv)
    qr_l = apply_axial_rope(qr_l, cos, sin)
    kr_l = apply_axial_rope(kr_l[:, :, None], cos, sin)[:, :, 0]
    kn = jnp.concatenate([kn_c, kn_l], axis=1)
    kr = jnp.concatenate([kr_c, kr_l], axis=1)
    v = jnp.concatenate([v_c, v_l], axis=1)
    att_l = blocked_attend(qn_l, qr_l, kn, kr, v)
    xs_c, b_c, c_c, dt_c = ssd_inputs(xbc_c, dtr_c, conv_w, conv_b, dt_bias)
    xs_l, b_l, c_l, dt_l = ssd_inputs(xbc_l, dtr_l, conv_w, conv_b, dt_bias)
    A = -jnp.exp(a_log.astype(jnp.float32))
    d_sum = (d_skip[0] + d_skip[1])[:, None]
    flip = lambda t: jnp.flip(t, axis=1)
    zeros = jnp.zeros((xs_c.shape[0], SSM_HEADS, SSM_HEAD_DIM, D_STATE), xs_c.dtype)
    y_cf, s_f = ssd_scan(xs_c, dt_c[:, :, 0], A[0], b_c, c_c, zeros, need_ctx)
    y_cb, s_b = ssd_scan(flip(xs_c), flip(dt_c[:, :, 1]), A[1], flip(b_c), flip(c_c), zeros, need_ctx)
    y_lf, _ = ssd_scan(xs_l, dt_l[:, :, 0], A[0], b_l, c_l, s_f, True)
    y_lb, _ = ssd_scan(flip(xs_l), flip(dt_l[:, :, 1]), A[1], flip(b_l), flip(c_l), s_b, True)
    ssm_l = y_lf + flip(y_lb) + d_sum * xs_l
    out_l = merge_branches(att_l, ssm_l, z_l, ga_l, gb_l, ssm_norm, w_oa, w_ob, w_out)
    if not need_ctx:
        return None, out_l
    att_c = mla_attend(qn_c, qr_c, kn_c, kr_c, v_c)
    ssm_c = y_cf + flip(y_cb) + d_sum * xs_c
    out_c = merge_branches(att_c, ssm_c, z_c, ga_c, gb_c, ssm_norm, w_oa, w_ob, w_out)
    return out_c, out_l


def moe_swiglu(h, w_router, w1, w3, w2):
    B, N, D = h.shape
    T = B * N
    t = h.reshape(T, D)
    logits = (t @ w_router).astype(jnp.float32)
    top_v, top_i = lax.top_k(logits, TOP_K)
    gate = jax.nn.softmax(top_v, axis=-1).astype(h.dtype).reshape(-1)
    expert = top_i.reshape(-1)
    token = jnp.repeat(jnp.arange(T), TOP_K)
    order = jnp.argsort(expert)
    s_exp, s_tok, s_gate = expert[order], token[order], gate[order]
    counts = jnp.bincount(expert, length=N_EXPERTS)
    padded = (counts + MOE_BLOCK - 1) // MOE_BLOCK * MOE_BLOCK
    pad_end = jnp.cumsum(padded)
    rank = jnp.arange(T * TOP_K) - (jnp.cumsum(counts) - counts)[s_exp]
    dest = (pad_end - padded)[s_exp] + rank
    n_blocks = -(-(T * TOP_K) // MOE_BLOCK) + N_EXPERTS
    block_expert = jnp.minimum(
        jnp.searchsorted(pad_end, jnp.arange(n_blocks) * MOE_BLOCK, side='right'), N_EXPERTS - 1)
    buf = jnp.zeros((n_blocks * MOE_BLOCK, D), h.dtype).at[dest].set(t[s_tok])

    def expert_block(args):
        xb, e = args
        return swiglu(xb, w1[e], w3[e], w2[e])

    y = lax.map(expert_block, (buf.reshape(n_blocks, MOE_BLOCK, D), block_expert))
    y = y.reshape(n_blocks * MOE_BLOCK, D)[dest] * s_gate[:, None]
    return jnp.zeros_like(t).at[s_tok].add(y).reshape(B, N, D)


def setup_inputs(seed: int = 0) -> dict:
    key = jax.random.key(seed)
    ks = iter(jax.random.split(key, 40))
    f32 = jnp.float32
    L = DEPTH

    def nrm(shape, fan_in, scale=1.0):
        return jax.random.normal(next(ks), shape, f32) * (scale * fan_in ** -0.5)

    def gain(shape):
        return 1.0 + 0.05 * jax.random.normal(next(ks), shape, f32)

    def small(shape):
        return 0.02 * jax.random.normal(next(ks), shape, f32)

    x = jax.random.normal(next(ks), (BATCH, SEQ, D_MODEL), f32)
    c = jax.random.normal(next(ks), (BATCH, D_MODEL), f32)
    ctx = jax.random.normal(next(ks), (BATCH, CTX_LEN, D_MODEL), f32)
    c_ctx = jax.random.normal(next(ks), (D_MODEL,), f32)
    norm_mix = gain((L, D_MODEL))
    norm_ffn = gain((L, D_MODEL))
    w_ada = nrm((L, D_MODEL, 6 * D_MODEL), D_MODEL, 0.2)
    b_ada = small((L, 6 * D_MODEL))
    w_in = nrm((L, D_MODEL, IN_COLS), D_MODEL)
    q_norm = gain((L, Q_LORA))
    w_uq = nrm((L, Q_LORA, N_HEADS * (NOPE_DIM + ROPE_DIM)), Q_LORA)
    kv_norm = gain((L, KV_LORA))
    w_ukv = nrm((L, KV_LORA, N_HEADS * (NOPE_DIM + V_DIM)), KV_LORA)
    conv_w = nrm((L, CONV_K, CONV_DIM), CONV_K)
    conv_b = small((L, CONV_DIM))
    a_log = jnp.log(jax.random.uniform(next(ks), (L, 2, SSM_HEADS), f32, 1.0, 16.0))
    dt0 = jnp.exp(jax.random.uniform(next(ks), (L, 2, SSM_HEADS), f32, math.log(1e-3), math.log(1e-1)))
    dt_bias = dt0 + jnp.log(-jnp.expm1(-dt0))
    d_skip = 1.0 + 0.1 * jax.random.normal(next(ks), (L, 2, SSM_HEADS), f32)
    ssm_norm = gain((L, D_INNER))
    w_oa = nrm((L, N_HEADS * V_DIM, D_MODEL), N_HEADS * V_DIM)
    w_ob = nrm((L, D_INNER, D_MODEL), D_INNER)
    w_out = nrm((L, D_MODEL, D_MODEL), D_MODEL)
    w1_dense = nrm((N_DENSE, D_MODEL, D_FF), D_MODEL)
    w3_dense = nrm((N_DENSE, D_MODEL, D_FF), D_MODEL)
    w2_dense = nrm((N_DENSE, D_FF, D_MODEL), D_FF)
    w_router = nrm((N_MOE, D_MODEL, N_EXPERTS), D_MODEL)
    w1_moe = nrm((N_MOE, N_EXPERTS, D_MODEL, D_FF_EXPERT), D_MODEL)
    w3_moe = nrm((N_MOE, N_EXPERTS, D_MODEL, D_FF_EXPERT), D_MODEL)
    w2_moe = nrm((N_MOE, N_EXPERTS, D_FF_EXPERT, D_MODEL), D_FF_EXPERT)
    final_norm = gain((D_MODEL,))
    return {'x': x, 'c': c, 'ctx': ctx, 'c_ctx': c_ctx, 'norm_mix': norm_mix, 'norm_ffn': norm_ffn,
            'w_ada': w_ada, 'b_ada': b_ada, 'w_in': w_in, 'q_norm': q_norm, 'w_uq': w_uq,
            'kv_norm': kv_norm, 'w_ukv': w_ukv, 'conv_w': conv_w, 'conv_b': conv_b, 'a_log': a_log,
            'dt_bias': dt_bias, 'd_skip': d_skip, 'ssm_norm': ssm_norm, 'w_oa': w_oa, 'w_ob': w_ob,
            'w_out': w_out, 'w1_dense': w1_dense, 'w3_dense': w3_dense, 'w2_dense': w2_dense,
            'w_router': w_router, 'w1_moe': w1_moe, 'w3_moe': w3_moe, 'w2_moe': w2_moe,
            'final_norm': final_norm}


def reference(x, c, ctx, c_ctx, norm_mix, norm_ffn, w_ada, b_ada, w_in, q_norm, w_uq, kv_norm, w_ukv,
              conv_w, conv_b, a_log, dt_bias, d_skip, ssm_norm, w_oa, w_ob, w_out, w1_dense, w3_dense,
              w2_dense, w_router, w1_moe, w3_moe, w2_moe, final_norm):
    cos, sin = axial_rope_tables(x.shape[1])
    for l in range(DEPTH):
        last = l == DEPTH - 1
        m_lat = [t[:, None] for t in jnp.split(jax.nn.silu(c) @ w_ada[l] + b_ada[l], 6, axis=-1)]
        m_ctx = jnp.split(jax.nn.silu(c_ctx) @ w_ada[l] + b_ada[l], 6, axis=-1)
        h_l = modulate(rms_norm(x, norm_mix[l]), m_lat[0], m_lat[1])
        h_c = modulate(rms_norm(ctx, norm_mix[l]), m_ctx[0], m_ctx[1])
        y_c, y_l = token_mixer(h_c, h_l, cos, sin, w_in[l], q_norm[l], w_uq[l], kv_norm[l], w_ukv[l],
                               conv_w[l], conv_b[l], a_log[l], dt_bias[l], d_skip[l], ssm_norm[l],
                               w_oa[l], w_ob[l], w_out[l], not last)
        x = x + m_lat[2] * y_l
        h = modulate(rms_norm(x, norm_ffn[l]), m_lat[3], m_lat[4])
        if not last:
            ctx = ctx + m_ctx[2] * y_c
            h_c = modulate(rms_norm(ctx, norm_ffn[l]), m_ctx[3], m_ctx[4])
            h = jnp.concatenate([h_c, h], axis=1)
        if l % 2 == 0:
            f = swiglu(h, w1_dense[l // 2], w3_dense[l // 2], w2_dense[l // 2])
        else:
            f = moe_swiglu(h, w_router[l // 2], w1_moe[l // 2], w3_moe[l // 2], w2_moe[l // 2])
        if not last:
            ctx = ctx + m_ctx[5] * f[:, :CTX_LEN]
            f = f[:, CTX_LEN:]
        x = x + m_lat[5] * f
    return rms_norm(x, final_norm)
```

```python
import functools
import math

import numpy as np
import jax
import jax.numpy as jnp
from jax import lax
from jax.experimental import pallas as pl
from jax.experimental.pallas import tpu as pltpu

D_MODEL = 2048
BATCH = 2
SEQ = 4096
DEPTH = 2
CTX_LEN = 256
GRID_W = 64
EPS = 1e-6

N_HEADS = D_MODEL // 128
Q_LORA = 512
KV_LORA = 512
NOPE_DIM = 128
ROPE_DIM = 64
V_DIM = 128
ROPE_BASE = 10000.0
SM_SCALE = (NOPE_DIM + ROPE_DIM) ** -0.5

D_INNER = 2 * D_MODEL
SSM_HEAD_DIM = 64
SSM_HEADS = D_INNER // SSM_HEAD_DIM
SSM_GROUPS = 8
D_STATE = 128
CONV_K = 5
CONV_DIM = D_INNER + 2 * SSM_GROUPS * D_STATE
CHUNK = 128

PROJ_SIZES = (Q_LORA, KV_LORA, ROPE_DIM, D_INNER, CONV_DIM, 2 * SSM_HEADS, D_MODEL, D_MODEL)

D_FF = 256 * ((8 * D_MODEL // 3 + 255) // 256)
N_EXPERTS = 8
TOP_K = 2
D_FF_EXPERT = 7 * D_MODEL // 2

F32 = jnp.float32
BF16 = jnp.bfloat16

VMEM_LIMIT_BYTES = 56 * 1024 * 1024
LOG2E = 1.4426950408889634

ROW_TILE = 512
MOE_ROW_TILE = 512
MOE_FF_TILE = 512
ATTN_Q_TILE = 512
ATTN_KV_CHUNK = 512


def _largest_tile(n, candidates):
    for c in candidates:
        if n % c == 0:
            return c
    raise ValueError(f"no tile for {n}")


def _mm_kernel_single(a_ref, b_ref, o_ref):
    o_ref[...] = jnp.dot(a_ref[...], b_ref[...], preferred_element_type=F32).astype(o_ref.dtype)


def _mm_kernel_acc(a_ref, b_ref, o_ref, acc_ref, *, nk):
    k = pl.program_id(2)

    @pl.when(k == 0)
    def _():
        acc_ref[...] = jnp.zeros_like(acc_ref)

    acc_ref[...] += jnp.dot(a_ref[...], b_ref[...], preferred_element_type=F32)

    @pl.when(k == nk - 1)
    def _():
        o_ref[...] = acc_ref[...].astype(o_ref.dtype)


def matmul(a, b, out_dtype=F32):
    a = a.astype(BF16)
    b = b.astype(BF16)
    m, k = a.shape
    k2, n = b.shape
    assert k == k2
    tm = _largest_tile(m, (ROW_TILE, 256, 128))
    tn = _largest_tile(n, (1024, 512, 256, 128))
    if k <= 2048:
        tk = k
    else:
        tk = _largest_tile(k, (2048, 2816, 1024, 512))
    nk = k // tk
    if nk == 1:
        return pl.pallas_call(
            _mm_kernel_single,
            out_shape=jax.ShapeDtypeStruct((m, n), out_dtype),
            grid=(m // tm, n // tn),
            in_specs=[pl.BlockSpec((tm, k), lambda i, j: (i, 0)),
                      pl.BlockSpec((k, tn), lambda i, j: (0, j))],
            out_specs=pl.BlockSpec((tm, tn), lambda i, j: (i, j)),
            compiler_params=pltpu.CompilerParams(
                dimension_semantics=("parallel", "parallel"),
                vmem_limit_bytes=VMEM_LIMIT_BYTES),
            name="mm_single",
        )(a, b)
    return pl.pallas_call(
        functools.partial(_mm_kernel_acc, nk=nk),
        out_shape=jax.ShapeDtypeStruct((m, n), out_dtype),
        grid=(m // tm, n // tn, nk),
        in_specs=[pl.BlockSpec((tm, tk), lambda i, j, kk: (i, kk)),
                  pl.BlockSpec((tk, tn), lambda i, j, kk: (kk, j))],
        out_specs=pl.BlockSpec((tm, tn), lambda i, j, kk: (i, j)),
        scratch_shapes=[pltpu.VMEM((tm, tn), F32)],
        compiler_params=pltpu.CompilerParams(
            dimension_semantics=("parallel", "parallel", "arbitrary"),
            vmem_limit_bytes=VMEM_LIMIT_BYTES),
        name="mm_acc",
    )(a, b)


def _attn_kernel(q_ref, k_ref, v_ref, o_ref, *, chunks):
    q = q_ref[...]
    c2 = SM_SCALE * LOG2E
    m = l = acc = None
    for start, size in chunks:
        k = k_ref[start:start + size, :]
        v = v_ref[start:start + size, :]
        s = lax.dot_general(q, k, (((1,), (1,)), ((), ())), preferred_element_type=F32)
        m_cur = jnp.max(s, axis=-1, keepdims=True)
        if m is None:
            m_new = m_cur
            p = jnp.exp2((s - m_new) * c2)
            l = jnp.sum(p, axis=-1, keepdims=True)
            acc = jnp.dot(p.astype(BF16), v, preferred_element_type=F32)
        else:
            m_new = jnp.maximum(m, m_cur)
            alpha = jnp.exp2((m - m_new) * c2)
            p = jnp.exp2((s - m_new) * c2)
            l = alpha * l + jnp.sum(p, axis=-1, keepdims=True)
            acc = alpha * acc + jnp.dot(p.astype(BF16), v, preferred_element_type=F32)
        m = m_new
    o_ref[...] = (acc / l).astype(o_ref.dtype)


def latent_attention(q_cat, k_cat, v):
    b, h, nq, dk = q_cat.shape
    nk = k_cat.shape[2]
    tq = ATTN_Q_TILE
    chunks = []
    start = 0
    while start < nk:
        size = min(ATTN_KV_CHUNK, nk - start)
        chunks.append((start, size))
        start += size
    return pl.pallas_call(
        functools.partial(_attn_kernel, chunks=tuple(chunks)),
        out_shape=jax.ShapeDtypeStruct((b, h, nq, V_DIM), F32),
        grid=(b, h, nq // tq),
        in_specs=[pl.BlockSpec((None, None, tq, dk), lambda bi, hi, qi: (bi, hi, qi, 0)),
                  pl.BlockSpec((None, None, nk, dk), lambda bi, hi, qi: (bi, hi, 0, 0)),
                  pl.BlockSpec((None, None, nk, V_DIM), lambda bi, hi, qi: (bi, hi, 0, 0))],
        out_specs=pl.BlockSpec((None, None, tq, V_DIM), lambda bi, hi, qi: (bi, hi, qi, 0)),
        compiler_params=pltpu.CompilerParams(
            dimension_semantics=("parallel", "parallel", "parallel"),
            vmem_limit_bytes=VMEM_LIMIT_BYTES),
        name="latent_attention",
    )(q_cat, k_cat, v)


def _moe_kernel(be_ref, nused_ref, x_ref, w1_ref, w3_ref, w2_ref, o_ref):
    i = pl.program_id(0)
    f = pl.program_id(1)

    @pl.when(f == 0)
    def _():
        o_ref[...] = jnp.zeros_like(o_ref)

    @pl.when(i < nused_ref[0])
    def _():
        x = x_ref[...]
        h1 = jnp.dot(x, w1_ref[...], preferred_element_type=F32)
        h3 = jnp.dot(x, w3_ref[...], preferred_element_type=F32)
        g = (h1 * jax.nn.sigmoid(h1)) * h3
        o_ref[...] += jnp.dot(g.astype(BF16), w2_ref[...], preferred_element_type=F32)


def moe_experts(buf, block_expert, n_used, w1, w3, w2):
    r, d = buf.shape
    tm, tf = MOE_ROW_TILE, MOE_FF_TILE
    nb = r // tm
    nf = D_FF_EXPERT // tf

    def row_idx(i, f, be, nu):
        return (jnp.minimum(i, nu[0] - 1), 0)

    def f_idx(i, f, nu):
        return jnp.where(i < nu[0], f, nf - 1)

    grid_spec = pltpu.PrefetchScalarGridSpec(
        num_scalar_prefetch=2,
        grid=(nb, nf),
        in_specs=[
            pl.BlockSpec((tm, d), row_idx),
            pl.BlockSpec((None, d, tf), lambda i, f, be, nu: (be[i], 0, f_idx(i, f, nu))),
            pl.BlockSpec((None, d, tf), lambda i, f, be, nu: (be[i], 0, f_idx(i, f, nu))),
            pl.BlockSpec((None, tf, d), lambda i, f, be, nu: (be[i], f_idx(i, f, nu), 0)),
        ],
        out_specs=pl.BlockSpec((tm, d), lambda i, f, be, nu: (i, 0)),
    )
    return pl.pallas_call(
        _moe_kernel,
        out_shape=jax.ShapeDtypeStruct((r, d), F32),
        grid_spec=grid_spec,
        compiler_params=pltpu.CompilerParams(
            dimension_semantics=("arbitrary", "arbitrary"),
            vmem_limit_bytes=VMEM_LIMIT_BYTES),
        name="moe_experts",
    )(block_expert, n_used, buf, w1, w3, w2)


def rms_norm(x, g):
    xf = x.astype(F32)
    y = xf * lax.rsqrt(jnp.mean(xf * xf, axis=-1, keepdims=True) + EPS)
    return y.astype(x.dtype) * g


def modulate(h, shift, scale):
    return h * (1 + scale) + shift


def axial_rope_tables(n):
    rows = n // GRID_W
    row = jnp.repeat(jnp.arange(rows, dtype=F32), GRID_W)
    col = jnp.tile(jnp.arange(GRID_W, dtype=F32), rows)
    axis_dim = ROPE_DIM // 2
    inv = ROPE_BASE ** (-jnp.arange(0, axis_dim, 2, dtype=F32) / axis_dim)
    ang = jnp.stack([row[:, None] * inv, col[:, None] * inv], axis=1)
    return jnp.cos(ang), jnp.sin(ang)


def apply_axial_rope(t, cos, sin):
    B, n, H, _ = t.shape
    tr = t.reshape(B, n, H, 2, 2, ROPE_DIM // 4)
    t1, t2 = tr[..., 0, :], tr[..., 1, :]
    cs = cos[None, :, None].astype(t.dtype)
    sn = sin[None, :, None].astype(t.dtype)
    out = jnp.stack([t1 * cs - t2 * sn, t2 * cs + t1 * sn], axis=-2)
    return out.reshape(B, n, H, ROPE_DIM)


def mla_attend(qn, qr, kn, kr, v):
    s = jnp.einsum('bqhd,bkhd->bhqk', qn, kn) + jnp.einsum('bqhr,bkr->bhqk', qr, kr)
    p = jax.nn.softmax(s.astype(F32) * SM_SCALE, axis=-1).astype(v.dtype)
    return jnp.einsum('bhqk,bkhd->bqhd', p, v)


def dwconv_silu(u, w, b):
    y = lax.conv_general_dilated(u, w[:, None, :], window_strides=(1,),
                                 padding=[(CONV_K // 2, CONV_K // 2)],
                                 dimension_numbers=('NWC', 'WIO', 'NWC'),
                                 feature_group_count=u.shape[-1])
    return jax.nn.silu(y + b)


def ssd_inputs(xbc_raw, dt_raw, conv_w, conv_b, dt_bias):
    B, N, _ = xbc_raw.shape
    xbc = dwconv_silu(xbc_raw, conv_w, conv_b)
    gs = SSM_GROUPS * D_STATE
    xs = xbc[..., :D_INNER].reshape(B, N, SSM_HEADS, SSM_HEAD_DIM)
    bm = xbc[..., D_INNER:D_INNER + gs].reshape(B, N, SSM_GROUPS, D_STATE)
    cm = xbc[..., D_INNER + gs:].reshape(B, N, SSM_GROUPS, D_STATE)
    dt = jax.nn.softplus(dt_raw.astype(F32).reshape(B, N, 2, SSM_HEADS) + dt_bias.astype(F32))
    return xs, bm, cm, dt


def ssd_scan(x, dt, A, bm, cm, init_state, with_output):
    Bsz, N, H, P = x.shape
    G, DS = bm.shape[2], bm.shape[3]
    E = H // G
    nc = N // CHUNK
    dtype = x.dtype
    xc = (x * dt[..., None].astype(dtype)).reshape(Bsz, nc, CHUNK, G, E, P)
    a = jnp.moveaxis((dt * A).reshape(Bsz, nc, CHUNK, G, E), 2, -1)
    a_cum = jnp.cumsum(a, axis=-1)
    bc = bm.reshape(Bsz, nc, CHUNK, G, DS)
    cc = cm.reshape(Bsz, nc, CHUNK, G, DS)
    to_end = jnp.exp(a_cum[..., -1:] - a_cum).astype(dtype)
    states = jnp.einsum('bcsgn,bcges,bcsgep->bcgepn', bc, to_end, xc)
    chunk_decay = jnp.exp(a_cum[..., -1]).astype(dtype)

    def step(carry, inp):
        st, dec = inp
        return carry * dec[..., None, None] + st, carry

    final, prev = lax.scan(step, init_state.reshape(Bsz, G, E, P, DS),
                           (jnp.moveaxis(states, 1, 0), jnp.moveaxis(chunk_decay, 1, 0)))
    final = final.reshape(Bsz, H, P, DS)
    if not with_output:
        return None, final
    prev = jnp.moveaxis(prev, 0, 1)
    lower = jnp.tril(jnp.ones((CHUNK, CHUNK), dtype=bool))
    seg = jnp.where(lower, a_cum[..., :, None] - a_cum[..., None, :], -jnp.inf)
    decay = jnp.exp(seg).astype(dtype)
    cb = jnp.einsum('bclgn,bcsgn->bcgls', cc, bc)
    y_diag = jnp.einsum('bcgls,bcgels,bcsgep->bclgep', cb, decay, xc)
    y_off = jnp.einsum('bclgn,bcgepn,bcgel->bclgep', cc, prev, jnp.exp(a_cum).astype(dtype))
    return (y_diag + y_off).reshape(Bsz, N, H, P), final


def gated_group_rmsnorm(y, z, w):
    B, N, _ = y.shape
    u = (y * jax.nn.silu(z)).reshape(B, N, SSM_GROUPS, D_INNER // SSM_GROUPS).astype(F32)
    u = u * lax.rsqrt(jnp.mean(u * u, axis=-1, keepdims=True) + EPS)
    return u.reshape(B, N, D_INNER).astype(y.dtype) * w


def _split_rows(t):
    n_lat = BATCH * SEQ
    return t[:n_lat].reshape(BATCH, SEQ, -1), t[n_lat:].reshape(BATCH, CTX_LEN, -1)


def token_mixer(h_c, h_l, cos, sin, w_in, q_norm, w_uq, kv_norm, w_ukv, conv_w, conv_b,
                a_log, dt_bias, d_skip, ssm_norm, w_oa, w_ob, w_out, need_ctx):
    h_all = jnp.concatenate([h_l.reshape(-1, D_MODEL), h_c.reshape(-1, D_MODEL)], axis=0)
    o = np.cumsum((0,) + PROJ_SIZES)
    seg = lambda i: w_in[:, int(o[i]):int(o[i + 1])]
    w_main = jnp.concatenate([seg(3), seg(4), seg(6), seg(7), seg(0), seg(1)], axis=1)
    w_small = jnp.concatenate([seg(5), seg(2), jnp.zeros((D_MODEL, 64), w_in.dtype)], axis=1)
    p_main = matmul(h_all, w_main)
    p_small = matmul(h_all, w_small)
    c0 = 0
    z_all = p_main[:, c0:c0 + D_INNER]; c0 += D_INNER
    xbc_all = p_main[:, c0:c0 + CONV_DIM]; c0 += CONV_DIM
    ga_all = p_main[:, c0:c0 + D_MODEL]; c0 += D_MODEL
    gb_all = p_main[:, c0:c0 + D_MODEL]; c0 += D_MODEL
    cq_all = p_main[:, c0:c0 + Q_LORA]; c0 += Q_LORA
    ckv_all = p_main[:, c0:c0 + KV_LORA]
    dtr_all = p_small[:, :2 * SSM_HEADS]
    kr_all = p_small[:, 2 * SSM_HEADS:2 * SSM_HEADS + ROPE_DIM]

    q_all = matmul(rms_norm(cq_all, q_norm), w_uq)
    kv_all = matmul(rms_norm(ckv_all, kv_norm), w_ukv)
    q_l, q_c = _split_rows(q_all)
    kv_l, kv_c = _split_rows(kv_all)
    kr_l, kr_c = _split_rows(kr_all)
    q_l = q_l.reshape(BATCH, SEQ, N_HEADS, NOPE_DIM + ROPE_DIM)
    q_c = q_c.reshape(BATCH, CTX_LEN, N_HEADS, NOPE_DIM + ROPE_DIM)
    kv_l = kv_l.reshape(BATCH, SEQ, N_HEADS, NOPE_DIM + V_DIM)
    kv_c = kv_c.reshape(BATCH, CTX_LEN, N_HEADS, NOPE_DIM + V_DIM)
    qn_l, qr_l = q_l[..., :NOPE_DIM], q_l[..., NOPE_DIM:]
    qn_c, qr_c = q_c[..., :NOPE_DIM], q_c[..., NOPE_DIM:]
    kn_l, v_l = kv_l[..., :NOPE_DIM], kv_l[..., NOPE_DIM:]
    kn_c, v_c = kv_c[..., :NOPE_DIM], kv_c[..., NOPE_DIM:]
    qr_l = apply_axial_rope(qr_l, cos, sin)
    kr_l = apply_axial_rope(kr_l[:, :, None], cos, sin)[:, :, 0]
    kn = jnp.concatenate([kn_c, kn_l], axis=1)
    kr = jnp.concatenate([kr_c, kr_l], axis=1)
    v = jnp.concatenate([v_c, v_l], axis=1)
    n_keys = CTX_LEN + SEQ
    q_cat = jnp.concatenate(
        [qn_l, qr_l, jnp.zeros((BATCH, SEQ, N_HEADS, 256 - NOPE_DIM - ROPE_DIM), F32)], axis=-1)
    k_cat = jnp.concatenate(
        [kn, jnp.broadcast_to(kr[:, :, None, :], (BATCH, n_keys, N_HEADS, ROPE_DIM)),
         jnp.zeros((BATCH, n_keys, N_HEADS, 256 - NOPE_DIM - ROPE_DIM), F32)], axis=-1)
    att = latent_attention(jnp.transpose(q_cat, (0, 2, 1, 3)).astype(BF16),
                           jnp.transpose(k_cat, (0, 2, 1, 3)).astype(BF16),
                           jnp.transpose(v, (0, 2, 1, 3)).astype(BF16))
    att_l = jnp.transpose(att, (0, 2, 1, 3)).reshape(BATCH, SEQ, N_HEADS * V_DIM)

    xbc_l, xbc_c = _split_rows(xbc_all)
    dtr_l, dtr_c = _split_rows(dtr_all)
    z_l, z_c = _split_rows(z_all)
    xs_c, b_c, c_c, dt_c = ssd_inputs(xbc_c, dtr_c, conv_w, conv_b, dt_bias)
    xs_l, b_l, c_l, dt_l = ssd_inputs(xbc_l, dtr_l, conv_w, conv_b, dt_bias)
    A = -jnp.exp(a_log.astype(F32))
    d_sum = (d_skip[0] + d_skip[1])[:, None]
    flip = lambda t: jnp.flip(t, axis=1)
    zeros = jnp.zeros((BATCH, SSM_HEADS, SSM_HEAD_DIM, D_STATE), xs_c.dtype)
    y_cf, s_f = ssd_scan(xs_c, dt_c[:, :, 0], A[0], b_c, c_c, zeros, need_ctx)
    y_cb, s_b = ssd_scan(flip(xs_c), flip(dt_c[:, :, 1]), A[1], flip(b_c), flip(c_c), zeros, need_ctx)
    y_lf, _ = ssd_scan(xs_l, dt_l[:, :, 0], A[0], b_l, c_l, s_f, True)
    y_lb, _ = ssd_scan(flip(xs_l), flip(dt_l[:, :, 1]), A[1], flip(b_l), flip(c_l), s_b, True)
    ssm_l = (y_lf + flip(y_lb) + d_sum * xs_l).reshape(BATCH, SEQ, D_INNER)
    u_l = gated_group_rmsnorm(ssm_l, z_l, ssm_norm)

    ga_l, ga_c = _split_rows(ga_all)
    gb_l, gb_c = _split_rows(gb_all)
    if need_ctx:
        att_c = mla_attend(qn_c, qr_c, kn_c, kr_c, v_c).reshape(BATCH, CTX_LEN, N_HEADS * V_DIM)
        ssm_c = (y_cf + flip(y_cb) + d_sum * xs_c).reshape(BATCH, CTX_LEN, D_INNER)
        u_c = gated_group_rmsnorm(ssm_c, z_c, ssm_norm)
        att_rows = jnp.concatenate([att_l.reshape(-1, D_MODEL), att_c.reshape(-1, D_MODEL)], axis=0)
        u_rows = jnp.concatenate([u_l.reshape(-1, D_INNER), u_c.reshape(-1, D_INNER)], axis=0)
        ga_rows, gb_rows = ga_all, gb_all
    else:
        att_rows = att_l.reshape(-1, D_MODEL)
        u_rows = u_l.reshape(-1, D_INNER)
        ga_rows = ga_l.reshape(-1, D_MODEL)
        gb_rows = gb_l.reshape(-1, D_MODEL)
    o_a = matmul(att_rows, w_oa)
    o_b = matmul(u_rows, w_ob)
    mixed = jax.nn.sigmoid(ga_rows) * o_a + jax.nn.sigmoid(gb_rows) * o_b
    out = matmul(mixed, w_out)
    if not need_ctx:
        return None, out.reshape(BATCH, SEQ, D_MODEL)
    out_l, out_c = _split_rows(out)
    return out_c, out_l


def dense_swiglu(h, w1, w3, w2):
    lead = h.shape[:-1]
    t = h.reshape(-1, D_MODEL)
    a = matmul(t, w1)
    b = matmul(t, w3)
    g = jax.nn.silu(a) * b
    return matmul(g, w2).reshape(*lead, D_MODEL)


def moe_swiglu(h, w_router, w1, w3, w2):
    B, N, D = h.shape
    T = B * N
    t = h.reshape(T, D)
    logits = jnp.dot(t, w_router, precision=lax.Precision.HIGHEST).astype(F32)
    top_v, top_i = lax.top_k(logits, TOP_K)
    gate = jax.nn.softmax(top_v, axis=-1).astype(h.dtype)
    expert = top_i.reshape(-1)
    onehot = (expert[:, None] == jnp.arange(N_EXPERTS)[None, :]).astype(jnp.int32)
    incl = jnp.cumsum(onehot, axis=0)
    rank = jnp.sum((incl - onehot) * onehot, axis=1)
    counts = incl[-1]
    tm = MOE_ROW_TILE
    padded = (counts + tm - 1) // tm * tm
    pad_end = jnp.cumsum(padded)
    dest = (pad_end - padded)[expert] + rank
    n_blocks = (T * TOP_K) // tm + N_EXPERTS
    n_used = (pad_end[-1] // tm).astype(jnp.int32)
    block_expert = jnp.minimum(
        jnp.searchsorted(pad_end, jnp.arange(n_blocks) * tm, side='right'), N_EXPERTS - 1).astype(jnp.int32)
    block_expert = jnp.where(jnp.arange(n_blocks) < n_used, block_expert,
                             block_expert[jnp.maximum(n_used - 1, 0)])
    token = jnp.repeat(jnp.arange(T), TOP_K)
    src = jnp.zeros((n_blocks * tm,), jnp.int32).at[dest].set(token)
    buf = jnp.take(t.astype(BF16), src, axis=0)
    y = moe_experts(buf, block_expert, n_used.reshape(1),
                    w1.astype(BF16), w3.astype(BF16), w2.astype(BF16))
    yd = jnp.take(y, dest, axis=0).reshape(T, TOP_K, D)
    return (yd[:, 0] * gate[:, 0:1] + yd[:, 1] * gate[:, 1:2]).reshape(B, N, D)


def kernel(x, c, ctx, c_ctx, norm_mix, norm_ffn, w_ada, b_ada, w_in, q_norm, w_uq, kv_norm, w_ukv,
           conv_w, conv_b, a_log, dt_bias, d_skip, ssm_norm, w_oa, w_ob, w_out, w1_dense, w3_dense,
           w2_dense, w_router, w1_moe, w3_moe, w2_moe, final_norm):
    cos, sin = axial_rope_tables(x.shape[1])
    for l in range(DEPTH):
        last = l == DEPTH - 1
        cc = jnp.concatenate([jax.nn.silu(c), jax.nn.silu(c_ctx)[None],
                              jnp.zeros((128 - BATCH - 1, D_MODEL), F32)], axis=0)
        mods = matmul(cc, w_ada[l]) + b_ada[l]
        m_lat = [t[:, None] for t in jnp.split(mods[:BATCH], 6, axis=-1)]
        m_ctx = jnp.split(mods[BATCH], 6, axis=-1)
        h_l = modulate(rms_norm(x, norm_mix[l]), m_lat[0], m_lat[1])
        h_c = modulate(rms_norm(ctx, norm_mix[l]), m_ctx[0], m_ctx[1])
        y_c, y_l = token_mixer(h_c, h_l, cos, sin, w_in[l], q_norm[l], w_uq[l], kv_norm[l], w_ukv[l],
                               conv_w[l], conv_b[l], a_log[l], dt_bias[l], d_skip[l], ssm_norm[l],
                               w_oa[l], w_ob[l], w_out[l], not last)
        x = x + m_lat[2] * y_l
        h = modulate(rms_norm(x, norm_ffn[l]), m_lat[3], m_lat[4])
        if not last:
            ctx = ctx + m_ctx[2] * y_c
            h_c = modulate(rms_norm(ctx, norm_ffn[l]), m_ctx[3], m_ctx[4])
            h = jnp.concatenate([h_c, h], axis=1)
        if l % 2 == 0:
            f = dense_swiglu(h, w1_dense[l // 2], w3_dense[l // 2], w2_dense[l // 2])
        else:
            f = moe_swiglu(h, w_router[l // 2], w1_moe[l // 2], w3_moe[l // 2], w2_moe[l // 2])
        if not last:
            ctx = ctx + m_ctx[5] * f[:, :CTX_LEN]
            f = f[:, CTX_LEN:]
        x = x + m_lat[5] * f
    return rms_norm(x, final_norm)
```

```python
import functools

import numpy as np
import jax
import jax.numpy as jnp
from jax import lax
from jax.experimental import pallas as pl
from jax.experimental.pallas import tpu as pltpu

D_MODEL = 2048
BATCH = 2
SEQ = 4096
DEPTH = 2
CTX_LEN = 256
GRID_W = 64
EPS = 1e-6

N_HEADS = D_MODEL // 128
Q_LORA = 512
KV_LORA = 512
NOPE_DIM = 128
ROPE_DIM = 64
V_DIM = 128
ROPE_BASE = 10000.0
SM_SCALE = (NOPE_DIM + ROPE_DIM) ** -0.5

D_INNER = 2 * D_MODEL
SSM_HEAD_DIM = 64
SSM_HEADS = D_INNER // SSM_HEAD_DIM
SSM_GROUPS = 8
D_STATE = 128
CONV_K = 5
CONV_DIM = D_INNER + 2 * SSM_GROUPS * D_STATE
CHUNK = 128

PROJ_SIZES = (Q_LORA, KV_LORA, ROPE_DIM, D_INNER, CONV_DIM, 2 * SSM_HEADS, D_MODEL, D_MODEL)

D_FF = 256 * ((8 * D_MODEL // 3 + 255) // 256)
N_EXPERTS = 8
TOP_K = 2
D_FF_EXPERT = 7 * D_MODEL // 2

F32 = jnp.float32
BF16 = jnp.bfloat16
HIGHEST = lax.Precision.HIGHEST

VMEM_LIMIT_BYTES = 56 * 1024 * 1024
LANES = 128
HALO_ROWS = 8
LOG2E = 1.4426950408889634

ROWS_LAT = BATCH * SEQ
ROWS_CTX = BATCH * CTX_LEN
ROWS = ROWS_LAT + ROWS_CTX
TM = 512
TILES_PER_BATCH = SEQ // TM
N_LAT_TILES = ROWS_LAT // TM
CHUNKS_LAT = SEQ // CHUNK
CHUNKS_CTX = CTX_LEN // CHUNK
CHUNKS_SEQ = CHUNKS_LAT + CHUNKS_CTX

COL_XBC = 0
COL_Z = COL_XBC + CONV_DIM
COL_GA = COL_Z + D_INNER
COL_GB = COL_GA + D_MODEL
COL_CQ = COL_GB + D_MODEL
COL_CKV = COL_CQ + Q_LORA
N_MAIN = COL_CKV + KV_LORA
HEAD_Q = 2 * LANES

MOE_ROW_TILE = 512
MOE_FF_TILE = 512
MOE_SLOTS = ROWS_LAT * TOP_K + N_EXPERTS * MOE_ROW_TILE
GATHER_ROWS = 256
ATTN_Q_TILE = 512
ATTN_KV_CHUNK = 512
ROPE_SWAP = np.concatenate([np.arange(16, 32), np.arange(0, 16), np.arange(48, 64), np.arange(32, 48)])


def _params(*sem):
    return pltpu.CompilerParams(dimension_semantics=sem, vmem_limit_bytes=VMEM_LIMIT_BYTES)


def _largest_tile(n, candidates):
    for c in candidates:
        if n % c == 0:
            return c
    raise ValueError(f"no tile for {n}")


def _mod_row(i):
    return jnp.minimum(i // TILES_PER_BATCH, BATCH)


def _rope_row(i):
    return jnp.where(i < N_LAT_TILES, i % TILES_PER_BATCH, TILES_PER_BATCH)


def _mod_spec(layer, k):
    return pl.BlockSpec((None, None, None, 1, D_MODEL), lambda i, *_: (layer, _mod_row(i), k, 0, 0))


def _rms(x):
    return x * lax.rsqrt(jnp.mean(x * x, axis=-1, keepdims=True) + EPS)


def _silu(x):
    return x * jax.nn.sigmoid(x)


def _rope(x, tab):
    y = x * tab
    y = y + pltpu.roll(y, ROPE_DIM, axis=1)
    lane = lax.broadcasted_iota(jnp.int32, y.shape, 1)
    return jnp.where(lane < ROPE_DIM, y, 0.0)


def _adaln_kernel(c_ref, w_ref, b_ref, o_ref):
    a = _silu(c_ref[...]).astype(BF16)
    o_ref[...] = jnp.dot(a, w_ref[...].astype(BF16), preferred_element_type=F32) + b_ref[...]


def adaln(cc, w_ada, b_ada):
    tn = 1024
    n = 6 * D_MODEL
    out = pl.pallas_call(
        _adaln_kernel,
        out_shape=jax.ShapeDtypeStruct((DEPTH, 8, n), F32),
        grid=(DEPTH, n // tn),
        in_specs=[pl.BlockSpec((8, D_MODEL), lambda l, j: (0, 0)),
                  pl.BlockSpec((None, D_MODEL, tn), lambda l, j: (l, 0, j)),
                  pl.BlockSpec((None, 1, tn), lambda l, j: (l, 0, j))],
        out_specs=pl.BlockSpec((None, 8, tn), lambda l, j: (l, 0, j)),
        compiler_params=_params("parallel", "parallel"),
        name="adaln",
    )(cc, w_ada, b_ada.reshape(DEPTH, 1, n))
    return out.reshape(DEPTH, 8, 6, 1, D_MODEL)


def _norm_mod_kernel(x_ref, g_ref, sh_ref, sc_ref, o_ref):
    y = _rms(x_ref[...]) * g_ref[...]
    o_ref[...] = (y * (1 + sc_ref[...]) + sh_ref[...]).astype(o_ref.dtype)


def norm_mod(x, g, mods, layer, k_shift, k_scale):
    r = x.shape[0]
    return pl.pallas_call(
        _norm_mod_kernel,
        out_shape=jax.ShapeDtypeStruct((r, D_MODEL), BF16),
        grid=(r // TM,),
        in_specs=[pl.BlockSpec((TM, D_MODEL), lambda i: (i, 0)),
                  pl.BlockSpec((1, D_MODEL), lambda i: (0, 0)),
                  _mod_spec(layer, k_shift), _mod_spec(layer, k_scale)],
        out_specs=pl.BlockSpec((TM, D_MODEL), lambda i: (i, 0)),
        compiler_params=_params("parallel"),
        name="norm_mod",
    )(x, g.reshape(1, D_MODEL), mods, mods)


def _mm_kernel(a_ref, b_ref, o_ref):
    o_ref[...] = jnp.dot(a_ref[...], b_ref[...], preferred_element_type=F32).astype(o_ref.dtype)


def matmul(a, b, out_dtype):
    m, k = a.shape
    n = b.shape[1]
    tn = _largest_tile(n, (1024, 512, 256, 128))
    return pl.pallas_call(
        _mm_kernel,
        out_shape=jax.ShapeDtypeStruct((m, n), out_dtype),
        grid=(m // TM, n // tn),
        in_specs=[pl.BlockSpec((TM, k), lambda i, j: (i, 0)),
                  pl.BlockSpec((k, tn), lambda i, j: (0, j))],
        out_specs=pl.BlockSpec((TM, tn), lambda i, j: (i, j)),
        compiler_params=_params("parallel", "parallel"),
        name="matmul",
    )(a, b)


def _proj_small_kernel(a_ref, b_ref, tab_ref, dt_ref, kr_ref):
    acc = jnp.dot(a_ref[...], b_ref[...], preferred_element_type=F32)
    dt_ref[...] = acc[:, :LANES]
    kr_ref[...] = _rope(acc[:, LANES:], tab_ref[...]).astype(kr_ref.dtype)


def proj_small(h, w_small, tab):
    return pl.pallas_call(
        _proj_small_kernel,
        out_shape=(jax.ShapeDtypeStruct((ROWS, LANES), F32), jax.ShapeDtypeStruct((ROWS, LANES), BF16)),
        grid=(ROWS // TM,),
        in_specs=[pl.BlockSpec((TM, D_MODEL), lambda i: (i, 0)),
                  pl.BlockSpec((D_MODEL, 2 * LANES), lambda i: (0, 0)),
                  pl.BlockSpec((TM, LANES), lambda i: (_rope_row(i), 0))],
        out_specs=(pl.BlockSpec((TM, LANES), lambda i: (i, 0)),
                   pl.BlockSpec((TM, LANES), lambda i: (i, 0))),
        compiler_params=_params("parallel"),
        name="proj_small",
    )(h, w_small, tab)


def _up_kernel(c_ref, g_ref, w_ref, *rest, rope, heads_per_tile):
    if rope:
        tab_ref, o_ref = rest
    else:
        (o_ref,) = rest
    c = _rms(c_ref[...].astype(F32)) * g_ref[...]
    acc = jnp.dot(c.astype(BF16), w_ref[...], preferred_element_type=F32)
    if not rope:
        o_ref[...] = acc.astype(o_ref.dtype)
        return
    tab = tab_ref[...]
    for hh in range(heads_per_tile):
        base = hh * HEAD_Q
        o_ref[:, base:base + LANES] = acc[:, base:base + LANES].astype(o_ref.dtype)
        o_ref[:, base + LANES:base + HEAD_Q] = _rope(acc[:, base + LANES:base + HEAD_Q], tab).astype(o_ref.dtype)


def up_proj(p_main, col, g, w, tab):
    lora, n = w.shape
    tn = 1024
    rope = tab is not None
    in_specs = [pl.BlockSpec((TM, lora), lambda i, j: (i, col // lora)),
                pl.BlockSpec((1, lora), lambda i, j: (0, 0)),
                pl.BlockSpec((lora, tn), lambda i, j: (0, j))]
    args = [p_main, g.reshape(1, lora), w]
    if rope:
        in_specs.append(pl.BlockSpec((TM, LANES), lambda i, j: (_rope_row(i), 0)))
        args.append(tab)
    return pl.pallas_call(
        functools.partial(_up_kernel, rope=rope, heads_per_tile=tn // HEAD_Q),
        out_shape=jax.ShapeDtypeStruct((ROWS, n), BF16),
        grid=(ROWS // TM, n // tn),
        in_specs=in_specs,
        out_specs=pl.BlockSpec((TM, tn), lambda i, j: (i, j)),
        compiler_params=_params("parallel", "parallel"),
        name="up_proj_rope" if rope else "up_proj",
    )(*args)


def _softmax_chunks(q, kcat_ref, v_chunks):
    c2 = SM_SCALE * LOG2E
    m = l = acc = None
    for start, size, v in v_chunks:
        k = kcat_ref[start:start + size, :]
        s = lax.dot_general(q, k, (((1,), (1,)), ((), ())), preferred_element_type=F32)
        m_cur = jnp.max(s, axis=-1, keepdims=True)
        if m is None:
            m_new = m_cur
            p = jnp.exp2((s - m_new) * c2)
            l = jnp.sum(p, axis=-1, keepdims=True)
            acc = jnp.dot(p.astype(BF16), v, preferred_element_type=F32)
        else:
            m_new = jnp.maximum(m, m_cur)
            alpha = jnp.exp2((m - m_new) * c2)
            p = jnp.exp2((s - m_new) * c2)
            l = alpha * l + jnp.sum(p, axis=-1, keepdims=True)
            acc = alpha * acc + jnp.dot(p.astype(BF16), v, preferred_element_type=F32)
        m = m_new
    return acc / l


def _lat_attn_kernel(q_ref, knl_ref, krl_ref, vl_ref, knc_ref, krc_ref, vc_ref, o_ref, kcat_ref):
    @pl.when(pl.program_id(2) == 0)
    def _():
        kcat_ref[0:CTX_LEN, 0:LANES] = knc_ref[...]
        kcat_ref[0:CTX_LEN, LANES:HEAD_Q] = krc_ref[...]
        kcat_ref[CTX_LEN:, 0:LANES] = knl_ref[...]
        kcat_ref[CTX_LEN:, LANES:HEAD_Q] = krl_ref[...]

    chunks = [(0, CTX_LEN, vc_ref[...])]
    for s in range(0, SEQ, ATTN_KV_CHUNK):
        chunks.append((CTX_LEN + s, ATTN_KV_CHUNK, vl_ref[s:s + ATTN_KV_CHUNK, :]))
    o_ref[...] = _softmax_chunks(q_ref[...], kcat_ref, chunks).astype(o_ref.dtype)


def latent_attention(q, kv, kr):
    tq = ATTN_Q_TILE
    qt = SEQ // tq
    ctx_blk = ROWS_LAT // CTX_LEN
    return pl.pallas_call(
        _lat_attn_kernel,
        out_shape=jax.ShapeDtypeStruct((ROWS_LAT, D_MODEL), BF16),
        grid=(BATCH, N_HEADS, qt),
        in_specs=[pl.BlockSpec((tq, HEAD_Q), lambda b, h, i: (b * qt + i, h)),
                  pl.BlockSpec((SEQ, LANES), lambda b, h, i: (b, h)),
                  pl.BlockSpec((SEQ, LANES), lambda b, h, i: (b, 0)),
                  pl.BlockSpec((SEQ, LANES), lambda b, h, i: (b, N_HEADS + h)),
                  pl.BlockSpec((CTX_LEN, LANES), lambda b, h, i: (ctx_blk + b, h)),
                  pl.BlockSpec((CTX_LEN, LANES), lambda b, h, i: (ctx_blk + b, 0)),
                  pl.BlockSpec((CTX_LEN, LANES), lambda b, h, i: (ctx_blk + b, N_HEADS + h))],
        out_specs=pl.BlockSpec((tq, LANES), lambda b, h, i: (b * qt + i, h)),
        scratch_shapes=[pltpu.VMEM((CTX_LEN + SEQ, HEAD_Q), BF16)],
        compiler_params=_params("parallel", "parallel", "arbitrary"),
        name="latent_attention",
    )(q, kv, kr, kv, kv, kr, kv)


def _ctx_attn_kernel(q_ref, kn_ref, kr_ref, v_ref, o_ref, kcat_ref):
    kcat_ref[:, 0:LANES] = kn_ref[...]
    kcat_ref[:, LANES:HEAD_Q] = kr_ref[...]
    o_ref[...] = _softmax_chunks(q_ref[...], kcat_ref, [(0, CTX_LEN, v_ref[...])]).astype(o_ref.dtype)


def context_attention(q, kv, kr):
    ctx_blk = ROWS_LAT // CTX_LEN
    return pl.pallas_call(
        _ctx_attn_kernel,
        out_shape=jax.ShapeDtypeStruct((ROWS_CTX, D_MODEL), BF16),
        grid=(BATCH, N_HEADS),
        in_specs=[pl.BlockSpec((CTX_LEN, HEAD_Q), lambda b, h: (ctx_blk + b, h)),
                  pl.BlockSpec((CTX_LEN, LANES), lambda b, h: (ctx_blk + b, h)),
                  pl.BlockSpec((CTX_LEN, LANES), lambda b, h: (ctx_blk + b, 0)),
                  pl.BlockSpec((CTX_LEN, LANES), lambda b, h: (ctx_blk + b, N_HEADS + h))],
        out_specs=pl.BlockSpec((CTX_LEN, LANES), lambda b, h: (b, h)),
        scratch_shapes=[pltpu.VMEM((CTX_LEN, HEAD_Q), BF16)],
        compiler_params=_params("parallel", "parallel"),
        name="context_attention",
    )(q, kv, kr, kv)


def _chunk_block(b, s, reverse):
    if reverse:
        return jnp.where(s < CHUNKS_CTX, ROWS_LAT // CHUNK + CHUNKS_CTX * b + (CHUNKS_CTX - 1 - s),
                         CHUNKS_LAT * b + (CHUNKS_SEQ - 1 - s))
    return jnp.where(s < CHUNKS_CTX, ROWS_LAT // CHUNK + CHUNKS_CTX * b + s,
                     CHUNKS_LAT * b + (s - CHUNKS_CTX))


def _expand_heads(v, e3_ref):
    hi = v.astype(BF16)
    r1 = v - hi.astype(F32)
    mid = r1.astype(BF16)
    lo = (r1 - mid.astype(F32)).astype(BF16)
    return jnp.dot(jnp.concatenate([hi, mid, lo], axis=1), e3_ref[...], preferred_element_type=F32)


def _ssd_chunk(xbc, dt_raw, dtb_ref, a_ref, e3_ref, state_ref, reverse):
    off = SSM_HEADS if reverse else 0
    row = lax.broadcasted_iota(jnp.int32, (CHUNK, CHUNK), 0)
    colm = lax.broadcasted_iota(jnp.int32, (CHUNK, CHUNK), 1)
    keep = (colm >= row) if reverse else (colm <= row)
    tri = keep.astype(F32)
    dt = jax.nn.softplus(dt_raw + dtb_ref[...])
    a = dt * a_ref[...]
    cum = jnp.dot(tri, a, precision=HIGHEST, preferred_element_type=F32)
    cum_t = cum.T
    dt_full = _expand_heads(dt[:, off:off + SSM_HEADS], e3_ref)
    cum_full = _expand_heads(cum[:, off:off + SSM_HEADS], e3_ref)
    last_row = 0 if reverse else CHUNK - 1
    last = cum_full[last_row:last_row + 1, :]
    xc = xbc[:, :D_INNER] * dt_full
    dec_in = jnp.exp(cum_full)
    xce = (xc * jnp.exp(last - cum_full)).astype(BF16)
    xcb = xc.astype(BF16)
    chunk_dec = jnp.exp(last)
    gw = SSM_HEADS // SSM_GROUPS * SSM_HEAD_DIM
    lane = lax.broadcasted_iota(jnp.int32, (CHUNK, LANES), 1)
    ys = []
    for g in range(SSM_GROUPS):
        b32 = xbc[:, D_INNER + g * D_STATE:D_INNER + (g + 1) * D_STATE]
        bg = b32.astype(BF16)
        cg = xbc[:, D_INNER + (SSM_GROUPS + g) * D_STATE:D_INNER + (SSM_GROUPS + g + 1) * D_STATE].astype(BF16)
        cb = lax.dot_general(cg, bg, (((1,), (1,)), ((), ())), preferred_element_type=F32)
        st = state_ref[g]
        y_off = jnp.dot(cg, st.astype(BF16), preferred_element_type=F32) * dec_in[:, g * gw:(g + 1) * gw]
        upd = jnp.dot(b32.T.astype(BF16), xce[:, g * gw:(g + 1) * gw],
                      preferred_element_type=F32)
        state_ref[g] = st * chunk_dec[:, g * gw:(g + 1) * gw] + upd
        pairs = []
        for j in range(gw // LANES):
            h0 = off + g * (SSM_HEADS // SSM_GROUPS) + 2 * j
            ms = []
            for hh in (h0, h0 + 1):
                seg = jnp.where(keep, cum[:, hh:hh + 1] - cum_t[hh:hh + 1, :], -jnp.inf)
                ms.append((jnp.exp(seg) * cb).astype(BF16))
            xp = xcb[:, g * gw + j * LANES:g * gw + (j + 1) * LANES]
            rhs = jnp.concatenate([jnp.where(lane < SSM_HEAD_DIM, xp, 0), jnp.where(lane >= SSM_HEAD_DIM, xp, 0)],
                                  axis=0)
            pairs.append(jnp.dot(jnp.concatenate(ms, axis=1), rhs, preferred_element_type=F32))
        ys.append(jnp.concatenate(pairs, axis=1) + y_off)
    return jnp.concatenate(ys, axis=1)


def _ssd_fwd_kernel(prev_ref, cur_ref, next_ref, dt_ref, cw_ref, cbias_ref, dtb_ref, a_ref, e3_ref,
                    y_ref, xbc_ref, state_ref, ext_ref):
    s = pl.program_id(1)

    @pl.when(s == 0)
    def _():
        state_ref[...] = jnp.zeros_like(state_ref)

    first = (s == 0) | (s == CHUNKS_CTX)
    final = (s == CHUNKS_CTX - 1) | (s == CHUNKS_SEQ - 1)
    ext_ref[0:HALO_ROWS, :] = jnp.where(first, 0.0, prev_ref[...].astype(F32))
    ext_ref[HALO_ROWS:HALO_ROWS + CHUNK, :] = cur_ref[...].astype(F32)
    ext_ref[HALO_ROWS + CHUNK:, :] = jnp.where(final, 0.0, next_ref[...].astype(F32))
    acc = cbias_ref[...] + cw_ref[0:1, :] * ext_ref[pl.ds(HALO_ROWS - CONV_K // 2, CHUNK), :]
    for k in range(1, CONV_K):
        acc = acc + cw_ref[k:k + 1, :] * ext_ref[pl.ds(HALO_ROWS - CONV_K // 2 + k, CHUNK), :]
    xbc = _silu(acc)
    xbc_ref[...] = xbc.astype(xbc_ref.dtype)
    y_ref[...] = _ssd_chunk(xbc, dt_ref[...], dtb_ref, a_ref, e3_ref, state_ref, reverse=False)


def _ssd_bwd_kernel(xbc_ref, dt_ref, yf_ref, z0_ref, z1_ref, dtb_ref, a_ref, e3_ref, dsum_ref, nw_ref,
                    u_ref, state_ref):
    @pl.when(pl.program_id(1) == 0)
    def _():
        state_ref[...] = jnp.zeros_like(state_ref)

    xbc = xbc_ref[...].astype(F32)
    y = _ssd_chunk(xbc, dt_ref[...], dtb_ref, a_ref, e3_ref, state_ref, reverse=True)
    y = y + yf_ref[...] + dsum_ref[...] * xbc[:, :D_INNER]
    z = jnp.concatenate([z0_ref[...], z1_ref[...]], axis=1).astype(F32)
    u = y * _silu(z)
    gw = D_INNER // SSM_GROUPS
    for g in range(SSM_GROUPS):
        ug = u[:, g * gw:(g + 1) * gw]
        u_ref[:, g * gw:(g + 1) * gw] = (_rms(ug) * nw_ref[:, g * gw:(g + 1) * gw]).astype(u_ref.dtype)


def ssd_mixer(p_main, dt_raw, conv_w, conv_b, dt_bias, a_log, d_skip, ssm_norm, e3):
    zb = COL_Z // (D_INNER // 2)
    dtb = dt_bias.reshape(1, 2 * SSM_HEADS).astype(F32)
    a_neg = (-jnp.exp(a_log.astype(F32))).reshape(1, 2 * SSM_HEADS)
    dsum = jnp.repeat(d_skip[0] + d_skip[1], SSM_HEAD_DIM).reshape(1, D_INNER)
    halo_per_chunk = CHUNK // HALO_ROWS
    n_halo = ROWS // HALO_ROWS
    full = lambda shape: pl.BlockSpec(shape, lambda b, s: tuple(0 for _ in shape))

    def blk(reverse):
        return lambda b, s: (_chunk_block(b, s, reverse), 0)

    fwd_blk = blk(False)
    y_f, xbc = pl.pallas_call(
        _ssd_fwd_kernel,
        out_shape=(jax.ShapeDtypeStruct((ROWS, D_INNER), F32), jax.ShapeDtypeStruct((ROWS, CONV_DIM), BF16)),
        grid=(BATCH, CHUNKS_SEQ),
        in_specs=[
            pl.BlockSpec((HALO_ROWS, CONV_DIM),
                         lambda b, s: (jnp.maximum(_chunk_block(b, s, False) * halo_per_chunk - 1, 0), 0)),
            pl.BlockSpec((CHUNK, CONV_DIM), fwd_blk),
            pl.BlockSpec((HALO_ROWS, CONV_DIM),
                         lambda b, s: (jnp.minimum((_chunk_block(b, s, False) + 1) * halo_per_chunk, n_halo - 1), 0)),
            pl.BlockSpec((CHUNK, LANES), fwd_blk),
            full((CONV_K, CONV_DIM)), full((1, CONV_DIM)), full((1, LANES)), full((1, LANES)),
            full((3 * SSM_HEADS, D_INNER)),
        ],
        out_specs=(pl.BlockSpec((CHUNK, D_INNER), fwd_blk), pl.BlockSpec((CHUNK, CONV_DIM), fwd_blk)),
        scratch_shapes=[pltpu.VMEM((SSM_GROUPS, D_STATE, D_INNER // SSM_GROUPS), F32),
                        pltpu.VMEM((CHUNK + 2 * HALO_ROWS, CONV_DIM), F32)],
        compiler_params=_params("parallel", "arbitrary"),
        name="ssd_forward",
    )(p_main, p_main, p_main, dt_raw, conv_w, conv_b.reshape(1, CONV_DIM), dtb, a_neg, e3)
    bwd_blk = blk(True)
    return pl.pallas_call(
        _ssd_bwd_kernel,
        out_shape=jax.ShapeDtypeStruct((ROWS, D_INNER), BF16),
        grid=(BATCH, CHUNKS_SEQ),
        in_specs=[
            pl.BlockSpec((CHUNK, CONV_DIM), bwd_blk),
            pl.BlockSpec((CHUNK, LANES), bwd_blk),
            pl.BlockSpec((CHUNK, D_INNER), bwd_blk),
            pl.BlockSpec((CHUNK, D_INNER // 2), lambda b, s: (_chunk_block(b, s, True), zb)),
            pl.BlockSpec((CHUNK, D_INNER // 2), lambda b, s: (_chunk_block(b, s, True), zb + 1)),
            full((1, LANES)), full((1, LANES)), full((3 * SSM_HEADS, D_INNER)),
            full((1, D_INNER)), full((1, D_INNER)),
        ],
        out_specs=pl.BlockSpec((CHUNK, D_INNER), bwd_blk),
        scratch_shapes=[pltpu.VMEM((SSM_GROUPS, D_STATE, D_INNER // SSM_GROUPS), F32)],
        compiler_params=_params("parallel", "arbitrary"),
        name="ssd_backward",
    )(xbc, dt_raw, y_f, p_main, p_main, dtb, a_neg, e3, dsum, ssm_norm.reshape(1, D_INNER))


def _merge_kernel(attl_ref, attc_ref, u_ref, woa_ref, wob_ref, ga_ref, gb_ref, o_ref):
    att = jnp.where(pl.program_id(0) < N_LAT_TILES, attl_ref[...], attc_ref[...])
    o_a = jnp.dot(att, woa_ref[...], preferred_element_type=F32)
    o_b = jnp.dot(u_ref[...], wob_ref[...], preferred_element_type=F32)
    o_ref[...] = (jax.nn.sigmoid(ga_ref[...].astype(F32)) * o_a
                  + jax.nn.sigmoid(gb_ref[...].astype(F32)) * o_b).astype(o_ref.dtype)


def merge_branches(att_lat, att_ctx, u, p_main, w_oa, w_ob, rows):
    tn = 512
    return pl.pallas_call(
        _merge_kernel,
        out_shape=jax.ShapeDtypeStruct((rows, D_MODEL), BF16),
        grid=(rows // TM, D_MODEL // tn),
        in_specs=[pl.BlockSpec((TM, D_MODEL), lambda i, j: (jnp.minimum(i, N_LAT_TILES - 1), 0)),
                  pl.BlockSpec((TM, D_MODEL), lambda i, j: (0, 0)),
                  pl.BlockSpec((TM, D_INNER), lambda i, j: (i, 0)),
                  pl.BlockSpec((D_MODEL, tn), lambda i, j: (0, j)),
                  pl.BlockSpec((D_INNER, tn), lambda i, j: (0, j)),
                  pl.BlockSpec((TM, tn), lambda i, j: (i, COL_GA // tn + j)),
                  pl.BlockSpec((TM, tn), lambda i, j: (i, COL_GB // tn + j))],
        out_specs=pl.BlockSpec((TM, tn), lambda i, j: (i, j)),
        compiler_params=_params("parallel", "parallel"),
        name="merge_branches",
    )(att_lat, att_ctx, u, w_oa, w_ob, p_main, p_main)


def _route(h, wr_ref):
    logits = jnp.dot(h, wr_ref[...], precision=HIGHEST, preferred_element_type=F32)
    lane = lax.broadcasted_iota(jnp.int32, logits.shape, 1)
    lg = jnp.where(lane < N_EXPERTS, logits, -jnp.inf)
    v1 = jnp.max(lg, axis=-1, keepdims=True)
    i1 = jnp.min(jnp.where(lg == v1, lane, LANES), axis=-1, keepdims=True)
    lg2 = jnp.where(lane == i1, -jnp.inf, lg)
    v2 = jnp.max(lg2, axis=-1, keepdims=True)
    i2 = jnp.min(jnp.where(lg2 == v2, lane, LANES), axis=-1, keepdims=True)
    e = jnp.exp(v2 - v1)
    g1 = 1.0 / (1.0 + e)
    g2 = e / (1.0 + e)
    return jnp.where(lane == 0, i1.astype(F32),
                     jnp.where(lane == 1, i2.astype(F32),
                               jnp.where(lane == 2, g1, jnp.where(lane == 3, g2, 0.0))))


def _out_res_kernel(m_ref, w_ref, x_ref, gate_ref, g2_ref, sh_ref, sc_ref, *rest, route):
    if route:
        wr_ref, xo_ref, ho_ref, ro_ref = rest
    else:
        xo_ref, ho_ref = rest
    acc = jnp.dot(m_ref[...], w_ref[...], preferred_element_type=F32)
    xn = x_ref[...] + gate_ref[...] * acc
    xo_ref[...] = xn
    h = (_rms(xn) * g2_ref[...]) * (1 + sc_ref[...]) + sh_ref[...]
    ho_ref[...] = h.astype(ho_ref.dtype)
    if route:
        ro_ref[...] = _route(h, wr_ref)


def out_proj_residual(mixed, w_out, x, g2, mods, layer, rows, w_router=None):
    route = w_router is not None
    h_dtype = F32 if route else BF16
    in_specs = [pl.BlockSpec((TM, D_MODEL), lambda i: (i, 0)),
                pl.BlockSpec((D_MODEL, D_MODEL), lambda i: (0, 0)),
                pl.BlockSpec((TM, D_MODEL), lambda i: (i, 0)),
                _mod_spec(layer, 2), pl.BlockSpec((1, D_MODEL), lambda i: (0, 0)),
                _mod_spec(layer, 3), _mod_spec(layer, 4)]
    args = [mixed, w_out, x, mods, g2.reshape(1, D_MODEL), mods, mods]
    out_shape = [jax.ShapeDtypeStruct((rows, D_MODEL), F32), jax.ShapeDtypeStruct((rows, D_MODEL), h_dtype)]
    out_specs = [pl.BlockSpec((TM, D_MODEL), lambda i: (i, 0)), pl.BlockSpec((TM, D_MODEL), lambda i: (i, 0))]
    if route:
        in_specs.append(pl.BlockSpec((D_MODEL, LANES), lambda i: (0, 0)))
        args.append(w_router)
        out_shape.append(jax.ShapeDtypeStruct((rows, LANES), F32))
        out_specs.append(pl.BlockSpec((TM, LANES), lambda i: (i, 0)))
    return pl.pallas_call(
        functools.partial(_out_res_kernel, route=route),
        out_shape=tuple(out_shape),
        grid=(rows // TM,),
        in_specs=in_specs,
        out_specs=tuple(out_specs),
        compiler_params=_params("parallel"),
        name="out_proj_residual",
    )(*args)


def _ffn_up_kernel(h_ref, w1_ref, w3_ref, o_ref):
    h = h_ref[...]
    a = jnp.dot(h, w1_ref[...], preferred_element_type=F32)
    b = jnp.dot(h, w3_ref[...], preferred_element_type=F32)
    o_ref[...] = (_silu(a) * b).astype(o_ref.dtype)


def ffn_up(h, w1, w3):
    tn = 512
    return pl.pallas_call(
        _ffn_up_kernel,
        out_shape=jax.ShapeDtypeStruct((ROWS, D_FF), BF16),
        grid=(ROWS // TM, D_FF // tn),
        in_specs=[pl.BlockSpec((TM, D_MODEL), lambda i, j: (i, 0)),
                  pl.BlockSpec((D_MODEL, tn), lambda i, j: (0, j)),
                  pl.BlockSpec((D_MODEL, tn), lambda i, j: (0, j))],
        out_specs=pl.BlockSpec((TM, tn), lambda i, j: (i, j)),
        compiler_params=_params("parallel", "parallel"),
        name="ffn_up",
    )(h, w1, w3)


def _ffn_down_kernel(g_ref, w_ref, x_ref, gate_ref, o_ref):
    acc = jnp.dot(g_ref[...], w_ref[...], preferred_element_type=F32)
    o_ref[...] = x_ref[...] + gate_ref[...] * acc


def ffn_down_residual(g, w2, x, mods, layer):
    tn = 512
    gate_spec = pl.BlockSpec((None, None, None, 1, tn), lambda i, j: (layer, _mod_row(i), 5, 0, j))
    return pl.pallas_call(
        _ffn_down_kernel,
        out_shape=jax.ShapeDtypeStruct((ROWS, D_MODEL), F32),
        grid=(ROWS // TM, D_MODEL // tn),
        in_specs=[pl.BlockSpec((TM, D_FF), lambda i, j: (i, 0)),
                  pl.BlockSpec((D_FF, tn), lambda i, j: (0, j)),
                  pl.BlockSpec((TM, tn), lambda i, j: (i, j)),
                  gate_spec],
        out_specs=pl.BlockSpec((TM, tn), lambda i, j: (i, j)),
        compiler_params=_params("parallel", "parallel"),
        name="ffn_down_residual",
    )(g, w2, x, mods)


def _gather_rows_kernel(src_ref, h_hbm, o_hbm, sem):
    base = pl.program_id(0) * GATHER_ROWS

    def row_copy(j):
        return pltpu.make_async_copy(h_hbm.at[pl.ds(src_ref[base + j], 1)], o_hbm.at[pl.ds(base + j, 1)], sem)

    def start(j, c):
        row_copy(j).start()
        return c

    def wait(j, c):
        row_copy(j).wait()
        return c

    lax.fori_loop(0, GATHER_ROWS, start, 0)
    lax.fori_loop(0, GATHER_ROWS, wait, 0)


def gather_rows(h, src):
    n = src.shape[0]
    grid_spec = pltpu.PrefetchScalarGridSpec(
        num_scalar_prefetch=1, grid=(n // GATHER_ROWS,),
        in_specs=[pl.BlockSpec(memory_space=pl.ANY)],
        out_specs=pl.BlockSpec(memory_space=pl.ANY),
        scratch_shapes=[pltpu.SemaphoreType.DMA(())])
    return pl.pallas_call(
        _gather_rows_kernel,
        out_shape=jax.ShapeDtypeStruct((n, h.shape[1]), h.dtype),
        grid_spec=grid_spec,
        compiler_params=_params("arbitrary"),
        name="moe_gather",
    )(src, h)


def _moe_kernel(be_ref, nused_ref, x_ref, w1_ref, w3_ref, w2_ref, o_ref, xb_ref):
    i = pl.program_id(0)
    f = pl.program_id(1)

    @pl.when(f == 0)
    def _():
        o_ref[...] = jnp.zeros_like(o_ref)
        xb_ref[...] = x_ref[...].astype(BF16)

    @pl.when(i < nused_ref[0])
    def _():
        x = xb_ref[...]
        h1 = jnp.dot(x, w1_ref[...], preferred_element_type=F32)
        h3 = jnp.dot(x, w3_ref[...], preferred_element_type=F32)
        g = _silu(h1) * h3
        o_ref[...] += jnp.dot(g.astype(BF16), w2_ref[...], preferred_element_type=F32)


def moe_experts(buf, block_expert, n_used, w1, w3, w2):
    r, d = buf.shape
    tm, tf = MOE_ROW_TILE, MOE_FF_TILE
    nf = D_FF_EXPERT // tf

    def f_idx(i, f, nu):
        return jnp.where(i < nu[0], f, nf - 1)

    grid_spec = pltpu.PrefetchScalarGridSpec(
        num_scalar_prefetch=2,
        grid=(r // tm, nf),
        in_specs=[
            pl.BlockSpec((tm, d), lambda i, f, be, nu: (jnp.minimum(i, nu[0] - 1), 0)),
            pl.BlockSpec((None, d, tf), lambda i, f, be, nu: (be[i], 0, f_idx(i, f, nu))),
            pl.BlockSpec((None, d, tf), lambda i, f, be, nu: (be[i], 0, f_idx(i, f, nu))),
            pl.BlockSpec((None, tf, d), lambda i, f, be, nu: (be[i], f_idx(i, f, nu), 0)),
        ],
        out_specs=pl.BlockSpec((tm, d), lambda i, f, be, nu: (i, 0)),
        scratch_shapes=[pltpu.VMEM((tm, d), BF16)],
    )
    return pl.pallas_call(
        _moe_kernel,
        out_shape=jax.ShapeDtypeStruct((r, d), F32),
        grid_spec=grid_spec,
        compiler_params=_params("arbitrary", "arbitrary"),
        name="moe_experts",
    )(block_expert, n_used, buf, w1, w3, w2)


def _combine_kernel(dest_ref, y_hbm, x_ref, route_ref, gate_ref, gfin_ref, o_ref, ybuf_ref, sem):
    base = pl.program_id(0) * GATHER_ROWS

    def row_copy(t, k):
        return pltpu.make_async_copy(y_hbm.at[pl.ds(dest_ref[TOP_K * (base + t) + k], 1)],
                                     ybuf_ref.at[k, pl.ds(t, 1)], sem)

    def start(t, c):
        for k in range(TOP_K):
            row_copy(t, k).start()
        return c

    def wait(t, c):
        for k in range(TOP_K):
            row_copy(t, k).wait()
        return c

    lax.fori_loop(0, GATHER_ROWS, start, 0)
    lax.fori_loop(0, GATHER_ROWS, wait, 0)
    route = route_ref[...]
    f = ybuf_ref[0] * route[:, 2:3] + ybuf_ref[1] * route[:, 3:4]
    xn = x_ref[...] + gate_ref[...] * f
    o_ref[...] = _rms(xn) * gfin_ref[...]


def moe_combine_final(y, dest, x, route, mods, layer, final_norm):
    tiles_per_batch = SEQ // GATHER_ROWS
    grid_spec = pltpu.PrefetchScalarGridSpec(
        num_scalar_prefetch=1, grid=(ROWS_LAT // GATHER_ROWS,),
        in_specs=[pl.BlockSpec(memory_space=pl.ANY),
                  pl.BlockSpec((GATHER_ROWS, D_MODEL), lambda i, d: (i, 0)),
                  pl.BlockSpec((GATHER_ROWS, LANES), lambda i, d: (i, 0)),
                  pl.BlockSpec((None, None, None, 1, D_MODEL), lambda i, d: (layer, i // tiles_per_batch, 5, 0, 0)),
                  pl.BlockSpec((1, D_MODEL), lambda i, d: (0, 0))],
        out_specs=pl.BlockSpec((GATHER_ROWS, D_MODEL), lambda i, d: (i, 0)),
        scratch_shapes=[pltpu.VMEM((TOP_K, GATHER_ROWS, D_MODEL), F32), pltpu.SemaphoreType.DMA(())])
    return pl.pallas_call(
        _combine_kernel,
        out_shape=jax.ShapeDtypeStruct((ROWS_LAT, D_MODEL), F32),
        grid_spec=grid_spec,
        compiler_params=_params("arbitrary"),
        name="moe_combine_final",
    )(dest, y, x, route, mods, final_norm.reshape(1, D_MODEL))


def moe_dispatch(route):
    expert = route[:, :TOP_K].astype(jnp.int32).reshape(-1)
    onehot = (expert[:, None] == jnp.arange(N_EXPERTS)[None, :]).astype(jnp.int32)
    incl = jnp.cumsum(onehot, axis=0)
    rank = jnp.sum((incl - onehot) * onehot, axis=1)
    counts = incl[-1]
    tm = MOE_ROW_TILE
    padded = (counts + tm - 1) // tm * tm
    pad_end = jnp.cumsum(padded)
    dest = ((pad_end - padded)[expert] + rank).astype(jnp.int32)
    n_blocks = MOE_SLOTS // tm
    n_used = (pad_end[-1] // tm).astype(jnp.int32)
    blocks = jnp.arange(n_blocks)
    block_expert = jnp.minimum(jnp.searchsorted(pad_end, blocks * tm, side='right'), N_EXPERTS - 1).astype(jnp.int32)
    block_expert = jnp.where(blocks < n_used, block_expert, block_expert[jnp.maximum(n_used - 1, 0)])
    token = jnp.repeat(jnp.arange(ROWS_LAT, dtype=jnp.int32), TOP_K)
    src = jnp.zeros((MOE_SLOTS,), jnp.int32).at[dest].set(token)
    return dest, src, block_expert, n_used.reshape(1)


def _rope_table(n):
    rows = n // GRID_W
    row = jnp.repeat(jnp.arange(rows, dtype=F32), GRID_W)
    col = jnp.tile(jnp.arange(GRID_W, dtype=F32), rows)
    axis_dim = ROPE_DIM // 2
    inv = ROPE_BASE ** (-jnp.arange(0, axis_dim, 2, dtype=F32) / axis_dim)
    ang_r, ang_c = row[:, None] * inv, col[:, None] * inv
    cr, sr, cc, sc = jnp.cos(ang_r), jnp.sin(ang_r), jnp.cos(ang_c), jnp.sin(ang_c)
    lat = jnp.concatenate([cr, cr, cc, cc, -sr, sr, -sc, sc], axis=1)
    ident = jnp.concatenate([jnp.ones((TM, ROPE_DIM), F32), jnp.zeros((TM, ROPE_DIM), F32)], axis=1)
    return jnp.concatenate([lat, ident], axis=0)


def _layer_weights(w_in, w_uq, w_ukv):
    o = np.cumsum((0,) + PROJ_SIZES)
    seg = lambda i: w_in[:, int(o[i]):int(o[i + 1])]
    w_main = jnp.concatenate([seg(4), seg(3), seg(6), seg(7), seg(0), seg(1)], axis=1).astype(BF16)
    kr_w = seg(2)
    w_small = jnp.concatenate([seg(5), kr_w, kr_w[:, ROPE_SWAP]], axis=1).astype(BF16)
    wq = w_uq.reshape(Q_LORA, N_HEADS, NOPE_DIM + ROPE_DIM)
    rope_w = wq[:, :, NOPE_DIM:]
    w_q = jnp.concatenate([wq[:, :, :NOPE_DIM], rope_w, rope_w[:, :, ROPE_SWAP]], axis=2)
    w_q = w_q.reshape(Q_LORA, N_HEADS * HEAD_Q).astype(BF16)
    wkv = w_ukv.reshape(KV_LORA, N_HEADS, NOPE_DIM + V_DIM)
    w_kv = jnp.concatenate([wkv[:, :, :NOPE_DIM].reshape(KV_LORA, -1), wkv[:, :, NOPE_DIM:].reshape(KV_LORA, -1)],
                           axis=1).astype(BF16)
    return w_main, w_small, w_q, w_kv


def kernel(x, c, ctx, c_ctx, norm_mix, norm_ffn, w_ada, b_ada, w_in, q_norm, w_uq, kv_norm, w_ukv,
           conv_w, conv_b, a_log, dt_bias, d_skip, ssm_norm, w_oa, w_ob, w_out, w1_dense, w3_dense,
           w2_dense, w_router, w1_moe, w3_moe, w2_moe, final_norm):
    tab = _rope_table(SEQ)
    cc = jnp.concatenate([c, c_ctx[None], jnp.zeros((8 - BATCH - 1, D_MODEL), F32)], axis=0)
    mods = adaln(cc, w_ada, b_ada)
    head_of = np.arange(D_INNER) // SSM_HEAD_DIM
    e1 = (np.arange(SSM_HEADS)[:, None] == head_of[None, :]).astype(np.float32)
    e3 = jnp.asarray(np.concatenate([e1, e1, e1], axis=0), dtype=BF16)
    xr = jnp.concatenate([x.reshape(ROWS_LAT, D_MODEL), ctx.reshape(ROWS_CTX, D_MODEL)], axis=0)
    out = None
    for l in range(DEPTH):
        last = l == DEPTH - 1
        rows = ROWS_LAT if last else ROWS
        w_main, w_small, w_q, w_kv = _layer_weights(w_in[l], w_uq[l], w_ukv[l])
        h = norm_mod(xr, norm_mix[l], mods, l, 0, 1)
        p_main = matmul(h, w_main, BF16)
        dt_raw, kr = proj_small(h, w_small, tab)
        q = up_proj(p_main, COL_CQ, q_norm[l], w_q, tab)
        kv = up_proj(p_main, COL_CKV, kv_norm[l], w_kv, None)
        att = latent_attention(q, kv, kr)
        att_ctx = att if last else context_attention(q, kv, kr)
        u = ssd_mixer(p_main, dt_raw, conv_w[l], conv_b[l], dt_bias[l], a_log[l], d_skip[l], ssm_norm[l], e3)
        mixed = merge_branches(att, att_ctx, u, p_main, w_oa[l].astype(BF16), w_ob[l].astype(BF16), rows)
        if l % 2 == 0:
            xr, h2 = out_proj_residual(mixed, w_out[l].astype(BF16), xr, norm_ffn[l], mods, l, rows)
            g = ffn_up(h2, w1_dense[l // 2].astype(BF16), w3_dense[l // 2].astype(BF16))
            xr = ffn_down_residual(g, w2_dense[l // 2].astype(BF16), xr, mods, l)
        else:
            wr = jnp.concatenate([w_router[l // 2], jnp.zeros((D_MODEL, LANES - N_EXPERTS), F32)], axis=1)
            x_lat, h2, route = out_proj_residual(mixed, w_out[l].astype(BF16), xr, norm_ffn[l], mods, l, rows, wr)
            dest, src, block_expert, n_used = moe_dispatch(route)
            buf = gather_rows(h2, src)
            y = moe_experts(buf, block_expert, n_used, w1_moe[l // 2].astype(BF16),
                            w3_moe[l // 2].astype(BF16), w2_moe[l // 2].astype(BF16))
            out = moe_combine_final(y, dest, x_lat, route, mods, l, final_norm)
    return out.reshape(BATCH, SEQ, D_MODEL)
```

```python
import functools

import numpy as np
import jax
import jax.numpy as jnp
from jax import lax
from jax.experimental import pallas as pl
from jax.experimental.pallas import tpu as pltpu

D_MODEL = 2048
BATCH = 2
SEQ = 4096
DEPTH = 2
CTX_LEN = 256
GRID_W = 64
EPS = 1e-6

N_HEADS = D_MODEL // 128
Q_LORA = 512
KV_LORA = 512
NOPE_DIM = 128
ROPE_DIM = 64
V_DIM = 128
ROPE_BASE = 10000.0
SM_SCALE = (NOPE_DIM + ROPE_DIM) ** -0.5

D_INNER = 2 * D_MODEL
SSM_HEAD_DIM = 64
SSM_HEADS = D_INNER // SSM_HEAD_DIM
SSM_GROUPS = 8
D_STATE = 128
CONV_K = 5
CONV_DIM = D_INNER + 2 * SSM_GROUPS * D_STATE
CHUNK = 128

PROJ_SIZES = (Q_LORA, KV_LORA, ROPE_DIM, D_INNER, CONV_DIM, 2 * SSM_HEADS, D_MODEL, D_MODEL)

D_FF = 256 * ((8 * D_MODEL // 3 + 255) // 256)
N_EXPERTS = 8
TOP_K = 2
D_FF_EXPERT = 7 * D_MODEL // 2

F32 = jnp.float32
BF16 = jnp.bfloat16
HIGHEST = lax.Precision.HIGHEST

VMEM_LIMIT_BYTES = 56 * 1024 * 1024
LANES = 128
HALO_ROWS = 16
OFF_CENTRE_TAPS = (0, 1, 3, 4)
CONV_COL_TILE = 2048
LOG2E = 1.4426950408889634

ROWS_LAT = BATCH * SEQ
ROWS_CTX = BATCH * CTX_LEN
ROWS = ROWS_LAT + ROWS_CTX
TM = 512
TILES_PER_BATCH = SEQ // TM
N_LAT_TILES = ROWS_LAT // TM
CHUNKS_LAT = SEQ // CHUNK
CHUNKS_CTX = CTX_LEN // CHUNK
CHUNKS_SEQ = CHUNKS_LAT + CHUNKS_CTX

COL_XBC = 0
COL_Z = COL_XBC + CONV_DIM
COL_GA = COL_Z + D_INNER
COL_GB = COL_GA + D_MODEL
COL_CQ = COL_GB + D_MODEL
COL_CKV = COL_CQ + Q_LORA
N_MAIN = COL_CKV + KV_LORA
HEAD_Q = 2 * LANES

MOE_ROW_TILE = 512
MOE_FF_TILE = 512
MOE_SLOTS = ROWS_LAT * TOP_K + N_EXPERTS * MOE_ROW_TILE
GATHER_ROWS = 256
ATTN_Q_TILE = 512
ATTN_KV_CHUNK = 512
ROPE_SWAP = np.concatenate([np.arange(16, 32), np.arange(0, 16), np.arange(48, 64), np.arange(32, 48)])


def _params(*sem):
    return pltpu.CompilerParams(dimension_semantics=sem, vmem_limit_bytes=VMEM_LIMIT_BYTES)


def _largest_tile(n, candidates):
    for c in candidates:
        if n % c == 0:
            return c
    raise ValueError(f"no tile for {n}")


def _mod_row(i):
    return jnp.minimum(i // TILES_PER_BATCH, BATCH)


def _rope_row(i):
    return jnp.where(i < N_LAT_TILES, i % TILES_PER_BATCH, TILES_PER_BATCH)


def _mod_spec(layer, k):
    return pl.BlockSpec((None, None, None, 1, D_MODEL), lambda i, *_: (layer, _mod_row(i), k, 0, 0))


def _rms(x):
    return x * lax.rsqrt(jnp.mean(x * x, axis=-1, keepdims=True) + EPS)


def _silu(x):
    return x * jax.nn.sigmoid(x)


def _rope(x, tab):
    y = x * tab
    y = y + pltpu.roll(y, ROPE_DIM, axis=1)
    lane = lax.broadcasted_iota(jnp.int32, y.shape, 1)
    return jnp.where(lane < ROPE_DIM, y, 0.0)


def _adaln_kernel(c_ref, w_ref, b_ref, o_ref):
    a = _silu(c_ref[...]).astype(BF16)
    o_ref[...] = jnp.dot(a, w_ref[...].astype(BF16), preferred_element_type=F32) + b_ref[...]


def adaln(cc, w_ada, b_ada):
    tn = 1024
    n = 6 * D_MODEL
    out = pl.pallas_call(
        _adaln_kernel,
        out_shape=jax.ShapeDtypeStruct((DEPTH, 8, n), F32),
        grid=(DEPTH, n // tn),
        in_specs=[pl.BlockSpec((8, D_MODEL), lambda l, j: (0, 0)),
                  pl.BlockSpec((None, D_MODEL, tn), lambda l, j: (l, 0, j)),
                  pl.BlockSpec((None, 1, tn), lambda l, j: (l, 0, j))],
        out_specs=pl.BlockSpec((None, 8, tn), lambda l, j: (l, 0, j)),
        compiler_params=_params("parallel", "parallel"),
        name="adaln",
    )(cc, w_ada, b_ada.reshape(DEPTH, 1, n))
    return out.reshape(DEPTH, 8, 6, 1, D_MODEL)


def _norm_mod_kernel(x_ref, g_ref, sh_ref, sc_ref, o_ref):
    y = _rms(x_ref[...]) * g_ref[...]
    o_ref[...] = (y * (1 + sc_ref[...]) + sh_ref[...]).astype(o_ref.dtype)


def norm_mod(x, g, mods, layer, k_shift, k_scale):
    r = x.shape[0]
    return pl.pallas_call(
        _norm_mod_kernel,
        out_shape=jax.ShapeDtypeStruct((r, D_MODEL), BF16),
        grid=(r // TM,),
        in_specs=[pl.BlockSpec((TM, D_MODEL), lambda i: (i, 0)),
                  pl.BlockSpec((1, D_MODEL), lambda i: (0, 0)),
                  _mod_spec(layer, k_shift), _mod_spec(layer, k_scale)],
        out_specs=pl.BlockSpec((TM, D_MODEL), lambda i: (i, 0)),
        compiler_params=_params("parallel"),
        name="norm_mod",
    )(x, g.reshape(1, D_MODEL), mods, mods)


def _mm_kernel(a_ref, b_ref, o_ref):
    o_ref[...] = jnp.dot(a_ref[...], b_ref[...], preferred_element_type=F32).astype(o_ref.dtype)


def matmul(a, b, out_dtype):
    m, k = a.shape
    n = b.shape[1]
    tn = _largest_tile(n, (1024, 512, 256, 128))
    return pl.pallas_call(
        _mm_kernel,
        out_shape=jax.ShapeDtypeStruct((m, n), out_dtype),
        grid=(m // TM, n // tn),
        in_specs=[pl.BlockSpec((TM, k), lambda i, j: (i, 0)),
                  pl.BlockSpec((k, tn), lambda i, j: (0, j))],
        out_specs=pl.BlockSpec((TM, tn), lambda i, j: (i, j)),
        compiler_params=_params("parallel", "parallel"),
        name="matmul",
    )(a, b)


def _proj_small_kernel(a_ref, b_ref, tab_ref, dt_ref, kr_ref):
    acc = jnp.dot(a_ref[...], b_ref[...], preferred_element_type=F32)
    dt_ref[...] = acc[:, :LANES]
    kr_ref[...] = _rope(acc[:, LANES:], tab_ref[...]).astype(kr_ref.dtype)


def proj_small(h, w_small, tab):
    return pl.pallas_call(
        _proj_small_kernel,
        out_shape=(jax.ShapeDtypeStruct((ROWS, LANES), F32), jax.ShapeDtypeStruct((ROWS, LANES), BF16)),
        grid=(ROWS // TM,),
        in_specs=[pl.BlockSpec((TM, D_MODEL), lambda i: (i, 0)),
                  pl.BlockSpec((D_MODEL, 2 * LANES), lambda i: (0, 0)),
                  pl.BlockSpec((TM, LANES), lambda i: (_rope_row(i), 0))],
        out_specs=(pl.BlockSpec((TM, LANES), lambda i: (i, 0)),
                   pl.BlockSpec((TM, LANES), lambda i: (i, 0))),
        compiler_params=_params("parallel"),
        name="proj_small",
    )(h, w_small, tab)


def _up_kernel(c_ref, g_ref, w_ref, *rest, rope, heads_per_tile):
    if rope:
        tab_ref, o_ref = rest
    else:
        (o_ref,) = rest
    c = _rms(c_ref[...].astype(F32)) * g_ref[...]
    acc = jnp.dot(c.astype(BF16), w_ref[...], preferred_element_type=F32)
    if not rope:
        o_ref[...] = acc.astype(o_ref.dtype)
        return
    tab = tab_ref[...]
    for hh in range(heads_per_tile):
        base = hh * HEAD_Q
        o_ref[:, base:base + LANES] = acc[:, base:base + LANES].astype(o_ref.dtype)
        o_ref[:, base + LANES:base + HEAD_Q] = _rope(acc[:, base + LANES:base + HEAD_Q], tab).astype(o_ref.dtype)


def up_proj(p_main, col, g, w, tab):
    lora, n = w.shape
    tn = 1024
    rope = tab is not None
    in_specs = [pl.BlockSpec((TM, lora), lambda i, j: (i, col // lora)),
                pl.BlockSpec((1, lora), lambda i, j: (0, 0)),
                pl.BlockSpec((lora, tn), lambda i, j: (0, j))]
    args = [p_main, g.reshape(1, lora), w]
    if rope:
        in_specs.append(pl.BlockSpec((TM, LANES), lambda i, j: (_rope_row(i), 0)))
        args.append(tab)
    return pl.pallas_call(
        functools.partial(_up_kernel, rope=rope, heads_per_tile=tn // HEAD_Q),
        out_shape=jax.ShapeDtypeStruct((ROWS, n), BF16),
        grid=(ROWS // TM, n // tn),
        in_specs=in_specs,
        out_specs=pl.BlockSpec((TM, tn), lambda i, j: (i, j)),
        compiler_params=_params("parallel", "parallel"),
        name="up_proj_rope" if rope else "up_proj",
    )(*args)


def _softmax_chunks(q, kcat_ref, v_chunks):
    c2 = SM_SCALE * LOG2E
    m = l = acc = None
    for start, size, v in v_chunks:
        k = kcat_ref[start:start + size, :]
        s = lax.dot_general(q, k, (((1,), (1,)), ((), ())), preferred_element_type=F32)
        m_cur = jnp.max(s, axis=-1, keepdims=True)
        if m is None:
            m_new = m_cur
            p = jnp.exp2((s - m_new) * c2)
            l = jnp.sum(p, axis=-1, keepdims=True)
            acc = jnp.dot(p.astype(BF16), v, preferred_element_type=F32)
        else:
            m_new = jnp.maximum(m, m_cur)
            alpha = jnp.exp2((m - m_new) * c2)
            p = jnp.exp2((s - m_new) * c2)
            l = alpha * l + jnp.sum(p, axis=-1, keepdims=True)
            acc = alpha * acc + jnp.dot(p.astype(BF16), v, preferred_element_type=F32)
        m = m_new
    return acc / l


def _lat_attn_kernel(q_ref, knl_ref, krl_ref, vl_ref, knc_ref, krc_ref, vc_ref, o_ref, kcat_ref):
    @pl.when(pl.program_id(2) == 0)
    def _():
        kcat_ref[0:CTX_LEN, 0:LANES] = knc_ref[...]
        kcat_ref[0:CTX_LEN, LANES:HEAD_Q] = krc_ref[...]
        kcat_ref[CTX_LEN:, 0:LANES] = knl_ref[...]
        kcat_ref[CTX_LEN:, LANES:HEAD_Q] = krl_ref[...]

    chunks = [(0, CTX_LEN, vc_ref[...])]
    for s in range(0, SEQ, ATTN_KV_CHUNK):
        chunks.append((CTX_LEN + s, ATTN_KV_CHUNK, vl_ref[s:s + ATTN_KV_CHUNK, :]))
    o_ref[...] = _softmax_chunks(q_ref[...], kcat_ref, chunks).astype(o_ref.dtype)


def latent_attention(q, kv, kr):
    tq = ATTN_Q_TILE
    qt = SEQ // tq
    ctx_blk = ROWS_LAT // CTX_LEN
    return pl.pallas_call(
        _lat_attn_kernel,
        out_shape=jax.ShapeDtypeStruct((ROWS_LAT, D_MODEL), BF16),
        grid=(BATCH, N_HEADS, qt),
        in_specs=[pl.BlockSpec((tq, HEAD_Q), lambda b, h, i: (b * qt + i, h)),
                  pl.BlockSpec((SEQ, LANES), lambda b, h, i: (b, h)),
                  pl.BlockSpec((SEQ, LANES), lambda b, h, i: (b, 0)),
                  pl.BlockSpec((SEQ, LANES), lambda b, h, i: (b, N_HEADS + h)),
                  pl.BlockSpec((CTX_LEN, LANES), lambda b, h, i: (ctx_blk + b, h)),
                  pl.BlockSpec((CTX_LEN, LANES), lambda b, h, i: (ctx_blk + b, 0)),
                  pl.BlockSpec((CTX_LEN, LANES), lambda b, h, i: (ctx_blk + b, N_HEADS + h))],
        out_specs=pl.BlockSpec((tq, LANES), lambda b, h, i: (b * qt + i, h)),
        scratch_shapes=[pltpu.VMEM((CTX_LEN + SEQ, HEAD_Q), BF16)],
        compiler_params=_params("parallel", "parallel", "arbitrary"),
        name="latent_attention",
    )(q, kv, kr, kv, kv, kr, kv)


def _ctx_attn_kernel(q_ref, kn_ref, kr_ref, v_ref, o_ref, kcat_ref):
    kcat_ref[:, 0:LANES] = kn_ref[...]
    kcat_ref[:, LANES:HEAD_Q] = kr_ref[...]
    o_ref[...] = _softmax_chunks(q_ref[...], kcat_ref, [(0, CTX_LEN, v_ref[...])]).astype(o_ref.dtype)


def context_attention(q, kv, kr):
    ctx_blk = ROWS_LAT // CTX_LEN
    return pl.pallas_call(
        _ctx_attn_kernel,
        out_shape=jax.ShapeDtypeStruct((ROWS_CTX, D_MODEL), BF16),
        grid=(BATCH, N_HEADS),
        in_specs=[pl.BlockSpec((CTX_LEN, HEAD_Q), lambda b, h: (ctx_blk + b, h)),
                  pl.BlockSpec((CTX_LEN, LANES), lambda b, h: (ctx_blk + b, h)),
                  pl.BlockSpec((CTX_LEN, LANES), lambda b, h: (ctx_blk + b, 0)),
                  pl.BlockSpec((CTX_LEN, LANES), lambda b, h: (ctx_blk + b, N_HEADS + h))],
        out_specs=pl.BlockSpec((CTX_LEN, LANES), lambda b, h: (b, h)),
        scratch_shapes=[pltpu.VMEM((CTX_LEN, HEAD_Q), BF16)],
        compiler_params=_params("parallel", "parallel"),
        name="context_attention",
    )(q, kv, kr, kv)


def _chunk_block(b, s, reverse):
    if reverse:
        return jnp.where(s < CHUNKS_CTX, ROWS_LAT // CHUNK + CHUNKS_CTX * b + (CHUNKS_CTX - 1 - s),
                         CHUNKS_LAT * b + (CHUNKS_SEQ - 1 - s))
    return jnp.where(s < CHUNKS_CTX, ROWS_LAT // CHUNK + CHUNKS_CTX * b + s,
                     CHUNKS_LAT * b + (s - CHUNKS_CTX))


def _expand_heads(v, e3_ref):
    hi = v.astype(BF16)
    r1 = v - hi.astype(F32)
    mid = r1.astype(BF16)
    lo = (r1 - mid.astype(F32)).astype(BF16)
    return jnp.dot(jnp.concatenate([hi, mid, lo], axis=1), e3_ref[...], preferred_element_type=F32)


def _ssd_chunk(xcol, dt_raw, dtb_ref, a_ref, e3_ref, state_ref, reverse):
    off = SSM_HEADS if reverse else 0
    heads = slice(off, off + SSM_HEADS)
    row = lax.broadcasted_iota(jnp.int32, (CHUNK, CHUNK), 0)
    colm = lax.broadcasted_iota(jnp.int32, (CHUNK, CHUNK), 1)
    keep = (colm >= row) if reverse else (colm <= row)
    tri = keep.astype(F32)
    dt = jax.nn.softplus(dt_raw + dtb_ref[...])
    a = dt * a_ref[...]
    cum = jnp.dot(tri, a, precision=HIGHEST, preferred_element_type=F32)
    cum2 = cum * LOG2E
    cum2_t = cum2.T
    last_row = 0 if reverse else CHUNK - 1
    dec_h = jnp.exp2(cum2)
    end_h = dt * jnp.exp2(cum2[last_row:last_row + 1, :] - cum2)
    dt_full = _expand_heads(dt[:, heads], e3_ref)
    end_full = _expand_heads(end_h[:, heads], e3_ref)
    dec_in = _expand_heads(dec_h[:, heads], e3_ref)
    chunk_dec = dec_in[last_row:last_row + 1, :]
    xs = xcol(0, D_INNER)
    xcb = (xs * dt_full).astype(BF16)
    xce = (xs * end_full).astype(BF16)
    gw = SSM_HEADS // SSM_GROUPS * SSM_HEAD_DIM
    lane = lax.broadcasted_iota(jnp.int32, (CHUNK, LANES), 1)
    ys = []
    for g in range(SSM_GROUPS):
        b32 = xcol(D_INNER + g * D_STATE, D_INNER + (g + 1) * D_STATE)
        bg = b32.astype(BF16)
        cg = xcol(D_INNER + (SSM_GROUPS + g) * D_STATE, D_INNER + (SSM_GROUPS + g + 1) * D_STATE).astype(BF16)
        cb = lax.dot_general(cg, bg, (((1,), (1,)), ((), ())), preferred_element_type=F32)
        st = state_ref[g]
        y_off = jnp.dot(cg, st.astype(BF16), preferred_element_type=F32) * dec_in[:, g * gw:(g + 1) * gw]
        upd = jnp.dot(b32.T.astype(BF16), xce[:, g * gw:(g + 1) * gw],
                      preferred_element_type=F32)
        state_ref[g] = st * chunk_dec[:, g * gw:(g + 1) * gw] + upd
        pairs = []
        for j in range(gw // LANES):
            h0 = off + g * (SSM_HEADS // SSM_GROUPS) + 2 * j
            ms = []
            for hh in (h0, h0 + 1):
                seg = jnp.where(keep, cum2[:, hh:hh + 1] - cum2_t[hh:hh + 1, :], -jnp.inf)
                ms.append((jnp.exp2(seg) * cb).astype(BF16))
            xp = xcb[:, g * gw + j * LANES:g * gw + (j + 1) * LANES]
            rhs = jnp.concatenate([jnp.where(lane < SSM_HEAD_DIM, xp, 0), jnp.where(lane >= SSM_HEAD_DIM, xp, 0)],
                                  axis=0)
            pairs.append(jnp.dot(jnp.concatenate(ms, axis=1), rhs, preferred_element_type=F32))
        ys.append(jnp.concatenate(pairs, axis=1) + y_off)
    return jnp.concatenate(ys, axis=1)


def _ssd_fwd_kernel(prev_ref, cur_ref, next_ref, dt_ref, shift_ref, cw_ref, cbias_ref, dtb_ref, a_ref, e3_ref,
                    y_ref, xbc_ref, state_ref, xs_ref):
    s = pl.program_id(1)

    @pl.when(s == 0)
    def _():
        state_ref[...] = jnp.zeros_like(state_ref)

    first = (s == 0) | (s == CHUNKS_CTX)
    final = (s == CHUNKS_CTX - 1) | (s == CHUNKS_SEQ - 1)
    halo_zeros = jnp.zeros(prev_ref.shape, prev_ref.dtype)
    prev = jnp.where(first, halo_zeros, prev_ref[...])
    nxt = jnp.where(final, halo_zeros, next_ref[...])
    ext = jnp.concatenate([prev, cur_ref[...], nxt], axis=0)
    for c0 in range(0, CONV_DIM, CONV_COL_TILE):
        cs = slice(c0, c0 + CONV_COL_TILE)
        shifted = jnp.dot(shift_ref[...], ext[:, cs], preferred_element_type=F32)
        acc = cbias_ref[:, cs] + cw_ref[CONV_K // 2:CONV_K // 2 + 1, cs] * cur_ref[:, cs].astype(F32)
        for j, k in enumerate(OFF_CENTRE_TAPS):
            acc = acc + cw_ref[k:k + 1, cs] * shifted[j * CHUNK:(j + 1) * CHUNK, :]
        xs = _silu(acc)
        xs_ref[:, cs] = xs
        xbc_ref[:, cs] = xs.astype(xbc_ref.dtype)
    y_ref[...] = _ssd_chunk(lambda a, b: xs_ref[:, a:b], dt_ref[...], dtb_ref, a_ref, e3_ref, state_ref,
                            reverse=False)


def _ssd_bwd_kernel(xbc_ref, dt_ref, yf_ref, z0_ref, z1_ref, dtb_ref, a_ref, e3_ref, dsum_ref, nw_ref,
                    u_ref, state_ref):
    @pl.when(pl.program_id(1) == 0)
    def _():
        state_ref[...] = jnp.zeros_like(state_ref)

    xcol = lambda a, b: xbc_ref[:, a:b].astype(F32)
    y = _ssd_chunk(xcol, dt_ref[...], dtb_ref, a_ref, e3_ref, state_ref, reverse=True)
    y = y + yf_ref[...] + dsum_ref[...] * xcol(0, D_INNER)
    z = jnp.concatenate([z0_ref[...], z1_ref[...]], axis=1).astype(F32)
    u = y * _silu(z)
    gw = D_INNER // SSM_GROUPS
    for g in range(SSM_GROUPS):
        ug = u[:, g * gw:(g + 1) * gw]
        u_ref[:, g * gw:(g + 1) * gw] = (_rms(ug) * nw_ref[:, g * gw:(g + 1) * gw]).astype(u_ref.dtype)


def ssd_mixer(p_main, dt_raw, conv_w, conv_b, dt_bias, a_log, d_skip, ssm_norm, e3):
    zb = COL_Z // (D_INNER // 2)
    dtb = dt_bias.reshape(1, 2 * SSM_HEADS).astype(F32)
    a_neg = (-jnp.exp(a_log.astype(F32))).reshape(1, 2 * SSM_HEADS)
    dsum = jnp.repeat(d_skip[0] + d_skip[1], SSM_HEAD_DIM).reshape(1, D_INNER)
    halo_per_chunk = CHUNK // HALO_ROWS
    n_halo = ROWS // HALO_ROWS
    shift_np = np.zeros((len(OFF_CENTRE_TAPS) * CHUNK, CHUNK + 2 * HALO_ROWS), np.float32)
    for j, k in enumerate(OFF_CENTRE_TAPS):
        shift_np[j * CHUNK + np.arange(CHUNK), HALO_ROWS + np.arange(CHUNK) + k - CONV_K // 2] = 1.0
    shift = jnp.asarray(shift_np, dtype=BF16)
    full = lambda shape: pl.BlockSpec(shape, lambda b, s: tuple(0 for _ in shape))

    def blk(reverse):
        return lambda b, s: (_chunk_block(b, s, reverse), 0)

    fwd_blk = blk(False)
    y_f, xbc = pl.pallas_call(
        _ssd_fwd_kernel,
        out_shape=(jax.ShapeDtypeStruct((ROWS, D_INNER), F32), jax.ShapeDtypeStruct((ROWS, CONV_DIM), BF16)),
        grid=(BATCH, CHUNKS_SEQ),
        in_specs=[
            pl.BlockSpec((HALO_ROWS, CONV_DIM),
                         lambda b, s: (jnp.maximum(_chunk_block(b, s, False) * halo_per_chunk - 1, 0), 0)),
            pl.BlockSpec((CHUNK, CONV_DIM), fwd_blk),
            pl.BlockSpec((HALO_ROWS, CONV_DIM),
                         lambda b, s: (jnp.minimum((_chunk_block(b, s, False) + 1) * halo_per_chunk, n_halo - 1), 0)),
            pl.BlockSpec((CHUNK, LANES), fwd_blk),
            full(shift.shape),
            full((CONV_K, CONV_DIM)), full((1, CONV_DIM)), full((1, LANES)), full((1, LANES)),
            full((3 * SSM_HEADS, D_INNER)),
        ],
        out_specs=(pl.BlockSpec((CHUNK, D_INNER), fwd_blk), pl.BlockSpec((CHUNK, CONV_DIM), fwd_blk)),
        scratch_shapes=[pltpu.VMEM((SSM_GROUPS, D_STATE, D_INNER // SSM_GROUPS), F32),
                        pltpu.VMEM((CHUNK, CONV_DIM), F32)],
        compiler_params=_params("parallel", "arbitrary"),
        name="ssd_forward",
    )(p_main, p_main, p_main, dt_raw, shift, conv_w, conv_b.reshape(1, CONV_DIM), dtb, a_neg, e3)
    bwd_blk = blk(True)
    return pl.pallas_call(
        _ssd_bwd_kernel,
        out_shape=jax.ShapeDtypeStruct((ROWS, D_INNER), BF16),
        grid=(BATCH, CHUNKS_SEQ),
        in_specs=[
            pl.BlockSpec((CHUNK, CONV_DIM), bwd_blk),
            pl.BlockSpec((CHUNK, LANES), bwd_blk),
            pl.BlockSpec((CHUNK, D_INNER), bwd_blk),
            pl.BlockSpec((CHUNK, D_INNER // 2), lambda b, s: (_chunk_block(b, s, True), zb)),
            pl.BlockSpec((CHUNK, D_INNER // 2), lambda b, s: (_chunk_block(b, s, True), zb + 1)),
            full((1, LANES)), full((1, LANES)), full((3 * SSM_HEADS, D_INNER)),
            full((1, D_INNER)), full((1, D_INNER)),
        ],
        out_specs=pl.BlockSpec((CHUNK, D_INNER), bwd_blk),
        scratch_shapes=[pltpu.VMEM((SSM_GROUPS, D_STATE, D_INNER // SSM_GROUPS), F32)],
        compiler_params=_params("parallel", "arbitrary"),
        name="ssd_backward",
    )(xbc, dt_raw, y_f, p_main, p_main, dtb, a_neg, e3, dsum, ssm_norm.reshape(1, D_INNER))


def _merge_kernel(attl_ref, attc_ref, u_ref, woa_ref, wob_ref, ga_ref, gb_ref, o_ref):
    att = jnp.where(pl.program_id(0) < N_LAT_TILES, attl_ref[...], attc_ref[...])
    o_a = jnp.dot(att, woa_ref[...], preferred_element_type=F32)
    o_b = jnp.dot(u_ref[...], wob_ref[...], preferred_element_type=F32)
    o_ref[...] = (jax.nn.sigmoid(ga_ref[...].astype(F32)) * o_a
                  + jax.nn.sigmoid(gb_ref[...].astype(F32)) * o_b).astype(o_ref.dtype)


def merge_branches(att_lat, att_ctx, u, p_main, w_oa, w_ob, rows):
    tn = 512
    return pl.pallas_call(
        _merge_kernel,
        out_shape=jax.ShapeDtypeStruct((rows, D_MODEL), BF16),
        grid=(rows // TM, D_MODEL // tn),
        in_specs=[pl.BlockSpec((TM, D_MODEL), lambda i, j: (jnp.minimum(i, N_LAT_TILES - 1), 0)),
                  pl.BlockSpec((TM, D_MODEL), lambda i, j: (0, 0)),
                  pl.BlockSpec((TM, D_INNER), lambda i, j: (i, 0)),
                  pl.BlockSpec((D_MODEL, tn), lambda i, j: (0, j)),
                  pl.BlockSpec((D_INNER, tn), lambda i, j: (0, j)),
                  pl.BlockSpec((TM, tn), lambda i, j: (i, COL_GA // tn + j)),
                  pl.BlockSpec((TM, tn), lambda i, j: (i, COL_GB // tn + j))],
        out_specs=pl.BlockSpec((TM, tn), lambda i, j: (i, j)),
        compiler_params=_params("parallel", "parallel"),
        name="merge_branches",
    )(att_lat, att_ctx, u, w_oa, w_ob, p_main, p_main)


def _route(h, wr_ref):
    logits = jnp.dot(h, wr_ref[...], precision=HIGHEST, preferred_element_type=F32)
    lane = lax.broadcasted_iota(jnp.int32, logits.shape, 1)
    lg = jnp.where(lane < N_EXPERTS, logits, -jnp.inf)
    v1 = jnp.max(lg, axis=-1, keepdims=True)
    i1 = jnp.min(jnp.where(lg == v1, lane, LANES), axis=-1, keepdims=True)
    lg2 = jnp.where(lane == i1, -jnp.inf, lg)
    v2 = jnp.max(lg2, axis=-1, keepdims=True)
    i2 = jnp.min(jnp.where(lg2 == v2, lane, LANES), axis=-1, keepdims=True)
    e = jnp.exp(v2 - v1)
    g1 = 1.0 / (1.0 + e)
    g2 = e / (1.0 + e)
    return jnp.where(lane == 0, i1.astype(F32),
                     jnp.where(lane == 1, i2.astype(F32),
                               jnp.where(lane == 2, g1, jnp.where(lane == 3, g2, 0.0))))


def _out_res_kernel(m_ref, w_ref, x_ref, gate_ref, g2_ref, sh_ref, sc_ref, *rest, route):
    if route:
        wr_ref, xo_ref, ho_ref, ro_ref = rest
    else:
        xo_ref, ho_ref = rest
    acc = jnp.dot(m_ref[...], w_ref[...], preferred_element_type=F32)
    xn = x_ref[...] + gate_ref[...] * acc
    xo_ref[...] = xn
    h = (_rms(xn) * g2_ref[...]) * (1 + sc_ref[...]) + sh_ref[...]
    ho_ref[...] = h.astype(ho_ref.dtype)
    if route:
        ro_ref[...] = _route(h, wr_ref)


def out_proj_residual(mixed, w_out, x, g2, mods, layer, rows, w_router=None):
    route = w_router is not None
    h_dtype = F32 if route else BF16
    in_specs = [pl.BlockSpec((TM, D_MODEL), lambda i: (i, 0)),
                pl.BlockSpec((D_MODEL, D_MODEL), lambda i: (0, 0)),
                pl.BlockSpec((TM, D_MODEL), lambda i: (i, 0)),
                _mod_spec(layer, 2), pl.BlockSpec((1, D_MODEL), lambda i: (0, 0)),
                _mod_spec(layer, 3), _mod_spec(layer, 4)]
    args = [mixed, w_out, x, mods, g2.reshape(1, D_MODEL), mods, mods]
    out_shape = [jax.ShapeDtypeStruct((rows, D_MODEL), F32), jax.ShapeDtypeStruct((rows, D_MODEL), h_dtype)]
    out_specs = [pl.BlockSpec((TM, D_MODEL), lambda i: (i, 0)), pl.BlockSpec((TM, D_MODEL), lambda i: (i, 0))]
    if route:
        in_specs.append(pl.BlockSpec((D_MODEL, LANES), lambda i: (0, 0)))
        args.append(w_router)
        out_shape.append(jax.ShapeDtypeStruct((rows, LANES), F32))
        out_specs.append(pl.BlockSpec((TM, LANES), lambda i: (i, 0)))
    return pl.pallas_call(
        functools.partial(_out_res_kernel, route=route),
        out_shape=tuple(out_shape),
        grid=(rows // TM,),
        in_specs=in_specs,
        out_specs=tuple(out_specs),
        compiler_params=_params("parallel"),
        name="out_proj_residual",
    )(*args)


def _ffn_up_kernel(h_ref, w1_ref, w3_ref, o_ref):
    h = h_ref[...]
    a = jnp.dot(h, w1_ref[...], preferred_element_type=F32)
    b = jnp.dot(h, w3_ref[...], preferred_element_type=F32)
    o_ref[...] = (_silu(a) * b).astype(o_ref.dtype)


def ffn_up(h, w1, w3):
    tn = 512
    return pl.pallas_call(
        _ffn_up_kernel,
        out_shape=jax.ShapeDtypeStruct((ROWS, D_FF), BF16),
        grid=(ROWS // TM, D_FF // tn),
        in_specs=[pl.BlockSpec((TM, D_MODEL), lambda i, j: (i, 0)),
                  pl.BlockSpec((D_MODEL, tn), lambda i, j: (0, j)),
                  pl.BlockSpec((D_MODEL, tn), lambda i, j: (0, j))],
        out_specs=pl.BlockSpec((TM, tn), lambda i, j: (i, j)),
        compiler_params=_params("parallel", "parallel"),
        name="ffn_up",
    )(h, w1, w3)


def _ffn_down_kernel(g_ref, w_ref, x_ref, gate_ref, o_ref):
    acc = jnp.dot(g_ref[...], w_ref[...], preferred_element_type=F32)
    o_ref[...] = x_ref[...] + gate_ref[...] * acc


def ffn_down_residual(g, w2, x, mods, layer):
    tn = 512
    gate_spec = pl.BlockSpec((None, None, None, 1, tn), lambda i, j: (layer, _mod_row(i), 5, 0, j))
    return pl.pallas_call(
        _ffn_down_kernel,
        out_shape=jax.ShapeDtypeStruct((ROWS, D_MODEL), F32),
        grid=(ROWS // TM, D_MODEL // tn),
        in_specs=[pl.BlockSpec((TM, D_FF), lambda i, j: (i, 0)),
                  pl.BlockSpec((D_FF, tn), lambda i, j: (0, j)),
                  pl.BlockSpec((TM, tn), lambda i, j: (i, j)),
                  gate_spec],
        out_specs=pl.BlockSpec((TM, tn), lambda i, j: (i, j)),
        compiler_params=_params("parallel", "parallel"),
        name="ffn_down_residual",
    )(g, w2, x, mods)


def _gather_rows_kernel(src_ref, h_hbm, o_ref, rows_ref, sem):
    base = pl.program_id(0) * GATHER_ROWS

    def row_copy(j):
        return pltpu.make_async_copy(h_hbm.at[pl.ds(src_ref[base + j], 1)], rows_ref.at[pl.ds(j, 1)], sem)

    def start(j, c):
        row_copy(j).start()
        return c

    def wait(j, c):
        row_copy(j).wait()
        return c

    lax.fori_loop(0, GATHER_ROWS, start, 0)
    lax.fori_loop(0, GATHER_ROWS, wait, 0)
    o_ref[...] = rows_ref[...].astype(o_ref.dtype)


def gather_rows(h, src):
    n = src.shape[0]
    d = h.shape[1]
    grid_spec = pltpu.PrefetchScalarGridSpec(
        num_scalar_prefetch=1, grid=(n // GATHER_ROWS,),
        in_specs=[pl.BlockSpec(memory_space=pl.ANY)],
        out_specs=pl.BlockSpec((GATHER_ROWS, d), lambda i, s: (i, 0)),
        scratch_shapes=[pltpu.VMEM((GATHER_ROWS, d), h.dtype), pltpu.SemaphoreType.DMA(())])
    return pl.pallas_call(
        _gather_rows_kernel,
        out_shape=jax.ShapeDtypeStruct((n, d), BF16),
        grid_spec=grid_spec,
        compiler_params=_params("arbitrary"),
        name="moe_gather",
    )(src, h)


def _moe_kernel(be_ref, nused_ref, x_ref, w1_ref, w3_ref, w2_ref, o_ref):
    i = pl.program_id(0)
    f = pl.program_id(1)

    @pl.when(f == 0)
    def _():
        o_ref[...] = jnp.zeros_like(o_ref)

    @pl.when(i < nused_ref[0])
    def _():
        x = x_ref[...]
        h1 = jnp.dot(x, w1_ref[...], preferred_element_type=F32)
        h3 = jnp.dot(x, w3_ref[...], preferred_element_type=F32)
        g = _silu(h1) * h3
        o_ref[...] += jnp.dot(g.astype(BF16), w2_ref[...], preferred_element_type=F32)


def moe_experts(buf, block_expert, n_used, w1, w3, w2):
    r, d = buf.shape
    tm, tf = MOE_ROW_TILE, MOE_FF_TILE
    nf = D_FF_EXPERT // tf

    def f_idx(i, f, nu):
        return jnp.where(i < nu[0], f, nf - 1)

    grid_spec = pltpu.PrefetchScalarGridSpec(
        num_scalar_prefetch=2,
        grid=(r // tm, nf),
        in_specs=[
            pl.BlockSpec((tm, d), lambda i, f, be, nu: (jnp.minimum(i, nu[0] - 1), 0)),
            pl.BlockSpec((None, d, tf), lambda i, f, be, nu: (be[i], 0, f_idx(i, f, nu))),
            pl.BlockSpec((None, d, tf), lambda i, f, be, nu: (be[i], 0, f_idx(i, f, nu))),
            pl.BlockSpec((None, tf, d), lambda i, f, be, nu: (be[i], f_idx(i, f, nu), 0)),
        ],
        out_specs=pl.BlockSpec((tm, d), lambda i, f, be, nu: (i, 0)),
    )
    return pl.pallas_call(
        _moe_kernel,
        out_shape=jax.ShapeDtypeStruct((r, d), F32),
        grid_spec=grid_spec,
        compiler_params=_params("arbitrary", "arbitrary"),
        name="moe_experts",
    )(block_expert, n_used, buf, w1, w3, w2)


def _combine_kernel(dest_ref, y_hbm, x_ref, route_ref, gate_ref, gfin_ref, o_ref, ybuf_ref, sem):
    base = pl.program_id(0) * GATHER_ROWS

    def row_copy(t, k):
        return pltpu.make_async_copy(y_hbm.at[pl.ds(dest_ref[TOP_K * (base + t) + k], 1)],
                                     ybuf_ref.at[k, pl.ds(t, 1)], sem)

    def start(t, c):
        for k in range(TOP_K):
            row_copy(t, k).start()
        return c

    def wait(t, c):
        for k in range(TOP_K):
            row_copy(t, k).wait()
        return c

    lax.fori_loop(0, GATHER_ROWS, start, 0)
    lax.fori_loop(0, GATHER_ROWS, wait, 0)
    route = route_ref[...]
    f = ybuf_ref[0] * route[:, 2:3] + ybuf_ref[1] * route[:, 3:4]
    xn = x_ref[...] + gate_ref[...] * f
    o_ref[...] = _rms(xn) * gfin_ref[...]


def moe_combine_final(y, dest, x, route, mods, layer, final_norm):
    tiles_per_batch = SEQ // GATHER_ROWS
    grid_spec = pltpu.PrefetchScalarGridSpec(
        num_scalar_prefetch=1, grid=(ROWS_LAT // GATHER_ROWS,),
        in_specs=[pl.BlockSpec(memory_space=pl.ANY),
                  pl.BlockSpec((GATHER_ROWS, D_MODEL), lambda i, d: (i, 0)),
                  pl.BlockSpec((GATHER_ROWS, LANES), lambda i, d: (i, 0)),
                  pl.BlockSpec((None, None, None, 1, D_MODEL), lambda i, d: (layer, i // tiles_per_batch, 5, 0, 0)),
                  pl.BlockSpec((1, D_MODEL), lambda i, d: (0, 0))],
        out_specs=pl.BlockSpec((GATHER_ROWS, D_MODEL), lambda i, d: (i, 0)),
        scratch_shapes=[pltpu.VMEM((TOP_K, GATHER_ROWS, D_MODEL), F32), pltpu.SemaphoreType.DMA(())])
    return pl.pallas_call(
        _combine_kernel,
        out_shape=jax.ShapeDtypeStruct((ROWS_LAT, D_MODEL), F32),
        grid_spec=grid_spec,
        compiler_params=_params("arbitrary"),
        name="moe_combine_final",
    )(dest, y, x, route, mods, final_norm.reshape(1, D_MODEL))


def moe_dispatch(route):
    expert = route[:, :TOP_K].astype(jnp.int32).reshape(-1)
    onehot = (expert[:, None] == jnp.arange(N_EXPERTS)[None, :]).astype(jnp.int32)
    incl = jnp.cumsum(onehot, axis=0)
    rank = jnp.sum((incl - onehot) * onehot, axis=1)
    counts = incl[-1]
    tm = MOE_ROW_TILE
    padded = (counts + tm - 1) // tm * tm
    pad_end = jnp.cumsum(padded)
    dest = ((pad_end - padded)[expert] + rank).astype(jnp.int32)
    n_blocks = MOE_SLOTS // tm
    n_used = (pad_end[-1] // tm).astype(jnp.int32)
    blocks = jnp.arange(n_blocks)
    block_expert = jnp.minimum(jnp.searchsorted(pad_end, blocks * tm, side='right'), N_EXPERTS - 1).astype(jnp.int32)
    block_expert = jnp.where(blocks < n_used, block_expert, block_expert[jnp.maximum(n_used - 1, 0)])
    token = jnp.repeat(jnp.arange(ROWS_LAT, dtype=jnp.int32), TOP_K)
    src = jnp.zeros((MOE_SLOTS,), jnp.int32).at[dest].set(token)
    return dest, src, block_expert, n_used.reshape(1)


def _rope_table(n):
    rows = n // GRID_W
    row = jnp.repeat(jnp.arange(rows, dtype=F32), GRID_W)
    col = jnp.tile(jnp.arange(GRID_W, dtype=F32), rows)
    axis_dim = ROPE_DIM // 2
    inv = ROPE_BASE ** (-jnp.arange(0, axis_dim, 2, dtype=F32) / axis_dim)
    ang_r, ang_c = row[:, None] * inv, col[:, None] * inv
    cr, sr, cc, sc = jnp.cos(ang_r), jnp.sin(ang_r), jnp.cos(ang_c), jnp.sin(ang_c)
    lat = jnp.concatenate([cr, cr, cc, cc, -sr, sr, -sc, sc], axis=1)
    ident = jnp.concatenate([jnp.ones((TM, ROPE_DIM), F32), jnp.zeros((TM, ROPE_DIM), F32)], axis=1)
    return jnp.concatenate([lat, ident], axis=0)


def _layer_weights(w_in, w_uq, w_ukv):
    o = np.cumsum((0,) + PROJ_SIZES)
    seg = lambda i: w_in[:, int(o[i]):int(o[i + 1])]
    w_main = jnp.concatenate([seg(4), seg(3), seg(6), seg(7), seg(0), seg(1)], axis=1).astype(BF16)
    kr_w = seg(2)
    w_small = jnp.concatenate([seg(5), kr_w, kr_w[:, ROPE_SWAP]], axis=1).astype(BF16)
    wq = w_uq.reshape(Q_LORA, N_HEADS, NOPE_DIM + ROPE_DIM)
    rope_w = wq[:, :, NOPE_DIM:]
    w_q = jnp.concatenate([wq[:, :, :NOPE_DIM], rope_w, rope_w[:, :, ROPE_SWAP]], axis=2)
    w_q = w_q.reshape(Q_LORA, N_HEADS * HEAD_Q).astype(BF16)
    wkv = w_ukv.reshape(KV_LORA, N_HEADS, NOPE_DIM + V_DIM)
    w_kv = jnp.concatenate([wkv[:, :, :NOPE_DIM].reshape(KV_LORA, -1), wkv[:, :, NOPE_DIM:].reshape(KV_LORA, -1)],
                           axis=1).astype(BF16)
    return w_main, w_small, w_q, w_kv


def kernel(x, c, ctx, c_ctx, norm_mix, norm_ffn, w_ada, b_ada, w_in, q_norm, w_uq, kv_norm, w_ukv,
           conv_w, conv_b, a_log, dt_bias, d_skip, ssm_norm, w_oa, w_ob, w_out, w1_dense, w3_dense,
           w2_dense, w_router, w1_moe, w3_moe, w2_moe, final_norm):
    tab = _rope_table(SEQ)
    cc = jnp.concatenate([c, c_ctx[None], jnp.zeros((8 - BATCH - 1, D_MODEL), F32)], axis=0)
    mods = adaln(cc, w_ada, b_ada)
    head_of = np.arange(D_INNER) // SSM_HEAD_DIM
    e1 = (np.arange(SSM_HEADS)[:, None] == head_of[None, :]).astype(np.float32)
    e3 = jnp.asarray(np.concatenate([e1, e1, e1], axis=0), dtype=BF16)
    xr = jnp.concatenate([x.reshape(ROWS_LAT, D_MODEL), ctx.reshape(ROWS_CTX, D_MODEL)], axis=0)
    out = None
    for l in range(DEPTH):
        last = l == DEPTH - 1
        rows = ROWS_LAT if last else ROWS
        w_main, w_small, w_q, w_kv = _layer_weights(w_in[l], w_uq[l], w_ukv[l])
        h = norm_mod(xr, norm_mix[l], mods, l, 0, 1)
        p_main = matmul(h, w_main, BF16)
        dt_raw, kr = proj_small(h, w_small, tab)
        q = up_proj(p_main, COL_CQ, q_norm[l], w_q, tab)
        kv = up_proj(p_main, COL_CKV, kv_norm[l], w_kv, None)
        att = latent_attention(q, kv, kr)
        att_ctx = att if last else context_attention(q, kv, kr)
        u = ssd_mixer(p_main, dt_raw, conv_w[l], conv_b[l], dt_bias[l], a_log[l], d_skip[l], ssm_norm[l], e3)
        mixed = merge_branches(att, att_ctx, u, p_main, w_oa[l].astype(BF16), w_ob[l].astype(BF16), rows)
        if l % 2 == 0:
            xr, h2 = out_proj_residual(mixed, w_out[l].astype(BF16), xr, norm_ffn[l], mods, l, rows)
            g = ffn_up(h2, w1_dense[l // 2].astype(BF16), w3_dense[l // 2].astype(BF16))
            xr = ffn_down_residual(g, w2_dense[l // 2].astype(BF16), xr, mods, l)
        else:
            wr = jnp.concatenate([w_router[l // 2], jnp.zeros((D_MODEL, LANES - N_EXPERTS), F32)], axis=1)
            x_lat, h2, route = out_proj_residual(mixed, w_out[l].astype(BF16), xr, norm_ffn[l], mods, l, rows, wr)
            dest, src, block_expert, n_used = moe_dispatch(route)
            buf = gather_rows(h2, src)
            y = moe_experts(buf, block_expert, n_used, w1_moe[l // 2].astype(BF16),
                            w3_moe[l // 2].astype(BF16), w2_moe[l // 2].astype(BF16))
            out = moe_combine_final(y, dest, x_lat, route, mods, l, final_norm)
    return out.reshape(BATCH, SEQ, D_MODEL)
```

```python
import functools

import numpy as np
import jax
import jax.numpy as jnp
from jax import lax
from jax.experimental import pallas as pl
from jax.experimental.pallas import tpu as pltpu

D_MODEL = 2048
BATCH = 2
SEQ = 4096
DEPTH = 2
CTX_LEN = 256
GRID_W = 64
EPS = 1e-6

N_HEADS = D_MODEL // 128
Q_LORA = 512
KV_LORA = 512
NOPE_DIM = 128
ROPE_DIM = 64
V_DIM = 128
ROPE_BASE = 10000.0
SM_SCALE = (NOPE_DIM + ROPE_DIM) ** -0.5

D_INNER = 2 * D_MODEL
SSM_HEAD_DIM = 64
SSM_HEADS = D_INNER // SSM_HEAD_DIM
SSM_GROUPS = 8
D_STATE = 128
CONV_K = 5
CONV_DIM = D_INNER + 2 * SSM_GROUPS * D_STATE
CHUNK = 128

PROJ_SIZES = (Q_LORA, KV_LORA, ROPE_DIM, D_INNER, CONV_DIM, 2 * SSM_HEADS, D_MODEL, D_MODEL)

D_FF = 256 * ((8 * D_MODEL // 3 + 255) // 256)
N_EXPERTS = 8
TOP_K = 2
D_FF_EXPERT = 7 * D_MODEL // 2

F32 = jnp.float32
BF16 = jnp.bfloat16
HIGHEST = lax.Precision.HIGHEST

VMEM_LIMIT_BYTES = 56 * 1024 * 1024
LANES = 128
HALO_ROWS = 16
OFF_CENTRE_TAPS = (0, 1, 3, 4)
CONV_COL_TILE = 2048
LOG2E = 1.4426950408889634
QK_SCALE = SM_SCALE * LOG2E

ROWS_LAT = BATCH * SEQ
ROWS_CTX = BATCH * CTX_LEN
ROWS = ROWS_LAT + ROWS_CTX
TM = 512
TILES_PER_BATCH = SEQ // TM
N_LAT_TILES = ROWS_LAT // TM
CHUNKS_LAT = SEQ // CHUNK
CHUNKS_CTX = CTX_LEN // CHUNK
CHUNKS_SEQ = CHUNKS_LAT + CHUNKS_CTX

COL_XBC = 0
COL_Z = COL_XBC + CONV_DIM
COL_GA = COL_Z + D_INNER
COL_GB = COL_GA + D_MODEL
COL_CQ = COL_GB + D_MODEL
COL_CKV = COL_CQ + Q_LORA
N_MAIN = COL_CKV + KV_LORA
HEAD_Q = 2 * LANES

MOE_ROW_TILE = 1024
MOE_FF_TILE = 256
MOE_SLOTS = ROWS_LAT * TOP_K + N_EXPERTS * MOE_ROW_TILE
GATHER_ROWS = 256
DMA_UNROLL = 8
ATTN_Q_TILE = 2048
ATTN_KV_CHUNK = 256
ROPE_SWAP = np.concatenate([np.arange(16, 32), np.arange(0, 16), np.arange(48, 64), np.arange(32, 48)])


def _params(*sem):
    return pltpu.CompilerParams(dimension_semantics=sem, vmem_limit_bytes=VMEM_LIMIT_BYTES)


def _largest_tile(n, candidates):
    for c in candidates:
        if n % c == 0:
            return c
    raise ValueError(f"no tile for {n}")


def _mod_row(i):
    return jnp.minimum(i // TILES_PER_BATCH, BATCH)


def _rope_row(i):
    return jnp.where(i < N_LAT_TILES, i % TILES_PER_BATCH, TILES_PER_BATCH)


def _mod_spec(layer, k):
    return pl.BlockSpec((None, None, None, 1, D_MODEL), lambda i, *_: (layer, _mod_row(i), k, 0, 0))


def _rms(x):
    return x * lax.rsqrt(jnp.mean(x * x, axis=-1, keepdims=True) + EPS)


def _silu(x):
    return x * jax.nn.sigmoid(x)


def _rope(x, tab):
    y = x * tab
    y = y + pltpu.roll(y, ROPE_DIM, axis=1)
    lane = lax.broadcasted_iota(jnp.int32, y.shape, 1)
    return jnp.where(lane < ROPE_DIM, y, 0.0)


def _adaln_kernel(c_ref, w_ref, b_ref, o_ref):
    a = _silu(c_ref[...]).astype(BF16)
    o_ref[...] = jnp.dot(a, w_ref[...].astype(BF16), preferred_element_type=F32) + b_ref[...]


def adaln(cc, w_ada, b_ada):
    tn = 2048
    n = 6 * D_MODEL
    out = pl.pallas_call(
        _adaln_kernel,
        out_shape=jax.ShapeDtypeStruct((DEPTH, 8, n), F32),
        grid=(DEPTH, n // tn),
        in_specs=[pl.BlockSpec((8, D_MODEL), lambda l, j: (0, 0)),
                  pl.BlockSpec((None, D_MODEL, tn), lambda l, j: (l, 0, j)),
                  pl.BlockSpec((None, 1, tn), lambda l, j: (l, 0, j))],
        out_specs=pl.BlockSpec((None, 8, tn), lambda l, j: (l, 0, j)),
        compiler_params=_params("parallel", "parallel"),
        name="adaln",
    )(cc, w_ada, b_ada.reshape(DEPTH, 1, n))
    return out.reshape(DEPTH, 8, 6, 1, D_MODEL)


def _norm_mod_kernel(x_ref, g_ref, sh_ref, sc_ref, o_ref):
    y = _rms(x_ref[...]) * g_ref[...]
    o_ref[...] = (y * (1 + sc_ref[...]) + sh_ref[...]).astype(o_ref.dtype)


def norm_mod(x, g, mods, layer, k_shift, k_scale):
    r = x.shape[0]
    return pl.pallas_call(
        _norm_mod_kernel,
        out_shape=jax.ShapeDtypeStruct((r, D_MODEL), BF16),
        grid=(r // TM,),
        in_specs=[pl.BlockSpec((TM, D_MODEL), lambda i: (i, 0)),
                  pl.BlockSpec((1, D_MODEL), lambda i: (0, 0)),
                  _mod_spec(layer, k_shift), _mod_spec(layer, k_scale)],
        out_specs=pl.BlockSpec((TM, D_MODEL), lambda i: (i, 0)),
        compiler_params=_params("parallel"),
        name="norm_mod",
    )(x, g.reshape(1, D_MODEL), mods, mods)


def _mm_kernel(a_ref, b_ref, o_ref):
    o_ref[...] = jnp.dot(a_ref[...], b_ref[...], preferred_element_type=F32).astype(o_ref.dtype)


def matmul(a, b, out_dtype):
    m, k = a.shape
    n = b.shape[1]
    tn = _largest_tile(n, (1024, 512, 256, 128))
    return pl.pallas_call(
        _mm_kernel,
        out_shape=jax.ShapeDtypeStruct((m, n), out_dtype),
        grid=(m // TM, n // tn),
        in_specs=[pl.BlockSpec((TM, k), lambda i, j: (i, 0)),
                  pl.BlockSpec((k, tn), lambda i, j: (0, j))],
        out_specs=pl.BlockSpec((TM, tn), lambda i, j: (i, j)),
        compiler_params=_params("parallel", "parallel"),
        name="matmul",
    )(a, b)


def _proj_small_kernel(a_ref, b_ref, tab_ref, dt_ref, kr_ref):
    acc = jnp.dot(a_ref[...], b_ref[...], preferred_element_type=F32)
    dt_ref[...] = acc[:, :LANES]
    kr_ref[...] = _rope(acc[:, LANES:], tab_ref[...]).astype(kr_ref.dtype)


def proj_small(h, w_small, tab):
    return pl.pallas_call(
        _proj_small_kernel,
        out_shape=(jax.ShapeDtypeStruct((ROWS, LANES), F32), jax.ShapeDtypeStruct((ROWS, LANES), BF16)),
        grid=(ROWS // TM,),
        in_specs=[pl.BlockSpec((TM, D_MODEL), lambda i: (i, 0)),
                  pl.BlockSpec((D_MODEL, 2 * LANES), lambda i: (0, 0)),
                  pl.BlockSpec((TM, LANES), lambda i: (_rope_row(i), 0))],
        out_specs=(pl.BlockSpec((TM, LANES), lambda i: (i, 0)),
                   pl.BlockSpec((TM, LANES), lambda i: (i, 0))),
        compiler_params=_params("parallel"),
        name="proj_small",
    )(h, w_small, tab)


def _up_kernel(c_ref, g_ref, w_ref, *rest, rope, heads_per_tile):
    if rope:
        tab_ref, o_ref = rest
    else:
        (o_ref,) = rest
    c = _rms(c_ref[...].astype(F32)) * g_ref[...]
    acc = jnp.dot(c.astype(BF16), w_ref[...], preferred_element_type=F32)
    if not rope:
        o_ref[...] = acc.astype(o_ref.dtype)
        return
    tab = tab_ref[...]
    acc = acc * QK_SCALE
    for hh in range(heads_per_tile):
        base = hh * HEAD_Q
        o_ref[:, base:base + LANES] = acc[:, base:base + LANES].astype(o_ref.dtype)
        o_ref[:, base + LANES:base + HEAD_Q] = _rope(acc[:, base + LANES:base + HEAD_Q], tab).astype(o_ref.dtype)


def up_proj(p_main, col, g, w, tab):
    lora, n = w.shape
    tn = n
    rope = tab is not None
    in_specs = [pl.BlockSpec((TM, lora), lambda i, j: (i, col // lora)),
                pl.BlockSpec((1, lora), lambda i, j: (0, 0)),
                pl.BlockSpec((lora, tn), lambda i, j: (0, j))]
    args = [p_main, g.reshape(1, lora), w]
    if rope:
        in_specs.append(pl.BlockSpec((TM, LANES), lambda i, j: (_rope_row(i), 0)))
        args.append(tab)
    return pl.pallas_call(
        functools.partial(_up_kernel, rope=rope, heads_per_tile=tn // HEAD_Q),
        out_shape=jax.ShapeDtypeStruct((ROWS, n), BF16),
        grid=(ROWS // TM, n // tn),
        in_specs=in_specs,
        out_specs=pl.BlockSpec((TM, tn), lambda i, j: (i, j)),
        compiler_params=_params("parallel", "parallel"),
        name="up_proj_rope" if rope else "up_proj",
    )(*args)


def _softmax_chunks(q, kcat_ref, v_chunks):
    m = l = acc = None
    for start, size, v in v_chunks:
        k = kcat_ref[start:start + size, :]
        s = lax.dot_general(q, k, (((1,), (1,)), ((), ())), preferred_element_type=F32)
        m_cur = jnp.max(s, axis=-1, keepdims=True)
        if m is None:
            m_new = m_cur
            p = jnp.exp2(s - m_new)
            l = jnp.sum(p, axis=-1, keepdims=True)
            acc = jnp.dot(p.astype(BF16), v, preferred_element_type=F32)
        else:
            m_new = jnp.maximum(m, m_cur)
            alpha = jnp.exp2(m - m_new)
            p = jnp.exp2(s - m_new)
            l = alpha * l + jnp.sum(p, axis=-1, keepdims=True)
            acc = alpha * acc + jnp.dot(p.astype(BF16), v, preferred_element_type=F32)
        m = m_new
    return acc / l


def _lat_attn_kernel(q_ref, knl_ref, krl_ref, vl_ref, knc_ref, krc_ref, vc_ref, o_ref, kcat_ref):
    @pl.when(pl.program_id(2) == 0)
    def _():
        kcat_ref[0:CTX_LEN, 0:LANES] = knc_ref[...]
        kcat_ref[0:CTX_LEN, LANES:HEAD_Q] = krc_ref[...]
        kcat_ref[CTX_LEN:, 0:LANES] = knl_ref[...]
        kcat_ref[CTX_LEN:, LANES:HEAD_Q] = krl_ref[...]

    chunks = [(0, CTX_LEN, vc_ref[...])]
    for s in range(0, SEQ, ATTN_KV_CHUNK):
        chunks.append((CTX_LEN + s, ATTN_KV_CHUNK, vl_ref[s:s + ATTN_KV_CHUNK, :]))
    o_ref[...] = _softmax_chunks(q_ref[...], kcat_ref, chunks).astype(o_ref.dtype)


def latent_attention(q, kv, kr):
    tq = ATTN_Q_TILE
    qt = SEQ // tq
    ctx_blk = ROWS_LAT // CTX_LEN
    return pl.pallas_call(
        _lat_attn_kernel,
        out_shape=jax.ShapeDtypeStruct((ROWS_LAT, D_MODEL), BF16),
        grid=(BATCH, N_HEADS, qt),
        in_specs=[pl.BlockSpec((tq, HEAD_Q), lambda b, h, i: (b * qt + i, h)),
                  pl.BlockSpec((SEQ, LANES), lambda b, h, i: (b, h)),
                  pl.BlockSpec((SEQ, LANES), lambda b, h, i: (b, 0)),
                  pl.BlockSpec((SEQ, LANES), lambda b, h, i: (b, N_HEADS + h)),
                  pl.BlockSpec((CTX_LEN, LANES), lambda b, h, i: (ctx_blk + b, h)),
                  pl.BlockSpec((CTX_LEN, LANES), lambda b, h, i: (ctx_blk + b, 0)),
                  pl.BlockSpec((CTX_LEN, LANES), lambda b, h, i: (ctx_blk + b, N_HEADS + h))],
        out_specs=pl.BlockSpec((tq, LANES), lambda b, h, i: (b * qt + i, h)),
        scratch_shapes=[pltpu.VMEM((CTX_LEN + SEQ, HEAD_Q), BF16)],
        compiler_params=_params("parallel", "parallel", "arbitrary"),
        name="latent_attention",
    )(q, kv, kr, kv, kv, kr, kv)


def _ctx_attn_kernel(q_ref, kn_ref, kr_ref, v_ref, o_ref, kcat_ref):
    kcat_ref[:, 0:LANES] = kn_ref[...]
    kcat_ref[:, LANES:HEAD_Q] = kr_ref[...]
    o_ref[...] = _softmax_chunks(q_ref[...], kcat_ref, [(0, CTX_LEN, v_ref[...])]).astype(o_ref.dtype)


def context_attention(q, kv, kr):
    ctx_blk = ROWS_LAT // CTX_LEN
    return pl.pallas_call(
        _ctx_attn_kernel,
        out_shape=jax.ShapeDtypeStruct((ROWS_CTX, D_MODEL), BF16),
        grid=(BATCH, N_HEADS),
        in_specs=[pl.BlockSpec((CTX_LEN, HEAD_Q), lambda b, h: (ctx_blk + b, h)),
                  pl.BlockSpec((CTX_LEN, LANES), lambda b, h: (ctx_blk + b, h)),
                  pl.BlockSpec((CTX_LEN, LANES), lambda b, h: (ctx_blk + b, 0)),
                  pl.BlockSpec((CTX_LEN, LANES), lambda b, h: (ctx_blk + b, N_HEADS + h))],
        out_specs=pl.BlockSpec((CTX_LEN, LANES), lambda b, h: (b, h)),
        scratch_shapes=[pltpu.VMEM((CTX_LEN, HEAD_Q), BF16)],
        compiler_params=_params("parallel", "parallel"),
        name="context_attention",
    )(q, kv, kr, kv)


def _chunk_block(b, s, reverse):
    if reverse:
        return jnp.where(s < CHUNKS_CTX, ROWS_LAT // CHUNK + CHUNKS_CTX * b + (CHUNKS_CTX - 1 - s),
                         CHUNKS_LAT * b + (CHUNKS_SEQ - 1 - s))
    return jnp.where(s < CHUNKS_CTX, ROWS_LAT // CHUNK + CHUNKS_CTX * b + s,
                     CHUNKS_LAT * b + (s - CHUNKS_CTX))


def _expand_heads(v, e3_ref):
    hi = v.astype(BF16)
    r1 = v - hi.astype(F32)
    mid = r1.astype(BF16)
    lo = (r1 - mid.astype(F32)).astype(BF16)
    return jnp.dot(jnp.concatenate([hi, mid, lo], axis=1), e3_ref[...], preferred_element_type=F32)


def _ssd_chunk(xcol, dt_raw, dtb_ref, a_ref, e3_ref, state_ref, reverse):
    off = SSM_HEADS if reverse else 0
    heads = slice(off, off + SSM_HEADS)
    row = lax.broadcasted_iota(jnp.int32, (CHUNK, CHUNK), 0)
    colm = lax.broadcasted_iota(jnp.int32, (CHUNK, CHUNK), 1)
    keep = (colm >= row) if reverse else (colm <= row)
    tri = keep.astype(F32)
    dt = jax.nn.softplus(dt_raw + dtb_ref[...])
    a = dt * a_ref[...]
    cum = jnp.dot(tri, a, precision=HIGHEST, preferred_element_type=F32)
    cum2 = cum * LOG2E
    cum2_t = cum2.T
    last_row = 0 if reverse else CHUNK - 1
    dec_h = jnp.exp2(cum2)
    end_h = dt * jnp.exp2(cum2[last_row:last_row + 1, :] - cum2)
    dt_full = _expand_heads(dt[:, heads], e3_ref)
    end_full = _expand_heads(end_h[:, heads], e3_ref)
    dec_in = _expand_heads(dec_h[:, heads], e3_ref)
    chunk_dec = dec_in[last_row:last_row + 1, :]
    xs = xcol(0, D_INNER)
    xcb = (xs * dt_full).astype(BF16)
    xce = (xs * end_full).astype(BF16)
    gw = SSM_HEADS // SSM_GROUPS * SSM_HEAD_DIM
    lane = lax.broadcasted_iota(jnp.int32, (CHUNK, LANES), 1)
    ys = []
    for g in range(SSM_GROUPS):
        b32 = xcol(D_INNER + g * D_STATE, D_INNER + (g + 1) * D_STATE)
        bg = b32.astype(BF16)
        cg = xcol(D_INNER + (SSM_GROUPS + g) * D_STATE, D_INNER + (SSM_GROUPS + g + 1) * D_STATE).astype(BF16)
        cb = lax.dot_general(cg, bg, (((1,), (1,)), ((), ())), preferred_element_type=F32)
        st = state_ref[g]
        y_off = jnp.dot(cg, st.astype(BF16), preferred_element_type=F32) * dec_in[:, g * gw:(g + 1) * gw]
        upd = jnp.dot(b32.T.astype(BF16), xce[:, g * gw:(g + 1) * gw],
                      preferred_element_type=F32)
        state_ref[g] = st * chunk_dec[:, g * gw:(g + 1) * gw] + upd
        pairs = []
        for j in range(gw // LANES):
            h0 = off + g * (SSM_HEADS // SSM_GROUPS) + 2 * j
            ms = []
            for hh in (h0, h0 + 1):
                seg = jnp.where(keep, cum2[:, hh:hh + 1] - cum2_t[hh:hh + 1, :], -jnp.inf)
                ms.append((jnp.exp2(seg) * cb).astype(BF16))
            xp = xcb[:, g * gw + j * LANES:g * gw + (j + 1) * LANES]
            rhs = jnp.concatenate([jnp.where(lane < SSM_HEAD_DIM, xp, 0), jnp.where(lane >= SSM_HEAD_DIM, xp, 0)],
                                  axis=0)
            pairs.append(jnp.dot(jnp.concatenate(ms, axis=1), rhs, preferred_element_type=F32))
        ys.append(jnp.concatenate(pairs, axis=1) + y_off)
    return jnp.concatenate(ys, axis=1)


def _ssd_fwd_kernel(prev_ref, cur_ref, next_ref, dt_ref, shift_ref, cw_ref, cbias_ref, dtb_ref, a_ref, e3_ref,
                    y_ref, xbc_ref, state_ref, xs_ref):
    s = pl.program_id(1)

    @pl.when(s == 0)
    def _():
        state_ref[...] = jnp.zeros_like(state_ref)

    first = (s == 0) | (s == CHUNKS_CTX)
    final = (s == CHUNKS_CTX - 1) | (s == CHUNKS_SEQ - 1)
    halo_zeros = jnp.zeros(prev_ref.shape, prev_ref.dtype)
    prev = jnp.where(first, halo_zeros, prev_ref[...])
    nxt = jnp.where(final, halo_zeros, next_ref[...])
    ext = jnp.concatenate([prev, cur_ref[...], nxt], axis=0)
    for c0 in range(0, CONV_DIM, CONV_COL_TILE):
        cs = slice(c0, c0 + CONV_COL_TILE)
        shifted = jnp.dot(shift_ref[...], ext[:, cs], preferred_element_type=F32)
        acc = cbias_ref[:, cs] + cw_ref[CONV_K // 2:CONV_K // 2 + 1, cs] * cur_ref[:, cs].astype(F32)
        for j, k in enumerate(OFF_CENTRE_TAPS):
            acc = acc + cw_ref[k:k + 1, cs] * shifted[j * CHUNK:(j + 1) * CHUNK, :]
        xs = _silu(acc)
        xs_ref[:, cs] = xs
        xbc_ref[:, cs] = xs.astype(xbc_ref.dtype)
    y_ref[...] = _ssd_chunk(lambda a, b: xs_ref[:, a:b], dt_ref[...], dtb_ref, a_ref, e3_ref, state_ref,
                            reverse=False)


def _ssd_bwd_kernel(xbc_ref, dt_ref, yf_ref, z0_ref, z1_ref, dtb_ref, a_ref, e3_ref, dsum_ref, nw_ref,
                    u_ref, state_ref):
    @pl.when(pl.program_id(1) == 0)
    def _():
        state_ref[...] = jnp.zeros_like(state_ref)

    xcol = lambda a, b: xbc_ref[:, a:b].astype(F32)
    y = _ssd_chunk(xcol, dt_ref[...], dtb_ref, a_ref, e3_ref, state_ref, reverse=True)
    y = y + yf_ref[...] + dsum_ref[...] * xcol(0, D_INNER)
    z = jnp.concatenate([z0_ref[...], z1_ref[...]], axis=1).astype(F32)
    u = y * _silu(z)
    gw = D_INNER // SSM_GROUPS
    for g in range(SSM_GROUPS):
        ug = u[:, g * gw:(g + 1) * gw]
        u_ref[:, g * gw:(g + 1) * gw] = (_rms(ug) * nw_ref[:, g * gw:(g + 1) * gw]).astype(u_ref.dtype)


def ssd_mixer(p_main, dt_raw, conv_w, conv_b, dt_bias, a_log, d_skip, ssm_norm, e3):
    zb = COL_Z // (D_INNER // 2)
    dtb = dt_bias.reshape(1, 2 * SSM_HEADS).astype(F32)
    a_neg = (-jnp.exp(a_log.astype(F32))).reshape(1, 2 * SSM_HEADS)
    dsum = jnp.repeat(d_skip[0] + d_skip[1], SSM_HEAD_DIM).reshape(1, D_INNER)
    halo_per_chunk = CHUNK // HALO_ROWS
    n_halo = ROWS // HALO_ROWS
    shift_np = np.zeros((len(OFF_CENTRE_TAPS) * CHUNK, CHUNK + 2 * HALO_ROWS), np.float32)
    for j, k in enumerate(OFF_CENTRE_TAPS):
        shift_np[j * CHUNK + np.arange(CHUNK), HALO_ROWS + np.arange(CHUNK) + k - CONV_K // 2] = 1.0
    shift = jnp.asarray(shift_np, dtype=BF16)
    full = lambda shape: pl.BlockSpec(shape, lambda b, s: tuple(0 for _ in shape))

    def blk(reverse):
        return lambda b, s: (_chunk_block(b, s, reverse), 0)

    fwd_blk = blk(False)
    y_f, xbc = pl.pallas_call(
        _ssd_fwd_kernel,
        out_shape=(jax.ShapeDtypeStruct((ROWS, D_INNER), F32), jax.ShapeDtypeStruct((ROWS, CONV_DIM), BF16)),
        grid=(BATCH, CHUNKS_SEQ),
        in_specs=[
            pl.BlockSpec((HALO_ROWS, CONV_DIM),
                         lambda b, s: (jnp.maximum(_chunk_block(b, s, False) * halo_per_chunk - 1, 0), 0)),
            pl.BlockSpec((CHUNK, CONV_DIM), fwd_blk),
            pl.BlockSpec((HALO_ROWS, CONV_DIM),
                         lambda b, s: (jnp.minimum((_chunk_block(b, s, False) + 1) * halo_per_chunk, n_halo - 1), 0)),
            pl.BlockSpec((CHUNK, LANES), fwd_blk),
            full(shift.shape),
            full((CONV_K, CONV_DIM)), full((1, CONV_DIM)), full((1, LANES)), full((1, LANES)),
            full((3 * SSM_HEADS, D_INNER)),
        ],
        out_specs=(pl.BlockSpec((CHUNK, D_INNER), fwd_blk), pl.BlockSpec((CHUNK, CONV_DIM), fwd_blk)),
        scratch_shapes=[pltpu.VMEM((SSM_GROUPS, D_STATE, D_INNER // SSM_GROUPS), F32),
                        pltpu.VMEM((CHUNK, CONV_DIM), F32)],
        compiler_params=_params("parallel", "arbitrary"),
        name="ssd_forward",
    )(p_main, p_main, p_main, dt_raw, shift, conv_w, conv_b.reshape(1, CONV_DIM), dtb, a_neg, e3)
    bwd_blk = blk(True)
    return pl.pallas_call(
        _ssd_bwd_kernel,
        out_shape=jax.ShapeDtypeStruct((ROWS, D_INNER), BF16),
        grid=(BATCH, CHUNKS_SEQ),
        in_specs=[
            pl.BlockSpec((CHUNK, CONV_DIM), bwd_blk),
            pl.BlockSpec((CHUNK, LANES), bwd_blk),
            pl.BlockSpec((CHUNK, D_INNER), bwd_blk),
            pl.BlockSpec((CHUNK, D_INNER // 2), lambda b, s: (_chunk_block(b, s, True), zb)),
            pl.BlockSpec((CHUNK, D_INNER // 2), lambda b, s: (_chunk_block(b, s, True), zb + 1)),
            full((1, LANES)), full((1, LANES)), full((3 * SSM_HEADS, D_INNER)),
            full((1, D_INNER)), full((1, D_INNER)),
        ],
        out_specs=pl.BlockSpec((CHUNK, D_INNER), bwd_blk),
        scratch_shapes=[pltpu.VMEM((SSM_GROUPS, D_STATE, D_INNER // SSM_GROUPS), F32)],
        compiler_params=_params("parallel", "arbitrary"),
        name="ssd_backward",
    )(xbc, dt_raw, y_f, p_main, p_main, dtb, a_neg, e3, dsum, ssm_norm.reshape(1, D_INNER))


def _merge_kernel(attl_ref, attc_ref, u_ref, woa_ref, wob_ref, ga_ref, gb_ref, o_ref):
    att = jnp.where(pl.program_id(0) < N_LAT_TILES, attl_ref[...], attc_ref[...])
    o_a = jnp.dot(att, woa_ref[...], preferred_element_type=F32)
    o_b = jnp.dot(u_ref[...], wob_ref[...], preferred_element_type=F32)
    o_ref[...] = (jax.nn.sigmoid(ga_ref[...].astype(F32)) * o_a
                  + jax.nn.sigmoid(gb_ref[...].astype(F32)) * o_b).astype(o_ref.dtype)


def merge_branches(att_lat, att_ctx, u, p_main, w_oa, w_ob, rows):
    tn = 512
    return pl.pallas_call(
        _merge_kernel,
        out_shape=jax.ShapeDtypeStruct((rows, D_MODEL), BF16),
        grid=(rows // TM, D_MODEL // tn),
        in_specs=[pl.BlockSpec((TM, D_MODEL), lambda i, j: (jnp.minimum(i, N_LAT_TILES - 1), 0)),
                  pl.BlockSpec((TM, D_MODEL), lambda i, j: (0, 0)),
                  pl.BlockSpec((TM, D_INNER), lambda i, j: (i, 0)),
                  pl.BlockSpec((D_MODEL, tn), lambda i, j: (0, j)),
                  pl.BlockSpec((D_INNER, tn), lambda i, j: (0, j)),
                  pl.BlockSpec((TM, tn), lambda i, j: (i, COL_GA // tn + j)),
                  pl.BlockSpec((TM, tn), lambda i, j: (i, COL_GB // tn + j))],
        out_specs=pl.BlockSpec((TM, tn), lambda i, j: (i, j)),
        compiler_params=_params("parallel", "parallel"),
        name="merge_branches",
    )(att_lat, att_ctx, u, w_oa, w_ob, p_main, p_main)


def _route(h, wr_ref):
    logits = jnp.dot(h, wr_ref[...], precision=HIGHEST, preferred_element_type=F32)
    lane = lax.broadcasted_iota(jnp.int32, logits.shape, 1)
    lg = jnp.where(lane < N_EXPERTS, logits, -jnp.inf)
    v1 = jnp.max(lg, axis=-1, keepdims=True)
    i1 = jnp.min(jnp.where(lg == v1, lane, LANES), axis=-1, keepdims=True)
    lg2 = jnp.where(lane == i1, -jnp.inf, lg)
    v2 = jnp.max(lg2, axis=-1, keepdims=True)
    i2 = jnp.min(jnp.where(lg2 == v2, lane, LANES), axis=-1, keepdims=True)
    e = jnp.exp(v2 - v1)
    g1 = 1.0 / (1.0 + e)
    g2 = e / (1.0 + e)
    return jnp.where(lane == 0, i1.astype(F32),
                     jnp.where(lane == 1, i2.astype(F32),
                               jnp.where(lane == 2, g1, jnp.where(lane == 3, g2, 0.0))))


def _out_res_kernel(m_ref, w_ref, x_ref, gate_ref, g2_ref, sh_ref, sc_ref, *rest, route):
    if route:
        wr_ref, xo_ref, ho_ref, ro_ref = rest
    else:
        xo_ref, ho_ref = rest
    acc = jnp.dot(m_ref[...], w_ref[...], preferred_element_type=F32)
    xn = x_ref[...] + gate_ref[...] * acc
    xo_ref[...] = xn
    h = (_rms(xn) * g2_ref[...]) * (1 + sc_ref[...]) + sh_ref[...]
    ho_ref[...] = h.astype(ho_ref.dtype)
    if route:
        ro_ref[...] = _route(h, wr_ref)


def out_proj_residual(mixed, w_out, x, g2, mods, layer, rows, w_router=None):
    route = w_router is not None
    h_dtype = F32 if route else BF16
    in_specs = [pl.BlockSpec((TM, D_MODEL), lambda i: (i, 0)),
                pl.BlockSpec((D_MODEL, D_MODEL), lambda i: (0, 0)),
                pl.BlockSpec((TM, D_MODEL), lambda i: (i, 0)),
                _mod_spec(layer, 2), pl.BlockSpec((1, D_MODEL), lambda i: (0, 0)),
                _mod_spec(layer, 3), _mod_spec(layer, 4)]
    args = [mixed, w_out, x, mods, g2.reshape(1, D_MODEL), mods, mods]
    out_shape = [jax.ShapeDtypeStruct((rows, D_MODEL), F32), jax.ShapeDtypeStruct((rows, D_MODEL), h_dtype)]
    out_specs = [pl.BlockSpec((TM, D_MODEL), lambda i: (i, 0)), pl.BlockSpec((TM, D_MODEL), lambda i: (i, 0))]
    if route:
        in_specs.append(pl.BlockSpec((D_MODEL, LANES), lambda i: (0, 0)))
        args.append(w_router)
        out_shape.append(jax.ShapeDtypeStruct((rows, LANES), F32))
        out_specs.append(pl.BlockSpec((TM, LANES), lambda i: (i, 0)))
    return pl.pallas_call(
        functools.partial(_out_res_kernel, route=route),
        out_shape=tuple(out_shape),
        grid=(rows // TM,),
        in_specs=in_specs,
        out_specs=tuple(out_specs),
        compiler_params=_params("parallel"),
        name="out_proj_residual",
    )(*args)


def _ffn_up_kernel(h_ref, w1_ref, w3_ref, o_ref):
    h = h_ref[...]
    a = jnp.dot(h, w1_ref[...], preferred_element_type=F32)
    b = jnp.dot(h, w3_ref[...], preferred_element_type=F32)
    o_ref[...] = (_silu(a) * b).astype(o_ref.dtype)


def ffn_up(h, w1, w3):
    tn = 512
    return pl.pallas_call(
        _ffn_up_kernel,
        out_shape=jax.ShapeDtypeStruct((ROWS, D_FF), BF16),
        grid=(ROWS // TM, D_FF // tn),
        in_specs=[pl.BlockSpec((TM, D_MODEL), lambda i, j: (i, 0)),
                  pl.BlockSpec((D_MODEL, tn), lambda i, j: (0, j)),
                  pl.BlockSpec((D_MODEL, tn), lambda i, j: (0, j))],
        out_specs=pl.BlockSpec((TM, tn), lambda i, j: (i, j)),
        compiler_params=_params("parallel", "parallel"),
        name="ffn_up",
    )(h, w1, w3)


def _ffn_down_kernel(g_ref, w_ref, x_ref, gate_ref, o_ref):
    acc = jnp.dot(g_ref[...], w_ref[...], preferred_element_type=F32)
    o_ref[...] = x_ref[...] + gate_ref[...] * acc


def ffn_down_residual(g, w2, x, mods, layer):
    tn = 512
    gate_spec = pl.BlockSpec((None, None, None, 1, tn), lambda i, j: (layer, _mod_row(i), 5, 0, j))
    return pl.pallas_call(
        _ffn_down_kernel,
        out_shape=jax.ShapeDtypeStruct((ROWS, D_MODEL), F32),
        grid=(ROWS // TM, D_MODEL // tn),
        in_specs=[pl.BlockSpec((TM, D_FF), lambda i, j: (i, 0)),
                  pl.BlockSpec((D_FF, tn), lambda i, j: (0, j)),
                  pl.BlockSpec((TM, tn), lambda i, j: (i, j)),
                  gate_spec],
        out_specs=pl.BlockSpec((TM, tn), lambda i, j: (i, j)),
        compiler_params=_params("parallel", "parallel"),
        name="ffn_down_residual",
    )(g, w2, x, mods)


def _gather_rows_kernel(src_ref, nvalid_ref, h_hbm, o_ref, rows_ref, sem):
    step = pl.program_id(0)
    base = step * GATHER_ROWS

    @pl.when(step == 0)
    def _():
        rows_ref[...] = jnp.zeros_like(rows_ref)

    n_groups = lax.shift_right_logical(nvalid_ref[step] + (DMA_UNROLL - 1), DMA_UNROLL.bit_length() - 1)

    def row_copy(j):
        return pltpu.make_async_copy(h_hbm.at[pl.ds(src_ref[base + j], 1)], rows_ref.at[pl.ds(j, 1)], sem)

    def start(gi, c):
        for u in range(DMA_UNROLL):
            row_copy(gi * DMA_UNROLL + u).start()
        return c

    def wait(gi, c):
        for u in range(DMA_UNROLL):
            row_copy(gi * DMA_UNROLL + u).wait()
        return c

    lax.fori_loop(0, n_groups, start, 0)
    lax.fori_loop(0, n_groups, wait, 0)
    o_ref[...] = rows_ref[...].astype(o_ref.dtype)


def gather_rows(h, src, step_rows):
    n = src.shape[0]
    d = h.shape[1]
    grid_spec = pltpu.PrefetchScalarGridSpec(
        num_scalar_prefetch=2, grid=(n // GATHER_ROWS,),
        in_specs=[pl.BlockSpec(memory_space=pl.ANY)],
        out_specs=pl.BlockSpec((GATHER_ROWS, d), lambda i, s, nv: (i, 0)),
        scratch_shapes=[pltpu.VMEM((GATHER_ROWS, d), h.dtype), pltpu.SemaphoreType.DMA(())])
    return pl.pallas_call(
        _gather_rows_kernel,
        out_shape=jax.ShapeDtypeStruct((n, d), BF16),
        grid_spec=grid_spec,
        compiler_params=_params("arbitrary"),
        name="moe_gather",
    )(src, step_rows, h)


def _moe_kernel(be_ref, nused_ref, nhalf_ref, x_ref, w1_ref, w3_ref, w2_ref, o_ref):
    i = pl.program_id(0)
    f = pl.program_id(1)

    @pl.when(f == 0)
    def _():
        o_ref[...] = jnp.zeros_like(o_ref)

    def mlp(rows):
        x = x_ref[0:rows, :]
        h1 = jnp.dot(x, w1_ref[...].astype(BF16), preferred_element_type=F32)
        h3 = jnp.dot(x, w3_ref[...].astype(BF16), preferred_element_type=F32)
        g = _silu(h1) * h3
        o_ref[0:rows, :] += jnp.dot(g.astype(BF16), w2_ref[...].astype(BF16), preferred_element_type=F32)

    @pl.when(nhalf_ref[i] == 1)
    def _():
        mlp(MOE_ROW_TILE // 2)

    @pl.when(nhalf_ref[i] == 2)
    def _():
        mlp(MOE_ROW_TILE)


def moe_experts(buf, block_expert, n_used, block_halves, w1, w3, w2):
    r, d = buf.shape
    tm, tf = MOE_ROW_TILE, MOE_FF_TILE
    nf = D_FF_EXPERT // tf

    def f_idx(i, f, nu):
        return jnp.where(i < nu[0], f, nf - 1)

    grid_spec = pltpu.PrefetchScalarGridSpec(
        num_scalar_prefetch=3,
        grid=(r // tm, nf),
        in_specs=[
            pl.BlockSpec((tm, d), lambda i, f, be, nu, nh: (jnp.minimum(i, nu[0] - 1), 0)),
            pl.BlockSpec((None, d, tf), lambda i, f, be, nu, nh: (be[i], 0, f_idx(i, f, nu))),
            pl.BlockSpec((None, d, tf), lambda i, f, be, nu, nh: (be[i], 0, f_idx(i, f, nu))),
            pl.BlockSpec((None, tf, d), lambda i, f, be, nu, nh: (be[i], f_idx(i, f, nu), 0)),
        ],
        out_specs=pl.BlockSpec((tm, d), lambda i, f, be, nu, nh: (i, 0)),
    )
    return pl.pallas_call(
        _moe_kernel,
        out_shape=jax.ShapeDtypeStruct((r, d), F32),
        grid_spec=grid_spec,
        compiler_params=_params("arbitrary", "arbitrary"),
        name="moe_experts",
    )(block_expert, n_used, block_halves, buf, w1, w3, w2)


def _combine_kernel(dest_ref, y_hbm, x_ref, route_ref, gate_ref, gfin_ref, o_ref, ybuf_ref, sem):
    base = pl.program_id(0) * GATHER_ROWS

    def row_copy(t, k):
        return pltpu.make_async_copy(y_hbm.at[pl.ds(dest_ref[TOP_K * (base + t) + k], 1)],
                                     ybuf_ref.at[k, pl.ds(t, 1)], sem)

    def start(gi, c):
        for u in range(DMA_UNROLL):
            for k in range(TOP_K):
                row_copy(gi * DMA_UNROLL + u, k).start()
        return c

    def wait(gi, c):
        for u in range(DMA_UNROLL):
            for k in range(TOP_K):
                row_copy(gi * DMA_UNROLL + u, k).wait()
        return c

    lax.fori_loop(0, GATHER_ROWS // DMA_UNROLL, start, 0)
    lax.fori_loop(0, GATHER_ROWS // DMA_UNROLL, wait, 0)
    route = route_ref[...]
    f = ybuf_ref[0] * route[:, 2:3] + ybuf_ref[1] * route[:, 3:4]
    xn = x_ref[...] + gate_ref[...] * f
    o_ref[...] = _rms(xn) * gfin_ref[...]


def moe_combine_final(y, dest, x, route, mods, layer, final_norm):
    tiles_per_batch = SEQ // GATHER_ROWS
    grid_spec = pltpu.PrefetchScalarGridSpec(
        num_scalar_prefetch=1, grid=(ROWS_LAT // GATHER_ROWS,),
        in_specs=[pl.BlockSpec(memory_space=pl.ANY),
                  pl.BlockSpec((GATHER_ROWS, D_MODEL), lambda i, d: (i, 0)),
                  pl.BlockSpec((GATHER_ROWS, LANES), lambda i, d: (i, 0)),
                  pl.BlockSpec((None, None, None, 1, D_MODEL), lambda i, d: (layer, i // tiles_per_batch, 5, 0, 0)),
                  pl.BlockSpec((1, D_MODEL), lambda i, d: (0, 0))],
        out_specs=pl.BlockSpec((GATHER_ROWS, D_MODEL), lambda i, d: (i, 0)),
        scratch_shapes=[pltpu.VMEM((TOP_K, GATHER_ROWS, D_MODEL), F32), pltpu.SemaphoreType.DMA(())])
    return pl.pallas_call(
        _combine_kernel,
        out_shape=jax.ShapeDtypeStruct((ROWS_LAT, D_MODEL), F32),
        grid_spec=grid_spec,
        compiler_params=_params("arbitrary"),
        name="moe_combine_final",
    )(dest, y, x, route, mods, final_norm.reshape(1, D_MODEL))


def moe_dispatch(route):
    expert = route[:, :TOP_K].astype(jnp.int32).reshape(-1)
    onehot = (expert[:, None] == jnp.arange(N_EXPERTS)[None, :]).astype(jnp.int32)
    incl = jnp.cumsum(onehot, axis=0)
    rank = jnp.sum((incl - onehot) * onehot, axis=1)
    counts = incl[-1]
    tm = MOE_ROW_TILE
    padded = (counts + tm - 1) // tm * tm
    pad_end = jnp.cumsum(padded)
    dest = ((pad_end - padded)[expert] + rank).astype(jnp.int32)
    n_blocks = MOE_SLOTS // tm
    n_used = (pad_end[-1] // tm).astype(jnp.int32)
    blocks = jnp.arange(n_blocks)
    block_expert = jnp.minimum(jnp.searchsorted(pad_end, blocks * tm, side='right'), N_EXPERTS - 1).astype(jnp.int32)
    block_rows = jnp.clip(counts[block_expert] - (blocks * tm - (pad_end - padded)[block_expert]), 0, tm)
    block_rows = jnp.where(blocks < n_used, block_rows, 0)
    block_expert = jnp.where(blocks < n_used, block_expert, block_expert[jnp.maximum(n_used - 1, 0)])
    block_halves = ((block_rows + tm // 2 - 1) // (tm // 2)).astype(jnp.int32)
    steps = jnp.arange(MOE_SLOTS // GATHER_ROWS)
    per_block = tm // GATHER_ROWS
    step_rows = jnp.clip(block_rows[steps // per_block] - (steps % per_block) * GATHER_ROWS, 0, GATHER_ROWS)
    token = jnp.repeat(jnp.arange(ROWS_LAT, dtype=jnp.int32), TOP_K)
    src = jnp.zeros((MOE_SLOTS,), jnp.int32).at[dest].set(token)
    return dest, src, step_rows.astype(jnp.int32), block_expert, n_used.reshape(1), block_halves


def _rope_table(n):
    rows = n // GRID_W
    row = jnp.repeat(jnp.arange(rows, dtype=F32), GRID_W)
    col = jnp.tile(jnp.arange(GRID_W, dtype=F32), rows)
    axis_dim = ROPE_DIM // 2
    inv = ROPE_BASE ** (-jnp.arange(0, axis_dim, 2, dtype=F32) / axis_dim)
    ang_r, ang_c = row[:, None] * inv, col[:, None] * inv
    cr, sr, cc, sc = jnp.cos(ang_r), jnp.sin(ang_r), jnp.cos(ang_c), jnp.sin(ang_c)
    lat = jnp.concatenate([cr, cr, cc, cc, -sr, sr, -sc, sc], axis=1)
    ident = jnp.concatenate([jnp.ones((TM, ROPE_DIM), F32), jnp.zeros((TM, ROPE_DIM), F32)], axis=1)
    return jnp.concatenate([lat, ident], axis=0)


def _layer_weights(w_in, w_uq, w_ukv):
    o = np.cumsum((0,) + PROJ_SIZES)
    seg = lambda i: w_in[:, int(o[i]):int(o[i + 1])]
    w_main = jnp.concatenate([seg(4), seg(3), seg(6), seg(7), seg(0), seg(1)], axis=1).astype(BF16)
    kr_w = seg(2)
    w_small = jnp.concatenate([seg(5), kr_w, kr_w[:, ROPE_SWAP]], axis=1).astype(BF16)
    wq = w_uq.reshape(Q_LORA, N_HEADS, NOPE_DIM + ROPE_DIM)
    rope_w = wq[:, :, NOPE_DIM:]
    w_q = jnp.concatenate([wq[:, :, :NOPE_DIM], rope_w, rope_w[:, :, ROPE_SWAP]], axis=2)
    w_q = w_q.reshape(Q_LORA, N_HEADS * HEAD_Q).astype(BF16)
    wkv = w_ukv.reshape(KV_LORA, N_HEADS, NOPE_DIM + V_DIM)
    w_kv = jnp.concatenate([wkv[:, :, :NOPE_DIM].reshape(KV_LORA, -1), wkv[:, :, NOPE_DIM:].reshape(KV_LORA, -1)],
                           axis=1).astype(BF16)
    return w_main, w_small, w_q, w_kv


def kernel(x, c, ctx, c_ctx, norm_mix, norm_ffn, w_ada, b_ada, w_in, q_norm, w_uq, kv_norm, w_ukv,
           conv_w, conv_b, a_log, dt_bias, d_skip, ssm_norm, w_oa, w_ob, w_out, w1_dense, w3_dense,
           w2_dense, w_router, w1_moe, w3_moe, w2_moe, final_norm):
    tab = _rope_table(SEQ)
    cc = jnp.concatenate([c, c_ctx[None], jnp.zeros((8 - BATCH - 1, D_MODEL), F32)], axis=0)
    mods = adaln(cc, w_ada, b_ada)
    head_of = np.arange(D_INNER) // SSM_HEAD_DIM
    e1 = (np.arange(SSM_HEADS)[:, None] == head_of[None, :]).astype(np.float32)
    e3 = jnp.asarray(np.concatenate([e1, e1, e1], axis=0), dtype=BF16)
    xr = jnp.concatenate([x.reshape(ROWS_LAT, D_MODEL), ctx.reshape(ROWS_CTX, D_MODEL)], axis=0)
    out = None
    for l in range(DEPTH):
        last = l == DEPTH - 1
        rows = ROWS_LAT if last else ROWS
        w_main, w_small, w_q, w_kv = _layer_weights(w_in[l], w_uq[l], w_ukv[l])
        h = norm_mod(xr, norm_mix[l], mods, l, 0, 1)
        p_main = matmul(h, w_main, BF16)
        dt_raw, kr = proj_small(h, w_small, tab)
        q = up_proj(p_main, COL_CQ, q_norm[l], w_q, tab)
        kv = up_proj(p_main, COL_CKV, kv_norm[l], w_kv, None)
        att = latent_attention(q, kv, kr)
        att_ctx = att if last else context_attention(q, kv, kr)
        u = ssd_mixer(p_main, dt_raw, conv_w[l], conv_b[l], dt_bias[l], a_log[l], d_skip[l], ssm_norm[l], e3)
        mixed = merge_branches(att, att_ctx, u, p_main, w_oa[l].astype(BF16), w_ob[l].astype(BF16), rows)
        if l % 2 == 0:
            xr, h2 = out_proj_residual(mixed, w_out[l].astype(BF16), xr, norm_ffn[l], mods, l, rows)
            g = ffn_up(h2, w1_dense[l // 2].astype(BF16), w3_dense[l // 2].astype(BF16))
            xr = ffn_down_residual(g, w2_dense[l // 2].astype(BF16), xr, mods, l)
        else:
            wr = jnp.concatenate([w_router[l // 2], jnp.zeros((D_MODEL, LANES - N_EXPERTS), F32)], axis=1)
            x_lat, h2, route = out_proj_residual(mixed, w_out[l].astype(BF16), xr, norm_ffn[l], mods, l, rows, wr)
            dest, src, step_rows, block_expert, n_used, block_halves = moe_dispatch(route)
            buf = gather_rows(h2, src, step_rows)
            y = moe_experts(buf, block_expert, n_used, block_halves, w1_moe[l // 2], w3_moe[l // 2], w2_moe[l // 2])
            out = moe_combine_final(y, dest, x_lat, route, mods, l, final_norm)
    return out.reshape(BATCH, SEQ, D_MODEL)
```

```python
import functools

import numpy as np
import jax
import jax.numpy as jnp
from jax import lax
from jax.experimental import pallas as pl
from jax.experimental.pallas import tpu as pltpu

D_MODEL = 2048
BATCH = 2
SEQ = 4096
DEPTH = 2
CTX_LEN = 256
GRID_W = 64
EPS = 1e-6

N_HEADS = D_MODEL // 128
Q_LORA = 512
KV_LORA = 512
NOPE_DIM = 128
ROPE_DIM = 64
V_DIM = 128
ROPE_BASE = 10000.0
SM_SCALE = (NOPE_DIM + ROPE_DIM) ** -0.5

D_INNER = 2 * D_MODEL
SSM_HEAD_DIM = 64
SSM_HEADS = D_INNER // SSM_HEAD_DIM
SSM_GROUPS = 8
D_STATE = 128
CONV_K = 5
CONV_DIM = D_INNER + 2 * SSM_GROUPS * D_STATE
CHUNK = 128

PROJ_SIZES = (Q_LORA, KV_LORA, ROPE_DIM, D_INNER, CONV_DIM, 2 * SSM_HEADS, D_MODEL, D_MODEL)

D_FF = 256 * ((8 * D_MODEL // 3 + 255) // 256)
N_EXPERTS = 8
TOP_K = 2
D_FF_EXPERT = 7 * D_MODEL // 2

F32 = jnp.float32
BF16 = jnp.bfloat16
HIGHEST = lax.Precision.HIGHEST

VMEM_LIMIT_BYTES = 56 * 1024 * 1024
LANES = 128
HALO_ROWS = 16
OFF_CENTRE_TAPS = (0, 1, 3, 4)
CONV_COL_TILE = 2048
LOG2E = 1.4426950408889634
QK_SCALE = SM_SCALE * LOG2E

ROWS_LAT = BATCH * SEQ
ROWS_CTX = BATCH * CTX_LEN
ROWS = ROWS_LAT + ROWS_CTX
TM = 512
PROJ_COL_TILE = 1536
TILES_PER_BATCH = SEQ // TM
N_LAT_TILES = ROWS_LAT // TM
CHUNKS_LAT = SEQ // CHUNK
CHUNKS_CTX = CTX_LEN // CHUNK
CHUNKS_SEQ = CHUNKS_LAT + CHUNKS_CTX

COL_XBC = 0
COL_Z = COL_XBC + CONV_DIM
COL_GA = COL_Z + D_INNER
COL_GB = COL_GA + D_MODEL
COL_CQ = COL_GB + D_MODEL
COL_CKV = COL_CQ + Q_LORA
N_MAIN = COL_CKV + KV_LORA
HEAD_Q = 2 * LANES

MOE_ROW_TILE = 1024
MOE_FF_TILE = 256
MOE_QUARTERS = 4
MOE_SLOTS = ROWS_LAT * TOP_K + N_EXPERTS * MOE_ROW_TILE
GATHER_ROWS = 256
DMA_UNROLL = 8
ATTN_Q_TILE = 2048
ATTN_KV_CHUNK = 256
ROPE_SWAP = np.concatenate([np.arange(16, 32), np.arange(0, 16), np.arange(48, 64), np.arange(32, 48)])


def _params(*sem):
    return pltpu.CompilerParams(dimension_semantics=sem, vmem_limit_bytes=VMEM_LIMIT_BYTES)


def _mod_row(i):
    return jnp.minimum(i // TILES_PER_BATCH, BATCH)


def _rope_row(i):
    return jnp.where(i < N_LAT_TILES, i % TILES_PER_BATCH, TILES_PER_BATCH)


def _mod_spec(layer, k):
    return pl.BlockSpec((None, None, None, 1, D_MODEL), lambda i, *_: (layer, _mod_row(i), k, 0, 0))


def _rms(x):
    return x * lax.rsqrt(jnp.mean(x * x, axis=-1, keepdims=True) + EPS)


def _silu(x):
    return x * jax.nn.sigmoid(x)


def _rope(x, tab):
    y = x * tab
    y = y + pltpu.roll(y, ROPE_DIM, axis=1)
    lane = lax.broadcasted_iota(jnp.int32, y.shape, 1)
    return jnp.where(lane < ROPE_DIM, y, 0.0)


def _adaln_kernel(c_ref, w_ref, b_ref, o_ref):
    @pl.when(pl.program_id(1) == 0)
    def _():
        o_ref[...] = jnp.broadcast_to(b_ref[...], o_ref.shape)

    a = _silu(c_ref[...]).astype(BF16)
    o_ref[...] += jnp.dot(a, w_ref[...].astype(BF16), preferred_element_type=F32)


def adaln(cc, w_ada, b_ada):
    tk = 256
    n = 6 * D_MODEL
    out = pl.pallas_call(
        _adaln_kernel,
        out_shape=jax.ShapeDtypeStruct((DEPTH, 8, n), F32),
        grid=(DEPTH, D_MODEL // tk),
        in_specs=[pl.BlockSpec((8, tk), lambda l, k: (0, k)),
                  pl.BlockSpec((None, tk, n), lambda l, k: (l, k, 0)),
                  pl.BlockSpec((None, 1, n), lambda l, k: (l, 0, 0))],
        out_specs=pl.BlockSpec((None, 8, n), lambda l, k: (l, 0, 0)),
        compiler_params=_params("parallel", "arbitrary"),
        name="adaln",
    )(cc, w_ada, b_ada.reshape(DEPTH, 1, n))
    return out.reshape(DEPTH, 8, 6, 1, D_MODEL)


def _norm_proj_kernel(x_ref, g_ref, sh_ref, sc_ref, w_ref, h_ref, o_ref):
    @pl.when(pl.program_id(1) == 0)
    def _():
        y = _rms(x_ref[...]) * g_ref[...]
        h_ref[...] = (y * (1 + sc_ref[...]) + sh_ref[...]).astype(h_ref.dtype)

    o_ref[...] = jnp.dot(h_ref[...], w_ref[...], preferred_element_type=F32).astype(o_ref.dtype)


def norm_proj(x, g, mods, layer, w):
    n = w.shape[1]
    tn = PROJ_COL_TILE
    return pl.pallas_call(
        _norm_proj_kernel,
        out_shape=(jax.ShapeDtypeStruct((ROWS, D_MODEL), BF16), jax.ShapeDtypeStruct((ROWS, n), BF16)),
        grid=(ROWS // TM, n // tn),
        in_specs=[pl.BlockSpec((TM, D_MODEL), lambda i, j: (i, 0)),
                  pl.BlockSpec((1, D_MODEL), lambda i, j: (0, 0)),
                  _mod_spec(layer, 0), _mod_spec(layer, 1),
                  pl.BlockSpec((D_MODEL, tn), lambda i, j: (0, j))],
        out_specs=(pl.BlockSpec((TM, D_MODEL), lambda i, j: (i, 0)),
                   pl.BlockSpec((TM, tn), lambda i, j: (i, j))),
        compiler_params=_params("parallel", "arbitrary"),
        name="norm_proj",
    )(x, g.reshape(1, D_MODEL), mods, mods, w)


def _proj_small_kernel(a_ref, b_ref, tab_ref, dt_ref, kr_ref):
    acc = jnp.dot(a_ref[...], b_ref[...], preferred_element_type=F32)
    dt_ref[...] = acc[:, :LANES]
    kr_ref[...] = _rope(acc[:, LANES:], tab_ref[...]).astype(kr_ref.dtype)


def proj_small(h, w_small, tab):
    return pl.pallas_call(
        _proj_small_kernel,
        out_shape=(jax.ShapeDtypeStruct((ROWS, LANES), F32), jax.ShapeDtypeStruct((ROWS, LANES), BF16)),
        grid=(ROWS // TM,),
        in_specs=[pl.BlockSpec((TM, D_MODEL), lambda i: (i, 0)),
                  pl.BlockSpec((D_MODEL, 2 * LANES), lambda i: (0, 0)),
                  pl.BlockSpec((TM, LANES), lambda i: (_rope_row(i), 0))],
        out_specs=(pl.BlockSpec((TM, LANES), lambda i: (i, 0)),
                   pl.BlockSpec((TM, LANES), lambda i: (i, 0))),
        compiler_params=_params("parallel"),
        name="proj_small",
    )(h, w_small, tab)


def _up_kernel(c_ref, g_ref, w_ref, *rest, rope, heads_per_tile):
    if rope:
        tab_ref, o_ref = rest
    else:
        (o_ref,) = rest
    c = _rms(c_ref[...].astype(F32)) * g_ref[...]
    acc = jnp.dot(c.astype(BF16), w_ref[...], preferred_element_type=F32)
    if not rope:
        o_ref[...] = acc.astype(o_ref.dtype)
        return
    tab = tab_ref[...]
    acc = acc * QK_SCALE
    for hh in range(heads_per_tile):
        base = hh * HEAD_Q
        o_ref[:, base:base + LANES] = acc[:, base:base + LANES].astype(o_ref.dtype)
        o_ref[:, base + LANES:base + HEAD_Q] = _rope(acc[:, base + LANES:base + HEAD_Q], tab).astype(o_ref.dtype)


def up_proj(p_main, col, g, w, tab):
    lora, n = w.shape
    tn = n
    rope = tab is not None
    in_specs = [pl.BlockSpec((TM, lora), lambda i, j: (i, col // lora)),
                pl.BlockSpec((1, lora), lambda i, j: (0, 0)),
                pl.BlockSpec((lora, tn), lambda i, j: (0, j))]
    args = [p_main, g.reshape(1, lora), w]
    if rope:
        in_specs.append(pl.BlockSpec((TM, LANES), lambda i, j: (_rope_row(i), 0)))
        args.append(tab)
    return pl.pallas_call(
        functools.partial(_up_kernel, rope=rope, heads_per_tile=tn // HEAD_Q),
        out_shape=jax.ShapeDtypeStruct((ROWS, n), BF16),
        grid=(ROWS // TM, n // tn),
        in_specs=in_specs,
        out_specs=pl.BlockSpec((TM, tn), lambda i, j: (i, j)),
        compiler_params=_params("parallel", "parallel"),
        name="up_proj_rope" if rope else "up_proj",
    )(*args)


def _softmax_chunks(q, kcat_ref, v_chunks):
    m = l = acc = None
    for start, size, v in v_chunks:
        k = kcat_ref[start:start + size, :]
        s = lax.dot_general(q, k, (((1,), (1,)), ((), ())), preferred_element_type=F32)
        m_cur = jnp.max(s, axis=-1, keepdims=True)
        if m is None:
            m_new = m_cur
            p = jnp.exp2(s - m_new)
            l = jnp.sum(p, axis=-1, keepdims=True)
            acc = jnp.dot(p.astype(BF16), v, preferred_element_type=F32)
        else:
            m_new = jnp.maximum(m, m_cur)
            alpha = jnp.exp2(m - m_new)
            p = jnp.exp2(s - m_new)
            l = alpha * l + jnp.sum(p, axis=-1, keepdims=True)
            acc = alpha * acc + jnp.dot(p.astype(BF16), v, preferred_element_type=F32)
        m = m_new
    return acc / l


def _lat_attn_kernel(q_ref, knl_ref, krl_ref, vl_ref, knc_ref, krc_ref, vc_ref, o_ref, kcat_ref):
    @pl.when(pl.program_id(2) == 0)
    def _():
        kcat_ref[0:CTX_LEN, 0:LANES] = knc_ref[...]
        kcat_ref[0:CTX_LEN, LANES:HEAD_Q] = krc_ref[...]
        kcat_ref[CTX_LEN:, 0:LANES] = knl_ref[...]
        kcat_ref[CTX_LEN:, LANES:HEAD_Q] = krl_ref[...]

    chunks = [(0, CTX_LEN, vc_ref[...])]
    for s in range(0, SEQ, ATTN_KV_CHUNK):
        chunks.append((CTX_LEN + s, ATTN_KV_CHUNK, vl_ref[s:s + ATTN_KV_CHUNK, :]))
    o_ref[...] = _softmax_chunks(q_ref[...], kcat_ref, chunks).astype(o_ref.dtype)


def latent_attention(q, kv, kr):
    tq = ATTN_Q_TILE
    qt = SEQ // tq
    ctx_blk = ROWS_LAT // CTX_LEN
    return pl.pallas_call(
        _lat_attn_kernel,
        out_shape=jax.ShapeDtypeStruct((ROWS_LAT, D_MODEL), BF16),
        grid=(BATCH, N_HEADS, qt),
        in_specs=[pl.BlockSpec((tq, HEAD_Q), lambda b, h, i: (b * qt + i, h)),
                  pl.BlockSpec((SEQ, LANES), lambda b, h, i: (b, h)),
                  pl.BlockSpec((SEQ, LANES), lambda b, h, i: (b, 0)),
                  pl.BlockSpec((SEQ, LANES), lambda b, h, i: (b, N_HEADS + h)),
                  pl.BlockSpec((CTX_LEN, LANES), lambda b, h, i: (ctx_blk + b, h)),
                  pl.BlockSpec((CTX_LEN, LANES), lambda b, h, i: (ctx_blk + b, 0)),
                  pl.BlockSpec((CTX_LEN, LANES), lambda b, h, i: (ctx_blk + b, N_HEADS + h))],
        out_specs=pl.BlockSpec((tq, LANES), lambda b, h, i: (b * qt + i, h)),
        scratch_shapes=[pltpu.VMEM((CTX_LEN + SEQ, HEAD_Q), BF16)],
        compiler_params=_params("parallel", "parallel", "arbitrary"),
        name="latent_attention",
    )(q, kv, kr, kv, kv, kr, kv)


def _ctx_attn_kernel(q_ref, kn_ref, kr_ref, v_ref, o_ref, kcat_ref):
    kcat_ref[:, 0:LANES] = kn_ref[...]
    kcat_ref[:, LANES:HEAD_Q] = kr_ref[...]
    o_ref[...] = _softmax_chunks(q_ref[...], kcat_ref, [(0, CTX_LEN, v_ref[...])]).astype(o_ref.dtype)


def context_attention(q, kv, kr):
    ctx_blk = ROWS_LAT // CTX_LEN
    return pl.pallas_call(
        _ctx_attn_kernel,
        out_shape=jax.ShapeDtypeStruct((ROWS_CTX, D_MODEL), BF16),
        grid=(BATCH, N_HEADS),
        in_specs=[pl.BlockSpec((CTX_LEN, HEAD_Q), lambda b, h: (ctx_blk + b, h)),
                  pl.BlockSpec((CTX_LEN, LANES), lambda b, h: (ctx_blk + b, h)),
                  pl.BlockSpec((CTX_LEN, LANES), lambda b, h: (ctx_blk + b, 0)),
                  pl.BlockSpec((CTX_LEN, LANES), lambda b, h: (ctx_blk + b, N_HEADS + h))],
        out_specs=pl.BlockSpec((CTX_LEN, LANES), lambda b, h: (b, h)),
        scratch_shapes=[pltpu.VMEM((CTX_LEN, HEAD_Q), BF16)],
        compiler_params=_params("parallel", "parallel"),
        name="context_attention",
    )(q, kv, kr, kv)


def _chunk_block(b, s, reverse):
    if reverse:
        return jnp.where(s < CHUNKS_CTX, ROWS_LAT // CHUNK + CHUNKS_CTX * b + (CHUNKS_CTX - 1 - s),
                         CHUNKS_LAT * b + (CHUNKS_SEQ - 1 - s))
    return jnp.where(s < CHUNKS_CTX, ROWS_LAT // CHUNK + CHUNKS_CTX * b + s,
                     CHUNKS_LAT * b + (s - CHUNKS_CTX))


def _expand_heads(v, e3_ref):
    hi = v.astype(BF16)
    r1 = v - hi.astype(F32)
    mid = r1.astype(BF16)
    lo = (r1 - mid.astype(F32)).astype(BF16)
    return jnp.dot(jnp.concatenate([hi, mid, lo], axis=1), e3_ref[...], preferred_element_type=F32)


def _ssd_chunk(xcol, dt_raw, dtb_ref, a_ref, e3_ref, state_ref, reverse):
    off = SSM_HEADS if reverse else 0
    heads = slice(off, off + SSM_HEADS)
    row = lax.broadcasted_iota(jnp.int32, (CHUNK, CHUNK), 0)
    colm = lax.broadcasted_iota(jnp.int32, (CHUNK, CHUNK), 1)
    keep = (colm >= row) if reverse else (colm <= row)
    tri = keep.astype(F32)
    dt = jax.nn.softplus(dt_raw + dtb_ref[...])
    a = dt * a_ref[...]
    cum = jnp.dot(tri, a, precision=HIGHEST, preferred_element_type=F32)
    cum2 = cum * LOG2E
    cum2_t = cum2.T
    last_row = 0 if reverse else CHUNK - 1
    dec_h = jnp.exp2(cum2)
    end_h = dt * jnp.exp2(cum2[last_row:last_row + 1, :] - cum2)
    dt_full = _expand_heads(dt[:, heads], e3_ref)
    end_full = _expand_heads(end_h[:, heads], e3_ref)
    dec_in = _expand_heads(dec_h[:, heads], e3_ref)
    chunk_dec = dec_in[last_row:last_row + 1, :]
    xs = xcol(0, D_INNER)
    xcb = (xs * dt_full).astype(BF16)
    xce = (xs * end_full).astype(BF16)
    gw = SSM_HEADS // SSM_GROUPS * SSM_HEAD_DIM
    lane = lax.broadcasted_iota(jnp.int32, (CHUNK, LANES), 1)
    ys = []
    for g in range(SSM_GROUPS):
        b32 = xcol(D_INNER + g * D_STATE, D_INNER + (g + 1) * D_STATE)
        bg = b32.astype(BF16)
        cg = xcol(D_INNER + (SSM_GROUPS + g) * D_STATE, D_INNER + (SSM_GROUPS + g + 1) * D_STATE).astype(BF16)
        cb = lax.dot_general(cg, bg, (((1,), (1,)), ((), ())), preferred_element_type=F32)
        st = state_ref[g]
        y_off = jnp.dot(cg, st.astype(BF16), preferred_element_type=F32) * dec_in[:, g * gw:(g + 1) * gw]
        upd = jnp.dot(b32.T.astype(BF16), xce[:, g * gw:(g + 1) * gw],
                      preferred_element_type=F32)
        state_ref[g] = st * chunk_dec[:, g * gw:(g + 1) * gw] + upd
        pairs = []
        for j in range(gw // LANES):
            h0 = off + g * (SSM_HEADS // SSM_GROUPS) + 2 * j
            ms = []
            for hh in (h0, h0 + 1):
                seg = jnp.where(keep, cum2[:, hh:hh + 1] - cum2_t[hh:hh + 1, :], -jnp.inf)
                ms.append((jnp.exp2(seg) * cb).astype(BF16))
            xp = xcb[:, g * gw + j * LANES:g * gw + (j + 1) * LANES]
            rhs = jnp.concatenate([jnp.where(lane < SSM_HEAD_DIM, xp, 0), jnp.where(lane >= SSM_HEAD_DIM, xp, 0)],
                                  axis=0)
            pairs.append(jnp.dot(jnp.concatenate(ms, axis=1), rhs, preferred_element_type=F32))
        ys.append(jnp.concatenate(pairs, axis=1) + y_off)
    return jnp.concatenate(ys, axis=1)


def _ssd_fwd_kernel(prev_ref, cur_ref, next_ref, dt_ref, shift_ref, cw_ref, cbias_ref, dtb_ref, a_ref, e3_ref,
                    y_ref, xbc_ref, state_ref, xs_ref):
    s = pl.program_id(1)

    @pl.when(s == 0)
    def _():
        state_ref[...] = jnp.zeros_like(state_ref)

    first = (s == 0) | (s == CHUNKS_CTX)
    final = (s == CHUNKS_CTX - 1) | (s == CHUNKS_SEQ - 1)
    halo_zeros = jnp.zeros(prev_ref.shape, prev_ref.dtype)
    prev = jnp.where(first, halo_zeros, prev_ref[...])
    nxt = jnp.where(final, halo_zeros, next_ref[...])
    ext = jnp.concatenate([prev, cur_ref[...], nxt], axis=0)
    for c0 in range(0, CONV_DIM, CONV_COL_TILE):
        cs = slice(c0, c0 + CONV_COL_TILE)
        shifted = jnp.dot(shift_ref[...], ext[:, cs], preferred_element_type=F32)
        acc = cbias_ref[:, cs] + cw_ref[CONV_K // 2:CONV_K // 2 + 1, cs] * cur_ref[:, cs].astype(F32)
        for j, k in enumerate(OFF_CENTRE_TAPS):
            acc = acc + cw_ref[k:k + 1, cs] * shifted[j * CHUNK:(j + 1) * CHUNK, :]
        xs = _silu(acc)
        xs_ref[:, cs] = xs
        xbc_ref[:, cs] = xs.astype(xbc_ref.dtype)
    y_ref[...] = _ssd_chunk(lambda a, b: xs_ref[:, a:b], dt_ref[...], dtb_ref, a_ref, e3_ref, state_ref,
                            reverse=False)


def _ssd_bwd_kernel(xbc_ref, dt_ref, yf_ref, z0_ref, z1_ref, dtb_ref, a_ref, e3_ref, dsum_ref, nw_ref,
                    u_ref, state_ref):
    @pl.when(pl.program_id(1) == 0)
    def _():
        state_ref[...] = jnp.zeros_like(state_ref)

    xcol = lambda a, b: xbc_ref[:, a:b].astype(F32)
    y = _ssd_chunk(xcol, dt_ref[...], dtb_ref, a_ref, e3_ref, state_ref, reverse=True)
    y = y + yf_ref[...] + dsum_ref[...] * xcol(0, D_INNER)
    z = jnp.concatenate([z0_ref[...], z1_ref[...]], axis=1).astype(F32)
    u = y * _silu(z)
    gw = D_INNER // SSM_GROUPS
    for g in range(SSM_GROUPS):
        ug = u[:, g * gw:(g + 1) * gw]
        u_ref[:, g * gw:(g + 1) * gw] = (_rms(ug) * nw_ref[:, g * gw:(g + 1) * gw]).astype(u_ref.dtype)


def ssd_mixer(p_main, dt_raw, conv_w, conv_b, dt_bias, a_log, d_skip, ssm_norm, e3):
    zb = COL_Z // (D_INNER // 2)
    dtb = dt_bias.reshape(1, 2 * SSM_HEADS).astype(F32)
    a_neg = (-jnp.exp(a_log.astype(F32))).reshape(1, 2 * SSM_HEADS)
    dsum = jnp.repeat(d_skip[0] + d_skip[1], SSM_HEAD_DIM).reshape(1, D_INNER)
    halo_per_chunk = CHUNK // HALO_ROWS
    n_halo = ROWS // HALO_ROWS
    shift_np = np.zeros((len(OFF_CENTRE_TAPS) * CHUNK, CHUNK + 2 * HALO_ROWS), np.float32)
    for j, k in enumerate(OFF_CENTRE_TAPS):
        shift_np[j * CHUNK + np.arange(CHUNK), HALO_ROWS + np.arange(CHUNK) + k - CONV_K // 2] = 1.0
    shift = jnp.asarray(shift_np, dtype=BF16)
    full = lambda shape: pl.BlockSpec(shape, lambda b, s: tuple(0 for _ in shape))

    def blk(reverse):
        return lambda b, s: (_chunk_block(b, s, reverse), 0)

    fwd_blk = blk(False)
    y_f, xbc = pl.pallas_call(
        _ssd_fwd_kernel,
        out_shape=(jax.ShapeDtypeStruct((ROWS, D_INNER), F32), jax.ShapeDtypeStruct((ROWS, CONV_DIM), BF16)),
        grid=(BATCH, CHUNKS_SEQ),
        in_specs=[
            pl.BlockSpec((HALO_ROWS, CONV_DIM),
                         lambda b, s: (jnp.maximum(_chunk_block(b, s, False) * halo_per_chunk - 1, 0), 0)),
            pl.BlockSpec((CHUNK, CONV_DIM), fwd_blk),
            pl.BlockSpec((HALO_ROWS, CONV_DIM),
                         lambda b, s: (jnp.minimum((_chunk_block(b, s, False) + 1) * halo_per_chunk, n_halo - 1), 0)),
            pl.BlockSpec((CHUNK, LANES), fwd_blk),
            full(shift.shape),
            full((CONV_K, CONV_DIM)), full((1, CONV_DIM)), full((1, LANES)), full((1, LANES)),
            full((3 * SSM_HEADS, D_INNER)),
        ],
        out_specs=(pl.BlockSpec((CHUNK, D_INNER), fwd_blk), pl.BlockSpec((CHUNK, CONV_DIM), fwd_blk)),
        scratch_shapes=[pltpu.VMEM((SSM_GROUPS, D_STATE, D_INNER // SSM_GROUPS), F32),
                        pltpu.VMEM((CHUNK, CONV_DIM), F32)],
        compiler_params=_params("parallel", "arbitrary"),
        name="ssd_forward",
    )(p_main, p_main, p_main, dt_raw, shift, conv_w, conv_b.reshape(1, CONV_DIM), dtb, a_neg, e3)
    bwd_blk = blk(True)
    return pl.pallas_call(
        _ssd_bwd_kernel,
        out_shape=jax.ShapeDtypeStruct((ROWS, D_INNER), BF16),
        grid=(BATCH, CHUNKS_SEQ),
        in_specs=[
            pl.BlockSpec((CHUNK, CONV_DIM), bwd_blk),
            pl.BlockSpec((CHUNK, LANES), bwd_blk),
            pl.BlockSpec((CHUNK, D_INNER), bwd_blk),
            pl.BlockSpec((CHUNK, D_INNER // 2), lambda b, s: (_chunk_block(b, s, True), zb)),
            pl.BlockSpec((CHUNK, D_INNER // 2), lambda b, s: (_chunk_block(b, s, True), zb + 1)),
            full((1, LANES)), full((1, LANES)), full((3 * SSM_HEADS, D_INNER)),
            full((1, D_INNER)), full((1, D_INNER)),
        ],
        out_specs=pl.BlockSpec((CHUNK, D_INNER), bwd_blk),
        scratch_shapes=[pltpu.VMEM((SSM_GROUPS, D_STATE, D_INNER // SSM_GROUPS), F32)],
        compiler_params=_params("parallel", "arbitrary"),
        name="ssd_backward",
    )(xbc, dt_raw, y_f, p_main, p_main, dtb, a_neg, e3, dsum, ssm_norm.reshape(1, D_INNER))


def _merge_kernel(attl_ref, attc_ref, u_ref, woa_ref, wob_ref, ga_ref, gb_ref, o_ref):
    att = jnp.where(pl.program_id(0) < N_LAT_TILES, attl_ref[...], attc_ref[...])
    o_a = jnp.dot(att, woa_ref[...], preferred_element_type=F32)
    o_b = jnp.dot(u_ref[...], wob_ref[...], preferred_element_type=F32)
    o_ref[...] = (jax.nn.sigmoid(ga_ref[...].astype(F32)) * o_a
                  + jax.nn.sigmoid(gb_ref[...].astype(F32)) * o_b).astype(o_ref.dtype)


def merge_branches(att_lat, att_ctx, u, p_main, w_oa, w_ob, rows):
    tn = 1024
    return pl.pallas_call(
        _merge_kernel,
        out_shape=jax.ShapeDtypeStruct((rows, D_MODEL), BF16),
        grid=(rows // TM, D_MODEL // tn),
        in_specs=[pl.BlockSpec((TM, D_MODEL), lambda i, j: (jnp.minimum(i, N_LAT_TILES - 1), 0)),
                  pl.BlockSpec((TM, D_MODEL), lambda i, j: (0, 0)),
                  pl.BlockSpec((TM, D_INNER), lambda i, j: (i, 0)),
                  pl.BlockSpec((D_MODEL, tn), lambda i, j: (0, j)),
                  pl.BlockSpec((D_INNER, tn), lambda i, j: (0, j)),
                  pl.BlockSpec((TM, tn), lambda i, j: (i, COL_GA // tn + j)),
                  pl.BlockSpec((TM, tn), lambda i, j: (i, COL_GB // tn + j))],
        out_specs=pl.BlockSpec((TM, tn), lambda i, j: (i, j)),
        compiler_params=_params("parallel", "parallel"),
        name="merge_branches",
    )(att_lat, att_ctx, u, w_oa, w_ob, p_main, p_main)


def _route(h, wr_ref):
    lane = lax.broadcasted_iota(jnp.int32, (h.shape[0], LANES), 1)
    lg = jnp.full((h.shape[0], LANES), -jnp.inf, F32)
    for e in range(N_EXPERTS):
        lg = jnp.where(lane == e, jnp.sum(h * wr_ref[e:e + 1, :], axis=-1, keepdims=True), lg)
    v1 = jnp.max(lg, axis=-1, keepdims=True)
    i1 = jnp.min(jnp.where(lg == v1, lane, LANES), axis=-1, keepdims=True)
    lg2 = jnp.where(lane == i1, -jnp.inf, lg)
    v2 = jnp.max(lg2, axis=-1, keepdims=True)
    i2 = jnp.min(jnp.where(lg2 == v2, lane, LANES), axis=-1, keepdims=True)
    e = jnp.exp(v2 - v1)
    g1 = 1.0 / (1.0 + e)
    g2 = e / (1.0 + e)
    return jnp.where(lane == 0, i1.astype(F32),
                     jnp.where(lane == 1, i2.astype(F32),
                               jnp.where(lane == 2, g1, jnp.where(lane == 3, g2, 0.0))))


def _out_res_kernel(m_ref, w_ref, x_ref, gate_ref, g2_ref, sh_ref, sc_ref, *rest, route):
    if route:
        wr_ref, xo_ref, ho_ref, ro_ref = rest
    else:
        xo_ref, ho_ref = rest
    acc = jnp.dot(m_ref[...], w_ref[...], preferred_element_type=F32)
    xn = x_ref[...] + gate_ref[...] * acc
    xo_ref[...] = xn
    h = (_rms(xn) * g2_ref[...]) * (1 + sc_ref[...]) + sh_ref[...]
    ho_ref[...] = h.astype(ho_ref.dtype)
    if route:
        ro_ref[...] = _route(h, wr_ref)


def out_proj_residual(mixed, w_out, x, g2, mods, layer, rows, w_router=None):
    route = w_router is not None
    h_dtype = F32 if route else BF16
    in_specs = [pl.BlockSpec((TM, D_MODEL), lambda i: (i, 0)),
                pl.BlockSpec((D_MODEL, D_MODEL), lambda i: (0, 0)),
                pl.BlockSpec((TM, D_MODEL), lambda i: (i, 0)),
                _mod_spec(layer, 2), pl.BlockSpec((1, D_MODEL), lambda i: (0, 0)),
                _mod_spec(layer, 3), _mod_spec(layer, 4)]
    args = [mixed, w_out, x, mods, g2.reshape(1, D_MODEL), mods, mods]
    out_shape = [jax.ShapeDtypeStruct((rows, D_MODEL), F32), jax.ShapeDtypeStruct((rows, D_MODEL), h_dtype)]
    out_specs = [pl.BlockSpec((TM, D_MODEL), lambda i: (i, 0)), pl.BlockSpec((TM, D_MODEL), lambda i: (i, 0))]
    if route:
        in_specs.append(pl.BlockSpec((N_EXPERTS, D_MODEL), lambda i: (0, 0)))
        args.append(w_router.T)
        out_shape.append(jax.ShapeDtypeStruct((rows, LANES), F32))
        out_specs.append(pl.BlockSpec((TM, LANES), lambda i: (i, 0)))
    return pl.pallas_call(
        functools.partial(_out_res_kernel, route=route),
        out_shape=tuple(out_shape),
        grid=(rows // TM,),
        in_specs=in_specs,
        out_specs=tuple(out_specs),
        compiler_params=_params("parallel"),
        name="out_proj_residual",
    )(*args)


def _ffn_up_kernel(h_ref, w1_ref, w3_ref, o_ref):
    h = h_ref[...]
    a = jnp.dot(h, w1_ref[...], preferred_element_type=F32)
    b = jnp.dot(h, w3_ref[...], preferred_element_type=F32)
    o_ref[...] = (_silu(a) * b).astype(o_ref.dtype)


def ffn_up(h, w1, w3):
    tn = D_FF // 4
    return pl.pallas_call(
        _ffn_up_kernel,
        out_shape=jax.ShapeDtypeStruct((ROWS, D_FF), BF16),
        grid=(ROWS // TM, D_FF // tn),
        in_specs=[pl.BlockSpec((TM, D_MODEL), lambda i, j: (i, 0)),
                  pl.BlockSpec((D_MODEL, tn), lambda i, j: (0, j)),
                  pl.BlockSpec((D_MODEL, tn), lambda i, j: (0, j))],
        out_specs=pl.BlockSpec((TM, tn), lambda i, j: (i, j)),
        compiler_params=_params("parallel", "parallel"),
        name="ffn_up",
    )(h, w1, w3)


def _ffn_down_kernel(g_ref, w_ref, x_ref, gate_ref, o_ref):
    acc = jnp.dot(g_ref[...], w_ref[...], preferred_element_type=F32)
    o_ref[...] = x_ref[...] + gate_ref[...] * acc


def ffn_down_residual(g, w2, x, mods, layer):
    tn = 1024
    gate_spec = pl.BlockSpec((None, None, None, 1, tn), lambda i, j: (layer, _mod_row(i), 5, 0, j))
    return pl.pallas_call(
        _ffn_down_kernel,
        out_shape=jax.ShapeDtypeStruct((ROWS, D_MODEL), F32),
        grid=(ROWS // TM, D_MODEL // tn),
        in_specs=[pl.BlockSpec((TM, D_FF), lambda i, j: (i, 0)),
                  pl.BlockSpec((D_FF, tn), lambda i, j: (0, j)),
                  pl.BlockSpec((TM, tn), lambda i, j: (i, j)),
                  gate_spec],
        out_specs=pl.BlockSpec((TM, tn), lambda i, j: (i, j)),
        compiler_params=_params("parallel", "parallel"),
        name="ffn_down_residual",
    )(g, w2, x, mods)


def _gather_rows_kernel(src_ref, nvalid_ref, h_hbm, o_ref, rows_ref, sem):
    step = pl.program_id(0)
    base = step * GATHER_ROWS

    @pl.when(step == 0)
    def _():
        rows_ref[...] = jnp.zeros_like(rows_ref)

    n_groups = lax.shift_right_logical(nvalid_ref[step] + (DMA_UNROLL - 1), DMA_UNROLL.bit_length() - 1)

    def row_copy(j):
        return pltpu.make_async_copy(h_hbm.at[pl.ds(src_ref[base + j], 1)], rows_ref.at[pl.ds(j, 1)], sem)

    def start(gi, c):
        for u in range(DMA_UNROLL):
            row_copy(gi * DMA_UNROLL + u).start()
        return c

    def wait(gi, c):
        for u in range(DMA_UNROLL):
            row_copy(gi * DMA_UNROLL + u).wait()
        return c

    lax.fori_loop(0, n_groups, start, 0)
    lax.fori_loop(0, n_groups, wait, 0)
    o_ref[...] = rows_ref[...].astype(o_ref.dtype)


def gather_rows(h, src, step_rows):
    n = src.shape[0]
    d = h.shape[1]
    grid_spec = pltpu.PrefetchScalarGridSpec(
        num_scalar_prefetch=2, grid=(n // GATHER_ROWS,),
        in_specs=[pl.BlockSpec(memory_space=pl.ANY)],
        out_specs=pl.BlockSpec((GATHER_ROWS, d), lambda i, s, nv: (i, 0)),
        scratch_shapes=[pltpu.VMEM((GATHER_ROWS, d), h.dtype), pltpu.SemaphoreType.DMA(())])
    return pl.pallas_call(
        _gather_rows_kernel,
        out_shape=jax.ShapeDtypeStruct((n, d), BF16),
        grid_spec=grid_spec,
        compiler_params=_params("arbitrary"),
        name="moe_gather",
    )(src, step_rows, h)


def _moe_kernel(be_ref, nused_ref, nquarter_ref, x_ref, w1_ref, w3_ref, w2_ref, o_ref):
    i = pl.program_id(0)
    f = pl.program_id(1)

    @pl.when(f == 0)
    def _():
        o_ref[...] = jnp.zeros_like(o_ref)

    def mlp(rows):
        x = x_ref[0:rows, :]
        h1 = jnp.dot(x, w1_ref[...].astype(BF16), preferred_element_type=F32)
        h3 = jnp.dot(x, w3_ref[...].astype(BF16), preferred_element_type=F32)
        g = _silu(h1) * h3
        o_ref[0:rows, :] += jnp.dot(g.astype(BF16), w2_ref[...].astype(BF16), preferred_element_type=F32)

    for nq in range(1, MOE_QUARTERS + 1):
        @pl.when(nquarter_ref[i] == nq)
        def _(nq=nq):
            mlp(nq * (MOE_ROW_TILE // MOE_QUARTERS))


def moe_experts(buf, block_expert, n_used, block_quarters, w1, w3, w2):
    r, d = buf.shape
    tm, tf = MOE_ROW_TILE, MOE_FF_TILE
    nf = D_FF_EXPERT // tf

    def f_idx(i, f, nu):
        return jnp.where(i < nu[0], f, nf - 1)

    grid_spec = pltpu.PrefetchScalarGridSpec(
        num_scalar_prefetch=3,
        grid=(r // tm, nf),
        in_specs=[
            pl.BlockSpec((tm, d), lambda i, f, be, nu, nh: (jnp.minimum(i, nu[0] - 1), 0)),
            pl.BlockSpec((None, d, tf), lambda i, f, be, nu, nh: (be[i], 0, f_idx(i, f, nu))),
            pl.BlockSpec((None, d, tf), lambda i, f, be, nu, nh: (be[i], 0, f_idx(i, f, nu))),
            pl.BlockSpec((None, tf, d), lambda i, f, be, nu, nh: (be[i], f_idx(i, f, nu), 0)),
        ],
        out_specs=pl.BlockSpec((tm, d), lambda i, f, be, nu, nh: (i, 0)),
    )
    return pl.pallas_call(
        _moe_kernel,
        out_shape=jax.ShapeDtypeStruct((r, d), F32),
        grid_spec=grid_spec,
        compiler_params=_params("arbitrary", "arbitrary"),
        name="moe_experts",
    )(block_expert, n_used, block_quarters, buf, w1, w3, w2)


def _combine_kernel(dest_ref, y_hbm, x_ref, route_ref, gate_ref, gfin_ref, o_ref, ybuf_ref, sem):
    base = pl.program_id(0) * GATHER_ROWS

    def row_copy(t, k):
        return pltpu.make_async_copy(y_hbm.at[pl.ds(dest_ref[TOP_K * (base + t) + k], 1)],
                                     ybuf_ref.at[k, pl.ds(t, 1)], sem)

    def start(gi, c):
        for u in range(DMA_UNROLL):
            for k in range(TOP_K):
                row_copy(gi * DMA_UNROLL + u, k).start()
        return c

    def wait(gi, c):
        for u in range(DMA_UNROLL):
            for k in range(TOP_K):
                row_copy(gi * DMA_UNROLL + u, k).wait()
        return c

    lax.fori_loop(0, GATHER_ROWS // DMA_UNROLL, start, 0)
    lax.fori_loop(0, GATHER_ROWS // DMA_UNROLL, wait, 0)
    route = route_ref[...]
    f = ybuf_ref[0] * route[:, 2:3] + ybuf_ref[1] * route[:, 3:4]
    xn = x_ref[...] + gate_ref[...] * f
    o_ref[...] = _rms(xn) * gfin_ref[...]


def moe_combine_final(y, dest, x, route, mods, layer, final_norm):
    tiles_per_batch = SEQ // GATHER_ROWS
    grid_spec = pltpu.PrefetchScalarGridSpec(
        num_scalar_prefetch=1, grid=(ROWS_LAT // GATHER_ROWS,),
        in_specs=[pl.BlockSpec(memory_space=pl.ANY),
                  pl.BlockSpec((GATHER_ROWS, D_MODEL), lambda i, d: (i, 0)),
                  pl.BlockSpec((GATHER_ROWS, LANES), lambda i, d: (i, 0)),
                  pl.BlockSpec((None, None, None, 1, D_MODEL), lambda i, d: (layer, i // tiles_per_batch, 5, 0, 0)),
                  pl.BlockSpec((1, D_MODEL), lambda i, d: (0, 0))],
        out_specs=pl.BlockSpec((GATHER_ROWS, D_MODEL), lambda i, d: (i, 0)),
        scratch_shapes=[pltpu.VMEM((TOP_K, GATHER_ROWS, D_MODEL), F32), pltpu.SemaphoreType.DMA(())])
    return pl.pallas_call(
        _combine_kernel,
        out_shape=jax.ShapeDtypeStruct((ROWS_LAT, D_MODEL), F32),
        grid_spec=grid_spec,
        compiler_params=_params("arbitrary"),
        name="moe_combine_final",
    )(dest, y, x, route, mods, final_norm.reshape(1, D_MODEL))


def moe_dispatch(route):
    expert = route[:, :TOP_K].astype(jnp.int32).reshape(-1)
    onehot = (expert[:, None] == jnp.arange(N_EXPERTS)[None, :]).astype(jnp.int32)
    incl = jnp.cumsum(onehot, axis=0)
    rank = jnp.sum((incl - onehot) * onehot, axis=1)
    counts = incl[-1]
    tm = MOE_ROW_TILE
    padded = (counts + tm - 1) // tm * tm
    pad_end = jnp.cumsum(padded)
    dest = ((pad_end - padded)[expert] + rank).astype(jnp.int32)
    n_blocks = MOE_SLOTS // tm
    n_used = (pad_end[-1] // tm).astype(jnp.int32)
    blocks = jnp.arange(n_blocks)
    block_expert = jnp.minimum(jnp.searchsorted(pad_end, blocks * tm, side='right'), N_EXPERTS - 1).astype(jnp.int32)
    block_rows = jnp.clip(counts[block_expert] - (blocks * tm - (pad_end - padded)[block_expert]), 0, tm)
    block_rows = jnp.where(blocks < n_used, block_rows, 0)
    block_expert = jnp.where(blocks < n_used, block_expert, block_expert[jnp.maximum(n_used - 1, 0)])
    quarter = tm // MOE_QUARTERS
    block_quarters = ((block_rows + quarter - 1) // quarter).astype(jnp.int32)
    steps = jnp.arange(MOE_SLOTS // GATHER_ROWS)
    per_block = tm // GATHER_ROWS
    step_rows = jnp.clip(block_rows[steps // per_block] - (steps % per_block) * GATHER_ROWS, 0, GATHER_ROWS)
    token = jnp.repeat(jnp.arange(ROWS_LAT, dtype=jnp.int32), TOP_K)
    src = jnp.zeros((MOE_SLOTS,), jnp.int32).at[dest].set(token)
    return dest, src, step_rows.astype(jnp.int32), block_expert, n_used.reshape(1), block_quarters


def _rope_table(n):
    rows = n // GRID_W
    row = jnp.repeat(jnp.arange(rows, dtype=F32), GRID_W)
    col = jnp.tile(jnp.arange(GRID_W, dtype=F32), rows)
    axis_dim = ROPE_DIM // 2
    inv = ROPE_BASE ** (-jnp.arange(0, axis_dim, 2, dtype=F32) / axis_dim)
    ang_r, ang_c = row[:, None] * inv, col[:, None] * inv
    cr, sr, cc, sc = jnp.cos(ang_r), jnp.sin(ang_r), jnp.cos(ang_c), jnp.sin(ang_c)
    lat = jnp.concatenate([cr, cr, cc, cc, -sr, sr, -sc, sc], axis=1)
    ident = jnp.concatenate([jnp.ones((TM, ROPE_DIM), F32), jnp.zeros((TM, ROPE_DIM), F32)], axis=1)
    return jnp.concatenate([lat, ident], axis=0)


def _layer_weights(w_in, w_uq, w_ukv):
    o = np.cumsum((0,) + PROJ_SIZES)
    seg = lambda i: w_in[:, int(o[i]):int(o[i + 1])]
    w_main = jnp.concatenate([seg(4), seg(3), seg(6), seg(7), seg(0), seg(1)], axis=1).astype(BF16)
    kr_w = seg(2)
    w_small = jnp.concatenate([seg(5), kr_w, kr_w[:, ROPE_SWAP]], axis=1).astype(BF16)
    wq = w_uq.reshape(Q_LORA, N_HEADS, NOPE_DIM + ROPE_DIM)
    rope_w = wq[:, :, NOPE_DIM:]
    w_q = jnp.concatenate([wq[:, :, :NOPE_DIM], rope_w, rope_w[:, :, ROPE_SWAP]], axis=2)
    w_q = w_q.reshape(Q_LORA, N_HEADS * HEAD_Q).astype(BF16)
    wkv = w_ukv.reshape(KV_LORA, N_HEADS, NOPE_DIM + V_DIM)
    w_kv = jnp.concatenate([wkv[:, :, :NOPE_DIM].reshape(KV_LORA, -1), wkv[:, :, NOPE_DIM:].reshape(KV_LORA, -1)],
                           axis=1).astype(BF16)
    return w_main, w_small, w_q, w_kv


def kernel(x, c, ctx, c_ctx, norm_mix, norm_ffn, w_ada, b_ada, w_in, q_norm, w_uq, kv_norm, w_ukv,
           conv_w, conv_b, a_log, dt_bias, d_skip, ssm_norm, w_oa, w_ob, w_out, w1_dense, w3_dense,
           w2_dense, w_router, w1_moe, w3_moe, w2_moe, final_norm):
    tab = _rope_table(SEQ)
    cc = jnp.concatenate([c, c_ctx[None], jnp.zeros((8 - BATCH - 1, D_MODEL), F32)], axis=0)
    mods = adaln(cc, w_ada, b_ada)
    head_of = np.arange(D_INNER) // SSM_HEAD_DIM
    e1 = (np.arange(SSM_HEADS)[:, None] == head_of[None, :]).astype(np.float32)
    e3 = jnp.asarray(np.concatenate([e1, e1, e1], axis=0), dtype=BF16)
    xr = jnp.concatenate([x.reshape(ROWS_LAT, D_MODEL), ctx.reshape(ROWS_CTX, D_MODEL)], axis=0)
    out = None
    for l in range(DEPTH):
        last = l == DEPTH - 1
        rows = ROWS_LAT if last else ROWS
        w_main, w_small, w_q, w_kv = _layer_weights(w_in[l], w_uq[l], w_ukv[l])
        h, p_main = norm_proj(xr, norm_mix[l], mods, l, w_main)
        dt_raw, kr = proj_small(h, w_small, tab)
        q = up_proj(p_main, COL_CQ, q_norm[l], w_q, tab)
        kv = up_proj(p_main, COL_CKV, kv_norm[l], w_kv, None)
        att = latent_attention(q, kv, kr)
        att_ctx = att if last else context_attention(q, kv, kr)
        u = ssd_mixer(p_main, dt_raw, conv_w[l], conv_b[l], dt_bias[l], a_log[l], d_skip[l], ssm_norm[l], e3)
        mixed = merge_branches(att, att_ctx, u, p_main, w_oa[l].astype(BF16), w_ob[l].astype(BF16), rows)
        if l % 2 == 0:
            xr, h2 = out_proj_residual(mixed, w_out[l].astype(BF16), xr, norm_ffn[l], mods, l, rows)
            g = ffn_up(h2, w1_dense[l // 2].astype(BF16), w3_dense[l // 2].astype(BF16))
            xr = ffn_down_residual(g, w2_dense[l // 2].astype(BF16), xr, mods, l)
        else:
            wr = w_router[l // 2]
            x_lat, h2, route = out_proj_residual(mixed, w_out[l].astype(BF16), xr, norm_ffn[l], mods, l, rows, wr)
            dest, src, step_rows, block_expert, n_used, block_quarters = moe_dispatch(route)
            buf = gather_rows(h2, src, step_rows)
            y = moe_experts(buf, block_expert, n_used, block_quarters, w1_moe[l // 2], w3_moe[l // 2], w2_moe[l // 2])
            out = moe_combine_final(y, dest, x_lat, route, mods, l, final_norm)
    return out.reshape(BATCH, SEQ, D_MODEL)
```

```python
import functools

import numpy as np
import jax
import jax.numpy as jnp
from jax import lax
from jax.experimental import pallas as pl
from jax.experimental.pallas import tpu as pltpu

D_MODEL = 2048
BATCH = 2
SEQ = 4096
DEPTH = 2
CTX_LEN = 256
GRID_W = 64
EPS = 1e-6

N_HEADS = D_MODEL // 128
Q_LORA = 512
KV_LORA = 512
NOPE_DIM = 128
ROPE_DIM = 64
V_DIM = 128
ROPE_BASE = 10000.0
SM_SCALE = (NOPE_DIM + ROPE_DIM) ** -0.5

D_INNER = 2 * D_MODEL
SSM_HEAD_DIM = 64
SSM_HEADS = D_INNER // SSM_HEAD_DIM
SSM_GROUPS = 8
D_STATE = 128
CONV_K = 5
CONV_DIM = D_INNER + 2 * SSM_GROUPS * D_STATE
CHUNK = 128

PROJ_SIZES = (Q_LORA, KV_LORA, ROPE_DIM, D_INNER, CONV_DIM, 2 * SSM_HEADS, D_MODEL, D_MODEL)

D_FF = 256 * ((8 * D_MODEL // 3 + 255) // 256)
N_EXPERTS = 8
TOP_K = 2
D_FF_EXPERT = 7 * D_MODEL // 2

F32 = jnp.float32
BF16 = jnp.bfloat16
HIGHEST = lax.Precision.HIGHEST

VMEM_LIMIT_BYTES = 56 * 1024 * 1024
LANES = 128
HALO_ROWS = 16
OFF_CENTRE_TAPS = (0, 1, 3, 4)
CONV_COL_TILE = 2048
LOG2E = 1.4426950408889634
QK_SCALE = SM_SCALE * LOG2E

ROWS_LAT = BATCH * SEQ
ROWS_CTX = BATCH * CTX_LEN
ROWS = ROWS_LAT + ROWS_CTX
TM = 512
PROJ_COL_TILE = 1536
TILES_PER_BATCH = SEQ // TM
N_LAT_TILES = ROWS_LAT // TM
CHUNKS_LAT = SEQ // CHUNK
CHUNKS_CTX = CTX_LEN // CHUNK
CHUNKS_SEQ = CHUNKS_LAT + CHUNKS_CTX

COL_XBC = 0
COL_Z = COL_XBC + CONV_DIM
COL_GA = COL_Z + D_INNER
COL_GB = COL_GA + D_MODEL
COL_CQ = COL_GB + D_MODEL
COL_CKV = COL_CQ + Q_LORA
N_MAIN = COL_CKV + KV_LORA
HEAD_Q = 2 * LANES

MOE_ROW_TILE = 1024
MOE_FF_TILE = 256
MOE_QUARTERS = 4
MOE_SLOTS = ROWS_LAT * TOP_K + N_EXPERTS * MOE_ROW_TILE
GATHER_ROWS = 256
DMA_UNROLL = 8
ATTN_Q_TILE = 2048
ATTN_KV_CHUNK = 256
ROPE_SWAP = np.concatenate([np.arange(16, 32), np.arange(0, 16), np.arange(48, 64), np.arange(32, 48)])


def _params(*sem):
    return pltpu.CompilerParams(dimension_semantics=sem, vmem_limit_bytes=VMEM_LIMIT_BYTES)


def _mod_row(i):
    return jnp.minimum(i // TILES_PER_BATCH, BATCH)


def _rope_row(i):
    return jnp.where(i < N_LAT_TILES, i % TILES_PER_BATCH, TILES_PER_BATCH)


def _mod_spec(layer, k):
    return pl.BlockSpec((None, None, None, 1, D_MODEL), lambda i, *_: (layer, _mod_row(i), k, 0, 0))


def _rms(x):
    return x * lax.rsqrt(jnp.mean(x * x, axis=-1, keepdims=True) + EPS)


def _silu(x):
    return x * jax.nn.sigmoid(x)


def _rope(x, tab):
    y = x * tab
    y = y + pltpu.roll(y, ROPE_DIM, axis=1)
    lane = lax.broadcasted_iota(jnp.int32, y.shape, 1)
    return jnp.where(lane < ROPE_DIM, y, 0.0)


def _adaln_kernel(c_ref, wa_ref, wb_ref, b_ref, o_ref):
    @pl.when(pl.program_id(1) == 0)
    def _():
        o_ref[...] = jnp.broadcast_to(b_ref[...], o_ref.shape)

    a = _silu(c_ref[...]).astype(BF16)
    half = wa_ref.shape[1]
    o_ref[:, :half] += jnp.dot(a, wa_ref[...].astype(BF16), preferred_element_type=F32)
    o_ref[:, half:] += jnp.dot(a, wb_ref[...].astype(BF16), preferred_element_type=F32)


def adaln(cc, w_ada, b_ada):
    tk = 256
    n = 6 * D_MODEL
    out = pl.pallas_call(
        _adaln_kernel,
        out_shape=jax.ShapeDtypeStruct((DEPTH, 8, n), F32),
        grid=(DEPTH, D_MODEL // tk),
        in_specs=[pl.BlockSpec((8, tk), lambda l, k: (0, k)),
                  pl.BlockSpec((None, tk, n // 2), lambda l, k: (l, k, 0)),
                  pl.BlockSpec((None, tk, n // 2), lambda l, k: (l, k, 1)),
                  pl.BlockSpec((None, 1, n), lambda l, k: (l, 0, 0))],
        out_specs=pl.BlockSpec((None, 8, n), lambda l, k: (l, 0, 0)),
        compiler_params=_params("parallel", "arbitrary"),
        name="adaln",
    )(cc, w_ada, w_ada, b_ada.reshape(DEPTH, 1, n))
    return out.reshape(DEPTH, 8, 6, 1, D_MODEL)


def _norm_proj_kernel(x_ref, g_ref, sh_ref, sc_ref, w_ref, ws_ref, tab_ref, o_ref, dt_ref, kr_ref, h_ref):
    @pl.when(pl.program_id(1) == 0)
    def _():
        y = _rms(x_ref[...]) * g_ref[...]
        h = (y * (1 + sc_ref[...]) + sh_ref[...]).astype(h_ref.dtype)
        h_ref[...] = h
        acc = jnp.dot(h, ws_ref[...], preferred_element_type=F32)
        dt_ref[...] = acc[:, :LANES]
        kr_ref[...] = _rope(acc[:, LANES:], tab_ref[...]).astype(kr_ref.dtype)

    o_ref[...] = jnp.dot(h_ref[...], w_ref[...], preferred_element_type=F32).astype(o_ref.dtype)


def norm_proj(x, g, mods, layer, w, w_small, tab):
    n = w.shape[1]
    tn = PROJ_COL_TILE
    return pl.pallas_call(
        _norm_proj_kernel,
        out_shape=(jax.ShapeDtypeStruct((ROWS, n), BF16),
                   jax.ShapeDtypeStruct((ROWS, LANES), F32), jax.ShapeDtypeStruct((ROWS, LANES), BF16)),
        grid=(ROWS // TM, n // tn),
        in_specs=[pl.BlockSpec((TM, D_MODEL), lambda i, j: (i, 0)),
                  pl.BlockSpec((1, D_MODEL), lambda i, j: (0, 0)),
                  _mod_spec(layer, 0), _mod_spec(layer, 1),
                  pl.BlockSpec((D_MODEL, tn), lambda i, j: (0, j)),
                  pl.BlockSpec((D_MODEL, 2 * LANES), lambda i, j: (0, 0)),
                  pl.BlockSpec((TM, LANES), lambda i, j: (_rope_row(i), 0))],
        out_specs=(pl.BlockSpec((TM, tn), lambda i, j: (i, j)),
                   pl.BlockSpec((TM, LANES), lambda i, j: (i, 0)),
                   pl.BlockSpec((TM, LANES), lambda i, j: (i, 0))),
        scratch_shapes=[pltpu.VMEM((TM, D_MODEL), BF16)],
        compiler_params=_params("parallel", "arbitrary"),
        name="norm_proj",
    )(x, g.reshape(1, D_MODEL), mods, mods, w, w_small, tab)


def _up_kernel(c_ref, g_ref, w_ref, *rest, rope, heads_per_tile):
    if rope:
        tab_ref, o_ref = rest
    else:
        (o_ref,) = rest
    c = _rms(c_ref[...].astype(F32)) * g_ref[...]
    acc = jnp.dot(c.astype(BF16), w_ref[...], preferred_element_type=F32)
    if not rope:
        o_ref[...] = acc.astype(o_ref.dtype)
        return
    tab = tab_ref[...]
    acc = acc * QK_SCALE
    for hh in range(heads_per_tile):
        base = hh * HEAD_Q
        o_ref[:, base:base + LANES] = acc[:, base:base + LANES].astype(o_ref.dtype)
        o_ref[:, base + LANES:base + HEAD_Q] = _rope(acc[:, base + LANES:base + HEAD_Q], tab).astype(o_ref.dtype)


def up_proj(p_main, col, g, w, tab):
    lora, n = w.shape
    tn = n
    rope = tab is not None
    in_specs = [pl.BlockSpec((TM, lora), lambda i, j: (i, col // lora)),
                pl.BlockSpec((1, lora), lambda i, j: (0, 0)),
                pl.BlockSpec((lora, tn), lambda i, j: (0, j))]
    args = [p_main, g.reshape(1, lora), w]
    if rope:
        in_specs.append(pl.BlockSpec((TM, LANES), lambda i, j: (_rope_row(i), 0)))
        args.append(tab)
    return pl.pallas_call(
        functools.partial(_up_kernel, rope=rope, heads_per_tile=tn // HEAD_Q),
        out_shape=jax.ShapeDtypeStruct((ROWS, n), BF16),
        grid=(ROWS // TM, n // tn),
        in_specs=in_specs,
        out_specs=pl.BlockSpec((TM, tn), lambda i, j: (i, j)),
        compiler_params=_params("parallel", "parallel"),
        name="up_proj_rope" if rope else "up_proj",
    )(*args)


def _softmax_chunks(q, kcat_ref, v_chunks):
    m = l = acc = None
    for start, size, v in v_chunks:
        k = kcat_ref[start:start + size, :]
        s = lax.dot_general(q, k, (((1,), (1,)), ((), ())), preferred_element_type=F32)
        m_cur = jnp.max(s, axis=-1, keepdims=True)
        if m is None:
            m_new = m_cur
            p = jnp.exp2(s - m_new)
            l = jnp.sum(p, axis=-1, keepdims=True)
            acc = jnp.dot(p.astype(BF16), v, preferred_element_type=F32)
        else:
            m_new = jnp.maximum(m, m_cur)
            alpha = jnp.exp2(m - m_new)
            p = jnp.exp2(s - m_new)
            l = alpha * l + jnp.sum(p, axis=-1, keepdims=True)
            acc = alpha * acc + jnp.dot(p.astype(BF16), v, preferred_element_type=F32)
        m = m_new
    return acc / l


def _lat_attn_kernel(q_ref, knl_ref, krl_ref, vl_ref, knc_ref, krc_ref, vc_ref, o_ref, kcat_ref):
    @pl.when(pl.program_id(2) == 0)
    def _():
        kcat_ref[0:CTX_LEN, 0:LANES] = knc_ref[...]
        kcat_ref[0:CTX_LEN, LANES:HEAD_Q] = krc_ref[...]
        kcat_ref[CTX_LEN:, 0:LANES] = knl_ref[...]
        kcat_ref[CTX_LEN:, LANES:HEAD_Q] = krl_ref[...]

    chunks = [(0, CTX_LEN, vc_ref[...])]
    for s in range(0, SEQ, ATTN_KV_CHUNK):
        chunks.append((CTX_LEN + s, ATTN_KV_CHUNK, vl_ref[s:s + ATTN_KV_CHUNK, :]))
    o_ref[...] = _softmax_chunks(q_ref[...], kcat_ref, chunks).astype(o_ref.dtype)


def latent_attention(q, kv, kr):
    tq = ATTN_Q_TILE
    qt = SEQ // tq
    ctx_blk = ROWS_LAT // CTX_LEN
    return pl.pallas_call(
        _lat_attn_kernel,
        out_shape=jax.ShapeDtypeStruct((ROWS_LAT, D_MODEL), BF16),
        grid=(BATCH, N_HEADS, qt),
        in_specs=[pl.BlockSpec((tq, HEAD_Q), lambda b, h, i: (b * qt + i, h)),
                  pl.BlockSpec((SEQ, LANES), lambda b, h, i: (b, h)),
                  pl.BlockSpec((SEQ, LANES), lambda b, h, i: (b, 0)),
                  pl.BlockSpec((SEQ, LANES), lambda b, h, i: (b, N_HEADS + h)),
                  pl.BlockSpec((CTX_LEN, LANES), lambda b, h, i: (ctx_blk + b, h)),
                  pl.BlockSpec((CTX_LEN, LANES), lambda b, h, i: (ctx_blk + b, 0)),
                  pl.BlockSpec((CTX_LEN, LANES), lambda b, h, i: (ctx_blk + b, N_HEADS + h))],
        out_specs=pl.BlockSpec((tq, LANES), lambda b, h, i: (b * qt + i, h)),
        scratch_shapes=[pltpu.VMEM((CTX_LEN + SEQ, HEAD_Q), BF16)],
        compiler_params=_params("parallel", "parallel", "arbitrary"),
        name="latent_attention",
    )(q, kv, kr, kv, kv, kr, kv)


def _ctx_attn_kernel(q_ref, kn_ref, kr_ref, v_ref, o_ref, kcat_ref):
    kcat_ref[:, 0:LANES] = kn_ref[...]
    kcat_ref[:, LANES:HEAD_Q] = kr_ref[...]
    o_ref[...] = _softmax_chunks(q_ref[...], kcat_ref, [(0, CTX_LEN, v_ref[...])]).astype(o_ref.dtype)


def context_attention(q, kv, kr):
    ctx_blk = ROWS_LAT // CTX_LEN
    return pl.pallas_call(
        _ctx_attn_kernel,
        out_shape=jax.ShapeDtypeStruct((ROWS_CTX, D_MODEL), BF16),
        grid=(BATCH, N_HEADS),
        in_specs=[pl.BlockSpec((CTX_LEN, HEAD_Q), lambda b, h: (ctx_blk + b, h)),
                  pl.BlockSpec((CTX_LEN, LANES), lambda b, h: (ctx_blk + b, h)),
                  pl.BlockSpec((CTX_LEN, LANES), lambda b, h: (ctx_blk + b, 0)),
                  pl.BlockSpec((CTX_LEN, LANES), lambda b, h: (ctx_blk + b, N_HEADS + h))],
        out_specs=pl.BlockSpec((CTX_LEN, LANES), lambda b, h: (b, h)),
        scratch_shapes=[pltpu.VMEM((CTX_LEN, HEAD_Q), BF16)],
        compiler_params=_params("parallel", "parallel"),
        name="context_attention",
    )(q, kv, kr, kv)


def _chunk_block(b, s, reverse):
    if reverse:
        return jnp.where(s < CHUNKS_CTX, ROWS_LAT // CHUNK + CHUNKS_CTX * b + (CHUNKS_CTX - 1 - s),
                         CHUNKS_LAT * b + (CHUNKS_SEQ - 1 - s))
    return jnp.where(s < CHUNKS_CTX, ROWS_LAT // CHUNK + CHUNKS_CTX * b + s,
                     CHUNKS_LAT * b + (s - CHUNKS_CTX))


def _expand_heads(v, e3_ref):
    hi = v.astype(BF16)
    r1 = v - hi.astype(F32)
    mid = r1.astype(BF16)
    lo = (r1 - mid.astype(F32)).astype(BF16)
    return jnp.dot(jnp.concatenate([hi, mid, lo], axis=1), e3_ref[...], preferred_element_type=F32)


def _ssd_chunk(xcol, dt_raw, dtb_ref, a_ref, e3_ref, state_ref, reverse):
    off = SSM_HEADS if reverse else 0
    heads = slice(off, off + SSM_HEADS)
    row = lax.broadcasted_iota(jnp.int32, (CHUNK, CHUNK), 0)
    colm = lax.broadcasted_iota(jnp.int32, (CHUNK, CHUNK), 1)
    keep = (colm >= row) if reverse else (colm <= row)
    tri = keep.astype(F32)
    dt = jax.nn.softplus(dt_raw + dtb_ref[...])
    a = dt * a_ref[...]
    cum = jnp.dot(tri, a, precision=HIGHEST, preferred_element_type=F32)
    cum2 = cum * LOG2E
    cum2_t = cum2.T
    last_row = 0 if reverse else CHUNK - 1
    dec_h = jnp.exp2(cum2)
    end_h = dt * jnp.exp2(cum2[last_row:last_row + 1, :] - cum2)
    dt_full = _expand_heads(dt[:, heads], e3_ref)
    end_full = _expand_heads(end_h[:, heads], e3_ref)
    dec_in = _expand_heads(dec_h[:, heads], e3_ref)
    chunk_dec = dec_in[last_row:last_row + 1, :]
    xs = xcol(0, D_INNER)
    xcb = (xs * dt_full).astype(BF16)
    xce = (xs * end_full).astype(BF16)
    gw = SSM_HEADS // SSM_GROUPS * SSM_HEAD_DIM
    lane = lax.broadcasted_iota(jnp.int32, (CHUNK, LANES), 1)
    ys = []
    for g in range(SSM_GROUPS):
        b32 = xcol(D_INNER + g * D_STATE, D_INNER + (g + 1) * D_STATE)
        bg = b32.astype(BF16)
        cg = xcol(D_INNER + (SSM_GROUPS + g) * D_STATE, D_INNER + (SSM_GROUPS + g + 1) * D_STATE).astype(BF16)
        cb = lax.dot_general(cg, bg, (((1,), (1,)), ((), ())), preferred_element_type=F32)
        st = state_ref[g]
        y_off = jnp.dot(cg, st.astype(BF16), preferred_element_type=F32) * dec_in[:, g * gw:(g + 1) * gw]
        upd = jnp.dot(b32.T.astype(BF16), xce[:, g * gw:(g + 1) * gw],
                      preferred_element_type=F32)
        state_ref[g] = st * chunk_dec[:, g * gw:(g + 1) * gw] + upd
        pairs = []
        for j in range(gw // LANES):
            h0 = off + g * (SSM_HEADS // SSM_GROUPS) + 2 * j
            ms = []
            for hh in (h0, h0 + 1):
                seg = jnp.where(keep, cum2[:, hh:hh + 1] - cum2_t[hh:hh + 1, :], -jnp.inf)
                ms.append((jnp.exp2(seg) * cb).astype(BF16))
            xp = xcb[:, g * gw + j * LANES:g * gw + (j + 1) * LANES]
            rhs = jnp.concatenate([jnp.where(lane < SSM_HEAD_DIM, xp, 0), jnp.where(lane >= SSM_HEAD_DIM, xp, 0)],
                                  axis=0)
            pairs.append(jnp.dot(jnp.concatenate(ms, axis=1), rhs, preferred_element_type=F32))
        ys.append(jnp.concatenate(pairs, axis=1) + y_off)
    return jnp.concatenate(ys, axis=1)


def _ssd_fwd_kernel(prev_ref, cur_ref, next_ref, dt_ref, shift_ref, cw_ref, cbias_ref, dtb_ref, a_ref, e3_ref,
                    y_ref, xbc_ref, state_ref, xs_ref):
    s = pl.program_id(1)

    @pl.when(s == 0)
    def _():
        state_ref[...] = jnp.zeros_like(state_ref)

    first = (s == 0) | (s == CHUNKS_CTX)
    final = (s == CHUNKS_CTX - 1) | (s == CHUNKS_SEQ - 1)
    halo_zeros = jnp.zeros(prev_ref.shape, prev_ref.dtype)
    prev = jnp.where(first, halo_zeros, prev_ref[...])
    nxt = jnp.where(final, halo_zeros, next_ref[...])
    ext = jnp.concatenate([prev, cur_ref[...], nxt], axis=0)
    for c0 in range(0, CONV_DIM, CONV_COL_TILE):
        cs = slice(c0, c0 + CONV_COL_TILE)
        shifted = jnp.dot(shift_ref[...], ext[:, cs], preferred_element_type=F32)
        acc = cbias_ref[:, cs] + cw_ref[CONV_K // 2:CONV_K // 2 + 1, cs] * cur_ref[:, cs].astype(F32)
        for j, k in enumerate(OFF_CENTRE_TAPS):
            acc = acc + cw_ref[k:k + 1, cs] * shifted[j * CHUNK:(j + 1) * CHUNK, :]
        xs = _silu(acc)
        xs_ref[:, cs] = xs
        xbc_ref[:, cs] = xs.astype(xbc_ref.dtype)
    y_ref[...] = _ssd_chunk(lambda a, b: xs_ref[:, a:b], dt_ref[...], dtb_ref, a_ref, e3_ref, state_ref,
                            reverse=False)


def _ssd_bwd_kernel(xbc_ref, dt_ref, yf_ref, z0_ref, z1_ref, dtb_ref, a_ref, e3_ref, dsum_ref, nw_ref,
                    u_ref, state_ref):
    @pl.when(pl.program_id(1) == 0)
    def _():
        state_ref[...] = jnp.zeros_like(state_ref)

    xcol = lambda a, b: xbc_ref[:, a:b].astype(F32)
    y = _ssd_chunk(xcol, dt_ref[...], dtb_ref, a_ref, e3_ref, state_ref, reverse=True)
    y = y + yf_ref[...] + dsum_ref[...] * xcol(0, D_INNER)
    z = jnp.concatenate([z0_ref[...], z1_ref[...]], axis=1).astype(F32)
    u = y * _silu(z)
    gw = D_INNER // SSM_GROUPS
    for g in range(SSM_GROUPS):
        ug = u[:, g * gw:(g + 1) * gw]
        u_ref[:, g * gw:(g + 1) * gw] = (_rms(ug) * nw_ref[:, g * gw:(g + 1) * gw]).astype(u_ref.dtype)


def ssd_mixer(p_main, dt_raw, conv_w, conv_b, dt_bias, a_log, d_skip, ssm_norm, e3):
    zb = COL_Z // (D_INNER // 2)
    dtb = dt_bias.reshape(1, 2 * SSM_HEADS).astype(F32)
    a_neg = (-jnp.exp(a_log.astype(F32))).reshape(1, 2 * SSM_HEADS)
    dsum = jnp.repeat(d_skip[0] + d_skip[1], SSM_HEAD_DIM).reshape(1, D_INNER)
    halo_per_chunk = CHUNK // HALO_ROWS
    n_halo = ROWS // HALO_ROWS
    shift_np = np.zeros((len(OFF_CENTRE_TAPS) * CHUNK, CHUNK + 2 * HALO_ROWS), np.float32)
    for j, k in enumerate(OFF_CENTRE_TAPS):
        shift_np[j * CHUNK + np.arange(CHUNK), HALO_ROWS + np.arange(CHUNK) + k - CONV_K // 2] = 1.0
    shift = jnp.asarray(shift_np, dtype=BF16)
    full = lambda shape: pl.BlockSpec(shape, lambda b, s: tuple(0 for _ in shape))

    def blk(reverse):
        return lambda b, s: (_chunk_block(b, s, reverse), 0)

    fwd_blk = blk(False)
    y_f, xbc = pl.pallas_call(
        _ssd_fwd_kernel,
        out_shape=(jax.ShapeDtypeStruct((ROWS, D_INNER), F32), jax.ShapeDtypeStruct((ROWS, CONV_DIM), BF16)),
        grid=(BATCH, CHUNKS_SEQ),
        in_specs=[
            pl.BlockSpec((HALO_ROWS, CONV_DIM),
                         lambda b, s: (jnp.maximum(_chunk_block(b, s, False) * halo_per_chunk - 1, 0), 0)),
            pl.BlockSpec((CHUNK, CONV_DIM), fwd_blk),
            pl.BlockSpec((HALO_ROWS, CONV_DIM),
                         lambda b, s: (jnp.minimum((_chunk_block(b, s, False) + 1) * halo_per_chunk, n_halo - 1), 0)),
            pl.BlockSpec((CHUNK, LANES), fwd_blk),
            full(shift.shape),
            full((CONV_K, CONV_DIM)), full((1, CONV_DIM)), full((1, LANES)), full((1, LANES)),
            full((3 * SSM_HEADS, D_INNER)),
        ],
        out_specs=(pl.BlockSpec((CHUNK, D_INNER), fwd_blk), pl.BlockSpec((CHUNK, CONV_DIM), fwd_blk)),
        scratch_shapes=[pltpu.VMEM((SSM_GROUPS, D_STATE, D_INNER // SSM_GROUPS), F32),
                        pltpu.VMEM((CHUNK, CONV_DIM), F32)],
        compiler_params=_params("parallel", "arbitrary"),
        name="ssd_forward",
    )(p_main, p_main, p_main, dt_raw, shift, conv_w, conv_b.reshape(1, CONV_DIM), dtb, a_neg, e3)
    bwd_blk = blk(True)
    return pl.pallas_call(
        _ssd_bwd_kernel,
        out_shape=jax.ShapeDtypeStruct((ROWS, D_INNER), BF16),
        grid=(BATCH, CHUNKS_SEQ),
        in_specs=[
            pl.BlockSpec((CHUNK, CONV_DIM), bwd_blk),
            pl.BlockSpec((CHUNK, LANES), bwd_blk),
            pl.BlockSpec((CHUNK, D_INNER), bwd_blk),
            pl.BlockSpec((CHUNK, D_INNER // 2), lambda b, s: (_chunk_block(b, s, True), zb)),
            pl.BlockSpec((CHUNK, D_INNER // 2), lambda b, s: (_chunk_block(b, s, True), zb + 1)),
            full((1, LANES)), full((1, LANES)), full((3 * SSM_HEADS, D_INNER)),
            full((1, D_INNER)), full((1, D_INNER)),
        ],
        out_specs=pl.BlockSpec((CHUNK, D_INNER), bwd_blk),
        scratch_shapes=[pltpu.VMEM((SSM_GROUPS, D_STATE, D_INNER // SSM_GROUPS), F32)],
        compiler_params=_params("parallel", "arbitrary"),
        name="ssd_backward",
    )(xbc, dt_raw, y_f, p_main, p_main, dtb, a_neg, e3, dsum, ssm_norm.reshape(1, D_INNER))


def _merge_kernel(attl_ref, attc_ref, u_ref, woa_ref, wob_ref, ga_ref, gb_ref, o_ref):
    att = jnp.where(pl.program_id(0) < N_LAT_TILES, attl_ref[...], attc_ref[...])
    o_a = jnp.dot(att, woa_ref[...], preferred_element_type=F32)
    o_b = jnp.dot(u_ref[...], wob_ref[...], preferred_element_type=F32)
    o_ref[...] = (jax.nn.sigmoid(ga_ref[...].astype(F32)) * o_a
                  + jax.nn.sigmoid(gb_ref[...].astype(F32)) * o_b).astype(o_ref.dtype)


def merge_branches(att_lat, att_ctx, u, p_main, w_oa, w_ob, rows):
    tn = 1024
    return pl.pallas_call(
        _merge_kernel,
        out_shape=jax.ShapeDtypeStruct((rows, D_MODEL), BF16),
        grid=(rows // TM, D_MODEL // tn),
        in_specs=[pl.BlockSpec((TM, D_MODEL), lambda i, j: (jnp.minimum(i, N_LAT_TILES - 1), 0)),
                  pl.BlockSpec((TM, D_MODEL), lambda i, j: (0, 0)),
                  pl.BlockSpec((TM, D_INNER), lambda i, j: (i, 0)),
                  pl.BlockSpec((D_MODEL, tn), lambda i, j: (0, j)),
                  pl.BlockSpec((D_INNER, tn), lambda i, j: (0, j)),
                  pl.BlockSpec((TM, tn), lambda i, j: (i, COL_GA // tn + j)),
                  pl.BlockSpec((TM, tn), lambda i, j: (i, COL_GB // tn + j))],
        out_specs=pl.BlockSpec((TM, tn), lambda i, j: (i, j)),
        compiler_params=_params("parallel", "parallel"),
        name="merge_branches",
    )(att_lat, att_ctx, u, w_oa, w_ob, p_main, p_main)


def _route(h, wr_ref):
    lane = lax.broadcasted_iota(jnp.int32, (h.shape[0], LANES), 1)
    lg = jnp.full((h.shape[0], LANES), -jnp.inf, F32)
    for e in range(N_EXPERTS):
        lg = jnp.where(lane == e, jnp.sum(h * wr_ref[e:e + 1, :], axis=-1, keepdims=True), lg)
    v1 = jnp.max(lg, axis=-1, keepdims=True)
    i1 = jnp.min(jnp.where(lg == v1, lane, LANES), axis=-1, keepdims=True)
    lg2 = jnp.where(lane == i1, -jnp.inf, lg)
    v2 = jnp.max(lg2, axis=-1, keepdims=True)
    i2 = jnp.min(jnp.where(lg2 == v2, lane, LANES), axis=-1, keepdims=True)
    e = jnp.exp(v2 - v1)
    g1 = 1.0 / (1.0 + e)
    g2 = e / (1.0 + e)
    return jnp.where(lane == 0, i1.astype(F32),
                     jnp.where(lane == 1, i2.astype(F32),
                               jnp.where(lane == 2, g1, jnp.where(lane == 3, g2, 0.0))))


def _out_res_kernel(m_ref, w_ref, x_ref, gate_ref, g2_ref, sh_ref, sc_ref, *rest, route):
    if route:
        wr_ref, xo_ref, ho_ref, ro_ref = rest
    else:
        xo_ref, ho_ref = rest
    acc = jnp.dot(m_ref[...], w_ref[...], preferred_element_type=F32)
    xn = x_ref[...] + gate_ref[...] * acc
    xo_ref[...] = xn
    h = (_rms(xn) * g2_ref[...]) * (1 + sc_ref[...]) + sh_ref[...]
    ho_ref[...] = h.astype(ho_ref.dtype)
    if route:
        ro_ref[...] = _route(h, wr_ref)


def out_proj_residual(mixed, w_out, x, g2, mods, layer, rows, w_router=None):
    route = w_router is not None
    h_dtype = F32 if route else BF16
    in_specs = [pl.BlockSpec((TM, D_MODEL), lambda i: (i, 0)),
                pl.BlockSpec((D_MODEL, D_MODEL), lambda i: (0, 0)),
                pl.BlockSpec((TM, D_MODEL), lambda i: (i, 0)),
                _mod_spec(layer, 2), pl.BlockSpec((1, D_MODEL), lambda i: (0, 0)),
                _mod_spec(layer, 3), _mod_spec(layer, 4)]
    args = [mixed, w_out, x, mods, g2.reshape(1, D_MODEL), mods, mods]
    out_shape = [jax.ShapeDtypeStruct((rows, D_MODEL), F32), jax.ShapeDtypeStruct((rows, D_MODEL), h_dtype)]
    out_specs = [pl.BlockSpec((TM, D_MODEL), lambda i: (i, 0)), pl.BlockSpec((TM, D_MODEL), lambda i: (i, 0))]
    if route:
        in_specs.append(pl.BlockSpec((N_EXPERTS, D_MODEL), lambda i: (0, 0)))
        args.append(w_router.T)
        out_shape.append(jax.ShapeDtypeStruct((rows, LANES), F32))
        out_specs.append(pl.BlockSpec((TM, LANES), lambda i: (i, 0)))
    return pl.pallas_call(
        functools.partial(_out_res_kernel, route=route),
        out_shape=tuple(out_shape),
        grid=(rows // TM,),
        in_specs=in_specs,
        out_specs=tuple(out_specs),
        compiler_params=_params("parallel"),
        name="out_proj_residual",
    )(*args)


def _ffn_up_kernel(h_ref, w1_ref, w3_ref, o_ref):
    h = h_ref[...]
    a = jnp.dot(h, w1_ref[...], preferred_element_type=F32)
    b = jnp.dot(h, w3_ref[...], preferred_element_type=F32)
    o_ref[...] = (_silu(a) * b).astype(o_ref.dtype)


def ffn_up(h, w1, w3):
    tn = D_FF // 4
    return pl.pallas_call(
        _ffn_up_kernel,
        out_shape=jax.ShapeDtypeStruct((ROWS, D_FF), BF16),
        grid=(ROWS // TM, D_FF // tn),
        in_specs=[pl.BlockSpec((TM, D_MODEL), lambda i, j: (i, 0)),
                  pl.BlockSpec((D_MODEL, tn), lambda i, j: (0, j)),
                  pl.BlockSpec((D_MODEL, tn), lambda i, j: (0, j))],
        out_specs=pl.BlockSpec((TM, tn), lambda i, j: (i, j)),
        compiler_params=_params("parallel", "parallel"),
        name="ffn_up",
    )(h, w1, w3)


def _ffn_down_kernel(g_ref, w_ref, x_ref, gate_ref, o_ref):
    acc = jnp.dot(g_ref[...], w_ref[...], preferred_element_type=F32)
    o_ref[...] = x_ref[...] + gate_ref[...] * acc


def ffn_down_residual(g, w2, x, mods, layer):
    tn = 1024
    gate_spec = pl.BlockSpec((None, None, None, 1, tn), lambda i, j: (layer, _mod_row(i), 5, 0, j))
    return pl.pallas_call(
        _ffn_down_kernel,
        out_shape=jax.ShapeDtypeStruct((ROWS, D_MODEL), F32),
        grid=(ROWS // TM, D_MODEL // tn),
        in_specs=[pl.BlockSpec((TM, D_FF), lambda i, j: (i, 0)),
                  pl.BlockSpec((D_FF, tn), lambda i, j: (0, j)),
                  pl.BlockSpec((TM, tn), lambda i, j: (i, j)),
                  gate_spec],
        out_specs=pl.BlockSpec((TM, tn), lambda i, j: (i, j)),
        compiler_params=_params("parallel", "parallel"),
        name="ffn_down_residual",
    )(g, w2, x, mods)


def _gather_rows_kernel(src_ref, nvalid_ref, h_hbm, o_ref, rows_ref, sem):
    step = pl.program_id(0)
    base = step * GATHER_ROWS

    @pl.when(step == 0)
    def _():
        rows_ref[...] = jnp.zeros_like(rows_ref)

    n_groups = lax.shift_right_logical(nvalid_ref[step] + (DMA_UNROLL - 1), DMA_UNROLL.bit_length() - 1)

    def row_copy(j):
        return pltpu.make_async_copy(h_hbm.at[pl.ds(src_ref[base + j], 1)], rows_ref.at[pl.ds(j, 1)], sem)

    def start(gi, c):
        for u in range(DMA_UNROLL):
            row_copy(gi * DMA_UNROLL + u).start(priority=u % 2)
        return c

    def wait(gi, c):
        for u in range(DMA_UNROLL):
            row_copy(gi * DMA_UNROLL + u).wait()
        return c

    lax.fori_loop(0, n_groups, start, 0)
    lax.fori_loop(0, n_groups, wait, 0)
    o_ref[...] = rows_ref[...].astype(o_ref.dtype)


def gather_rows(h, src, step_rows):
    n = src.shape[0]
    d = h.shape[1]
    grid_spec = pltpu.PrefetchScalarGridSpec(
        num_scalar_prefetch=2, grid=(n // GATHER_ROWS,),
        in_specs=[pl.BlockSpec(memory_space=pl.ANY)],
        out_specs=pl.BlockSpec((GATHER_ROWS, d), lambda i, s, nv: (i, 0)),
        scratch_shapes=[pltpu.VMEM((GATHER_ROWS, d), h.dtype), pltpu.SemaphoreType.DMA(())])
    return pl.pallas_call(
        _gather_rows_kernel,
        out_shape=jax.ShapeDtypeStruct((n, d), BF16),
        grid_spec=grid_spec,
        compiler_params=_params("arbitrary"),
        name="moe_gather",
    )(src, step_rows, h)


def _moe_kernel(be_ref, nused_ref, nquarter_ref, x_ref, w1_ref, w3_ref, w2_ref, o_ref):
    i = pl.program_id(0)
    f = pl.program_id(1)

    @pl.when(f == 0)
    def _():
        o_ref[...] = jnp.zeros_like(o_ref)

    def mlp(rows):
        x = x_ref[0:rows, :]
        h1 = jnp.dot(x, w1_ref[...].astype(BF16), preferred_element_type=F32)
        h3 = jnp.dot(x, w3_ref[...].astype(BF16), preferred_element_type=F32)
        g = _silu(h1) * h3
        o_ref[0:rows, :] += jnp.dot(g.astype(BF16), w2_ref[...].astype(BF16), preferred_element_type=F32)

    for nq in range(1, MOE_QUARTERS + 1):
        @pl.when(nquarter_ref[i] == nq)
        def _(nq=nq):
            mlp(nq * (MOE_ROW_TILE // MOE_QUARTERS))


def moe_experts(buf, block_expert, n_used, block_quarters, w1, w3, w2):
    r, d = buf.shape
    tm, tf = MOE_ROW_TILE, MOE_FF_TILE
    nf = D_FF_EXPERT // tf

    def f_idx(i, f, nu):
        return jnp.where(i < nu[0], f, nf - 1)

    grid_spec = pltpu.PrefetchScalarGridSpec(
        num_scalar_prefetch=3,
        grid=(r // tm, nf),
        in_specs=[
            pl.BlockSpec((tm, d), lambda i, f, be, nu, nh: (jnp.minimum(i, nu[0] - 1), 0)),
            pl.BlockSpec((None, d, tf), lambda i, f, be, nu, nh: (be[i], 0, f_idx(i, f, nu))),
            pl.BlockSpec((None, d, tf), lambda i, f, be, nu, nh: (be[i], 0, f_idx(i, f, nu))),
            pl.BlockSpec((None, tf, d), lambda i, f, be, nu, nh: (be[i], f_idx(i, f, nu), 0)),
        ],
        out_specs=pl.BlockSpec((tm, d), lambda i, f, be, nu, nh: (i, 0)),
    )
    return pl.pallas_call(
        _moe_kernel,
        out_shape=jax.ShapeDtypeStruct((r, d), F32),
        grid_spec=grid_spec,
        compiler_params=_params("arbitrary", "arbitrary"),
        name="moe_experts",
    )(block_expert, n_used, block_quarters, buf, w1, w3, w2)


def _combine_kernel(dest_ref, y_hbm, x_ref, route_ref, gate_ref, gfin_ref, o_ref, ybuf_ref, sem):
    base = pl.program_id(0) * GATHER_ROWS

    def row_copy(t, k):
        return pltpu.make_async_copy(y_hbm.at[pl.ds(dest_ref[TOP_K * (base + t) + k], 1)],
                                     ybuf_ref.at[k, pl.ds(t, 1)], sem)

    def start(gi, c):
        for u in range(DMA_UNROLL):
            for k in range(TOP_K):
                row_copy(gi * DMA_UNROLL + u, k).start(priority=k)
        return c

    def wait(gi, c):
        for u in range(DMA_UNROLL):
            for k in range(TOP_K):
                row_copy(gi * DMA_UNROLL + u, k).wait()
        return c

    lax.fori_loop(0, GATHER_ROWS // DMA_UNROLL, start, 0)
    lax.fori_loop(0, GATHER_ROWS // DMA_UNROLL, wait, 0)
    route = route_ref[...]
    f = ybuf_ref[0] * route[:, 2:3] + ybuf_ref[1] * route[:, 3:4]
    xn = x_ref[...] + gate_ref[...] * f
    o_ref[...] = _rms(xn) * gfin_ref[...]


def moe_combine_final(y, dest, x, route, mods, layer, final_norm):
    tiles_per_batch = SEQ // GATHER_ROWS
    grid_spec = pltpu.PrefetchScalarGridSpec(
        num_scalar_prefetch=1, grid=(ROWS_LAT // GATHER_ROWS,),
        in_specs=[pl.BlockSpec(memory_space=pl.ANY),
                  pl.BlockSpec((GATHER_ROWS, D_MODEL), lambda i, d: (i, 0)),
                  pl.BlockSpec((GATHER_ROWS, LANES), lambda i, d: (i, 0)),
                  pl.BlockSpec((None, None, None, 1, D_MODEL), lambda i, d: (layer, i // tiles_per_batch, 5, 0, 0)),
                  pl.BlockSpec((1, D_MODEL), lambda i, d: (0, 0))],
        out_specs=pl.BlockSpec((GATHER_ROWS, D_MODEL), lambda i, d: (i, 0)),
        scratch_shapes=[pltpu.VMEM((TOP_K, GATHER_ROWS, D_MODEL), F32), pltpu.SemaphoreType.DMA(())])
    return pl.pallas_call(
        _combine_kernel,
        out_shape=jax.ShapeDtypeStruct((ROWS_LAT, D_MODEL), F32),
        grid_spec=grid_spec,
        compiler_params=_params("arbitrary"),
        name="moe_combine_final",
    )(dest, y, x, route, mods, final_norm.reshape(1, D_MODEL))


def moe_dispatch(route):
    expert = route[:, :TOP_K].astype(jnp.int32).reshape(-1)
    onehot = (expert[:, None] == jnp.arange(N_EXPERTS)[None, :]).astype(jnp.int32)
    incl = jnp.cumsum(onehot, axis=0)
    rank = jnp.sum((incl - onehot) * onehot, axis=1)
    counts = incl[-1]
    tm = MOE_ROW_TILE
    padded = (counts + tm - 1) // tm * tm
    pad_end = jnp.cumsum(padded)
    dest = ((pad_end - padded)[expert] + rank).astype(jnp.int32)
    n_blocks = MOE_SLOTS // tm
    n_used = (pad_end[-1] // tm).astype(jnp.int32)
    blocks = jnp.arange(n_blocks)
    block_expert = jnp.minimum(jnp.searchsorted(pad_end, blocks * tm, side='right'), N_EXPERTS - 1).astype(jnp.int32)
    block_rows = jnp.clip(counts[block_expert] - (blocks * tm - (pad_end - padded)[block_expert]), 0, tm)
    block_rows = jnp.where(blocks < n_used, block_rows, 0)
    block_expert = jnp.where(blocks < n_used, block_expert, block_expert[jnp.maximum(n_used - 1, 0)])
    quarter = tm // MOE_QUARTERS
    block_quarters = ((block_rows + quarter - 1) // quarter).astype(jnp.int32)
    steps = jnp.arange(MOE_SLOTS // GATHER_ROWS)
    per_block = tm // GATHER_ROWS
    step_rows = jnp.clip(block_rows[steps // per_block] - (steps % per_block) * GATHER_ROWS, 0, GATHER_ROWS)
    token = jnp.repeat(jnp.arange(ROWS_LAT, dtype=jnp.int32), TOP_K)
    src = jnp.zeros((MOE_SLOTS,), jnp.int32).at[dest].set(token)
    return dest, src, step_rows.astype(jnp.int32), block_expert, n_used.reshape(1), block_quarters


def _rope_table(n):
    rows = n // GRID_W
    row = jnp.repeat(jnp.arange(rows, dtype=F32), GRID_W)
    col = jnp.tile(jnp.arange(GRID_W, dtype=F32), rows)
    axis_dim = ROPE_DIM // 2
    inv = ROPE_BASE ** (-jnp.arange(0, axis_dim, 2, dtype=F32) / axis_dim)
    ang_r, ang_c = row[:, None] * inv, col[:, None] * inv
    cr, sr, cc, sc = jnp.cos(ang_r), jnp.sin(ang_r), jnp.cos(ang_c), jnp.sin(ang_c)
    lat = jnp.concatenate([cr, cr, cc, cc, -sr, sr, -sc, sc], axis=1)
    ident = jnp.concatenate([jnp.ones((TM, ROPE_DIM), F32), jnp.zeros((TM, ROPE_DIM), F32)], axis=1)
    return jnp.concatenate([lat, ident], axis=0)


def _layer_weights(w_in, w_uq, w_ukv):
    o = np.cumsum((0,) + PROJ_SIZES)
    seg = lambda i: w_in[:, int(o[i]):int(o[i + 1])]
    w_main = jnp.concatenate([seg(4), seg(3), seg(6), seg(7), seg(0), seg(1)], axis=1).astype(BF16)
    kr_w = seg(2)
    w_small = jnp.concatenate([seg(5), kr_w, kr_w[:, ROPE_SWAP]], axis=1).astype(BF16)
    wq = w_uq.reshape(Q_LORA, N_HEADS, NOPE_DIM + ROPE_DIM)
    rope_w = wq[:, :, NOPE_DIM:]
    w_q = jnp.concatenate([wq[:, :, :NOPE_DIM], rope_w, rope_w[:, :, ROPE_SWAP]], axis=2)
    w_q = w_q.reshape(Q_LORA, N_HEADS * HEAD_Q).astype(BF16)
    wkv = w_ukv.reshape(KV_LORA, N_HEADS, NOPE_DIM + V_DIM)
    w_kv = jnp.concatenate([wkv[:, :, :NOPE_DIM].reshape(KV_LORA, -1), wkv[:, :, NOPE_DIM:].reshape(KV_LORA, -1)],
                           axis=1).astype(BF16)
    return w_main, w_small, w_q, w_kv


def kernel(x, c, ctx, c_ctx, norm_mix, norm_ffn, w_ada, b_ada, w_in, q_norm, w_uq, kv_norm, w_ukv,
           conv_w, conv_b, a_log, dt_bias, d_skip, ssm_norm, w_oa, w_ob, w_out, w1_dense, w3_dense,
           w2_dense, w_router, w1_moe, w3_moe, w2_moe, final_norm):
    tab = _rope_table(SEQ)
    cc = jnp.concatenate([c, c_ctx[None], jnp.zeros((8 - BATCH - 1, D_MODEL), F32)], axis=0)
    mods = adaln(cc, w_ada, b_ada)
    head_of = np.arange(D_INNER) // SSM_HEAD_DIM
    e1 = (np.arange(SSM_HEADS)[:, None] == head_of[None, :]).astype(np.float32)
    e3 = jnp.asarray(np.concatenate([e1, e1, e1], axis=0), dtype=BF16)
    xr = jnp.concatenate([x.reshape(ROWS_LAT, D_MODEL), ctx.reshape(ROWS_CTX, D_MODEL)], axis=0)
    out = None
    for l in range(DEPTH):
        last = l == DEPTH - 1
        rows = ROWS_LAT if last else ROWS
        w_main, w_small, w_q, w_kv = _layer_weights(w_in[l], w_uq[l], w_ukv[l])
        p_main, dt_raw, kr = norm_proj(xr, norm_mix[l], mods, l, w_main, w_small, tab)
        q = up_proj(p_main, COL_CQ, q_norm[l], w_q, tab)
        kv = up_proj(p_main, COL_CKV, kv_norm[l], w_kv, None)
        att = latent_attention(q, kv, kr)
        att_ctx = att if last else context_attention(q, kv, kr)
        u = ssd_mixer(p_main, dt_raw, conv_w[l], conv_b[l], dt_bias[l], a_log[l], d_skip[l], ssm_norm[l], e3)
        mixed = merge_branches(att, att_ctx, u, p_main, w_oa[l].astype(BF16), w_ob[l].astype(BF16), rows)
        if l % 2 == 0:
            xr, h2 = out_proj_residual(mixed, w_out[l].astype(BF16), xr, norm_ffn[l], mods, l, rows)
            g = ffn_up(h2, w1_dense[l // 2].astype(BF16), w3_dense[l // 2].astype(BF16))
            xr = ffn_down_residual(g, w2_dense[l // 2].astype(BF16), xr, mods, l)
        else:
            wr = w_router[l // 2]
            x_lat, h2, route = out_proj_residual(mixed, w_out[l].astype(BF16), xr, norm_ffn[l], mods, l, rows, wr)
            dest, src, step_rows, block_expert, n_used, block_quarters = moe_dispatch(route)
            buf = gather_rows(h2, src, step_rows)
            y = moe_experts(buf, block_expert, n_used, block_quarters, w1_moe[l // 2], w3_moe[l // 2], w2_moe[l // 2])
            out = moe_combine_final(y, dest, x_lat, route, mods, l, final_norm)
    return out.reshape(BATCH, SEQ, D_MODEL)
```

```python
import functools

import numpy as np
import jax
import jax.numpy as jnp
from jax import lax
from jax.experimental import pallas as pl
from jax.experimental.pallas import tpu as pltpu

D_MODEL = 2048
BATCH = 2
SEQ = 4096
DEPTH = 2
CTX_LEN = 256
GRID_W = 64
EPS = 1e-6

N_HEADS = D_MODEL // 128
Q_LORA = 512
KV_LORA = 512
NOPE_DIM = 128
ROPE_DIM = 64
V_DIM = 128
ROPE_BASE = 10000.0
SM_SCALE = (NOPE_DIM + ROPE_DIM) ** -0.5

D_INNER = 2 * D_MODEL
SSM_HEAD_DIM = 64
SSM_HEADS = D_INNER // SSM_HEAD_DIM
SSM_GROUPS = 8
D_STATE = 128
CONV_K = 5
CONV_DIM = D_INNER + 2 * SSM_GROUPS * D_STATE
CHUNK = 128

PROJ_SIZES = (Q_LORA, KV_LORA, ROPE_DIM, D_INNER, CONV_DIM, 2 * SSM_HEADS, D_MODEL, D_MODEL)

D_FF = 256 * ((8 * D_MODEL // 3 + 255) // 256)
N_EXPERTS = 8
TOP_K = 2
D_FF_EXPERT = 7 * D_MODEL // 2

F32 = jnp.float32
BF16 = jnp.bfloat16
HIGHEST = lax.Precision.HIGHEST

VMEM_LIMIT_BYTES = 56 * 1024 * 1024
LANES = 128
HALO_ROWS = 16
OFF_CENTRE_TAPS = (0, 1, 3, 4)
CONV_COL_TILE = 2048
LOG2E = 1.4426950408889634
QK_SCALE = SM_SCALE * LOG2E

ROWS_LAT = BATCH * SEQ
ROWS_CTX = BATCH * CTX_LEN
ROWS = ROWS_LAT + ROWS_CTX
TM = 512
PROJ_COL_TILE = 1536
TILES_PER_BATCH = SEQ // TM
N_LAT_TILES = ROWS_LAT // TM
CHUNKS_LAT = SEQ // CHUNK
CHUNKS_CTX = CTX_LEN // CHUNK
CHUNKS_SEQ = CHUNKS_LAT + CHUNKS_CTX

COL_XBC = 0
COL_Z = COL_XBC + CONV_DIM
COL_GA = COL_Z + D_INNER
COL_GB = COL_GA + D_MODEL
COL_CQ = COL_GB + D_MODEL
COL_CKV = COL_CQ + Q_LORA
N_MAIN = COL_CKV + KV_LORA
HEAD_Q = 2 * LANES

MOE_ROW_TILE = 1024
MOE_FF_TILE = 512
MOE_QUARTERS = 4
MOE_SLOTS = ROWS_LAT * TOP_K + N_EXPERTS * MOE_ROW_TILE
GATHER_ROWS = 256
DMA_UNROLL = 8
ATTN_Q_TILE = 2048
ATTN_KV_CHUNK = 256
ROPE_SWAP = np.concatenate([np.arange(16, 32), np.arange(0, 16), np.arange(48, 64), np.arange(32, 48)])


def _params(*sem):
    return pltpu.CompilerParams(dimension_semantics=sem, vmem_limit_bytes=VMEM_LIMIT_BYTES)


def _mod_row(i):
    return jnp.minimum(i // TILES_PER_BATCH, BATCH)


def _rope_row(i):
    return jnp.where(i < N_LAT_TILES, i % TILES_PER_BATCH, TILES_PER_BATCH)


def _mod_spec(layer, k):
    return pl.BlockSpec((None, None, None, 1, D_MODEL), lambda i, *_: (layer, _mod_row(i), k, 0, 0))


def _rms(x):
    return x * lax.rsqrt(jnp.mean(x * x, axis=-1, keepdims=True) + EPS)


def _silu(x):
    return x * jax.nn.sigmoid(x)


def _rope(x, tab):
    y = x * tab
    y = y + pltpu.roll(y, ROPE_DIM, axis=1)
    lane = lax.broadcasted_iota(jnp.int32, y.shape, 1)
    return jnp.where(lane < ROPE_DIM, y, 0.0)


def _adaln_kernel(c_ref, wa_ref, wb_ref, b_ref, o_ref):
    @pl.when(pl.program_id(1) == 0)
    def _():
        o_ref[...] = jnp.broadcast_to(b_ref[...], o_ref.shape)

    a = _silu(c_ref[...]).astype(BF16)
    half = wa_ref.shape[1]
    o_ref[:, :half] += jnp.dot(a, wa_ref[...].astype(BF16), preferred_element_type=F32)
    o_ref[:, half:] += jnp.dot(a, wb_ref[...].astype(BF16), preferred_element_type=F32)


def adaln(cc, w_ada, b_ada):
    tk = 256
    n = 6 * D_MODEL
    out = pl.pallas_call(
        _adaln_kernel,
        out_shape=jax.ShapeDtypeStruct((DEPTH, 8, n), F32),
        grid=(DEPTH, D_MODEL // tk),
        in_specs=[pl.BlockSpec((8, tk), lambda l, k: (0, k)),
                  pl.BlockSpec((None, tk, n // 2), lambda l, k: (l, k, 0)),
                  pl.BlockSpec((None, tk, n // 2), lambda l, k: (l, k, 1)),
                  pl.BlockSpec((None, 1, n), lambda l, k: (l, 0, 0))],
        out_specs=pl.BlockSpec((None, 8, n), lambda l, k: (l, 0, 0)),
        compiler_params=_params("parallel", "arbitrary"),
        name="adaln",
    )(cc, w_ada, w_ada, b_ada.reshape(DEPTH, 1, n))
    return out.reshape(DEPTH, 8, 6, 1, D_MODEL)


def _norm_proj_kernel(x_ref, g_ref, sh_ref, sc_ref, w_ref, ws_ref, tab_ref, o_ref, dt_ref, kr_ref, h_ref):
    @pl.when(pl.program_id(1) == 0)
    def _():
        y = _rms(x_ref[...]) * g_ref[...]
        h = (y * (1 + sc_ref[...]) + sh_ref[...]).astype(h_ref.dtype)
        h_ref[...] = h
        acc = jnp.dot(h, ws_ref[...], preferred_element_type=F32)
        dt_ref[...] = acc[:, :LANES]
        kr_ref[...] = _rope(acc[:, LANES:], tab_ref[...]).astype(kr_ref.dtype)

    o_ref[...] = jnp.dot(h_ref[...], w_ref[...], preferred_element_type=F32).astype(o_ref.dtype)


def norm_proj(x, g, mods, layer, w, w_small, tab):
    n = w.shape[1]
    tn = PROJ_COL_TILE
    return pl.pallas_call(
        _norm_proj_kernel,
        out_shape=(jax.ShapeDtypeStruct((ROWS, n), BF16),
                   jax.ShapeDtypeStruct((ROWS, LANES), F32), jax.ShapeDtypeStruct((ROWS, LANES), BF16)),
        grid=(ROWS // TM, n // tn),
        in_specs=[pl.BlockSpec((TM, D_MODEL), lambda i, j: (i, 0)),
                  pl.BlockSpec((1, D_MODEL), lambda i, j: (0, 0)),
                  _mod_spec(layer, 0), _mod_spec(layer, 1),
                  pl.BlockSpec((D_MODEL, tn), lambda i, j: (0, j)),
                  pl.BlockSpec((D_MODEL, 2 * LANES), lambda i, j: (0, 0)),
                  pl.BlockSpec((TM, LANES), lambda i, j: (_rope_row(i), 0))],
        out_specs=(pl.BlockSpec((TM, tn), lambda i, j: (i, j)),
                   pl.BlockSpec((TM, LANES), lambda i, j: (i, 0)),
                   pl.BlockSpec((TM, LANES), lambda i, j: (i, 0))),
        scratch_shapes=[pltpu.VMEM((TM, D_MODEL), BF16)],
        compiler_params=_params("parallel", "arbitrary"),
        name="norm_proj",
    )(x, g.reshape(1, D_MODEL), mods, mods, w, w_small, tab)


def _up_kernel(c_ref, g_ref, w_ref, *rest, rope, heads_per_tile):
    if rope:
        tab_ref, o_ref = rest
    else:
        (o_ref,) = rest
    c = _rms(c_ref[...].astype(F32)) * g_ref[...]
    acc = jnp.dot(c.astype(BF16), w_ref[...], preferred_element_type=F32)
    if not rope:
        o_ref[...] = acc.astype(o_ref.dtype)
        return
    tab = tab_ref[...]
    acc = acc * QK_SCALE
    for hh in range(heads_per_tile):
        base = hh * HEAD_Q
        o_ref[:, base:base + LANES] = acc[:, base:base + LANES].astype(o_ref.dtype)
        o_ref[:, base + LANES:base + HEAD_Q] = _rope(acc[:, base + LANES:base + HEAD_Q], tab).astype(o_ref.dtype)


def up_proj(p_main, col, g, w, tab):
    lora, n = w.shape
    tn = n
    rope = tab is not None
    in_specs = [pl.BlockSpec((TM, lora), lambda i, j: (i, col // lora)),
                pl.BlockSpec((1, lora), lambda i, j: (0, 0)),
                pl.BlockSpec((lora, tn), lambda i, j: (0, j))]
    args = [p_main, g.reshape(1, lora), w]
    if rope:
        in_specs.append(pl.BlockSpec((TM, LANES), lambda i, j: (_rope_row(i), 0)))
        args.append(tab)
    return pl.pallas_call(
        functools.partial(_up_kernel, rope=rope, heads_per_tile=tn // HEAD_Q),
        out_shape=jax.ShapeDtypeStruct((ROWS, n), BF16),
        grid=(ROWS // TM, n // tn),
        in_specs=in_specs,
        out_specs=pl.BlockSpec((TM, tn), lambda i, j: (i, j)),
        compiler_params=_params("parallel", "parallel"),
        name="up_proj_rope" if rope else "up_proj",
    )(*args)


def _softmax_chunks(q, kcat_ref, v_chunks):
    m = l = acc = None
    for start, size, v in v_chunks:
        k = kcat_ref[start:start + size, :]
        s = lax.dot_general(q, k, (((1,), (1,)), ((), ())), preferred_element_type=F32)
        m_cur = jnp.max(s, axis=-1, keepdims=True)
        if m is None:
            m_new = m_cur
            p = jnp.exp2(s - m_new)
            l = jnp.sum(p, axis=-1, keepdims=True)
            acc = jnp.dot(p.astype(BF16), v, preferred_element_type=F32)
        else:
            m_new = jnp.maximum(m, m_cur)
            alpha = jnp.exp2(m - m_new)
            p = jnp.exp2(s - m_new)
            l = alpha * l + jnp.sum(p, axis=-1, keepdims=True)
            acc = alpha * acc + jnp.dot(p.astype(BF16), v, preferred_element_type=F32)
        m = m_new
    return acc / l


def _lat_attn_kernel(q_ref, knl_ref, krl_ref, vl_ref, knc_ref, krc_ref, vc_ref, o_ref, kcat_ref):
    @pl.when(pl.program_id(2) == 0)
    def _():
        kcat_ref[0:CTX_LEN, 0:LANES] = knc_ref[...]
        kcat_ref[0:CTX_LEN, LANES:HEAD_Q] = krc_ref[...]
        kcat_ref[CTX_LEN:, 0:LANES] = knl_ref[...]
        kcat_ref[CTX_LEN:, LANES:HEAD_Q] = krl_ref[...]

    chunks = [(0, CTX_LEN, vc_ref[...])]
    for s in range(0, SEQ, ATTN_KV_CHUNK):
        chunks.append((CTX_LEN + s, ATTN_KV_CHUNK, vl_ref[s:s + ATTN_KV_CHUNK, :]))
    o_ref[...] = _softmax_chunks(q_ref[...], kcat_ref, chunks).astype(o_ref.dtype)


def latent_attention(q, kv, kr):
    tq = ATTN_Q_TILE
    qt = SEQ // tq
    ctx_blk = ROWS_LAT // CTX_LEN
    return pl.pallas_call(
        _lat_attn_kernel,
        out_shape=jax.ShapeDtypeStruct((ROWS_LAT, D_MODEL), BF16),
        grid=(BATCH, N_HEADS, qt),
        in_specs=[pl.BlockSpec((tq, HEAD_Q), lambda b, h, i: (b * qt + i, h)),
                  pl.BlockSpec((SEQ, LANES), lambda b, h, i: (b, h)),
                  pl.BlockSpec((SEQ, LANES), lambda b, h, i: (b, 0)),
                  pl.BlockSpec((SEQ, LANES), lambda b, h, i: (b, N_HEADS + h)),
                  pl.BlockSpec((CTX_LEN, LANES), lambda b, h, i: (ctx_blk + b, h)),
                  pl.BlockSpec((CTX_LEN, LANES), lambda b, h, i: (ctx_blk + b, 0)),
                  pl.BlockSpec((CTX_LEN, LANES), lambda b, h, i: (ctx_blk + b, N_HEADS + h))],
        out_specs=pl.BlockSpec((tq, LANES), lambda b, h, i: (b * qt + i, h)),
        scratch_shapes=[pltpu.VMEM((CTX_LEN + SEQ, HEAD_Q), BF16)],
        compiler_params=_params("parallel", "parallel", "arbitrary"),
        name="latent_attention",
    )(q, kv, kr, kv, kv, kr, kv)


def _ctx_attn_kernel(q_ref, kn_ref, kr_ref, v_ref, o_ref, kcat_ref):
    kcat_ref[:, 0:LANES] = kn_ref[...]
    kcat_ref[:, LANES:HEAD_Q] = kr_ref[...]
    o_ref[...] = _softmax_chunks(q_ref[...], kcat_ref, [(0, CTX_LEN, v_ref[...])]).astype(o_ref.dtype)


def context_attention(q, kv, kr):
    ctx_blk = ROWS_LAT // CTX_LEN
    return pl.pallas_call(
        _ctx_attn_kernel,
        out_shape=jax.ShapeDtypeStruct((ROWS_CTX, D_MODEL), BF16),
        grid=(BATCH, N_HEADS),
        in_specs=[pl.BlockSpec((CTX_LEN, HEAD_Q), lambda b, h: (ctx_blk + b, h)),
                  pl.BlockSpec((CTX_LEN, LANES), lambda b, h: (ctx_blk + b, h)),
                  pl.BlockSpec((CTX_LEN, LANES), lambda b, h: (ctx_blk + b, 0)),
                  pl.BlockSpec((CTX_LEN, LANES), lambda b, h: (ctx_blk + b, N_HEADS + h))],
        out_specs=pl.BlockSpec((CTX_LEN, LANES), lambda b, h: (b, h)),
        scratch_shapes=[pltpu.VMEM((CTX_LEN, HEAD_Q), BF16)],
        compiler_params=_params("parallel", "parallel"),
        name="context_attention",
    )(q, kv, kr, kv)


def _chunk_block(b, s, reverse):
    if reverse:
        return jnp.where(s < CHUNKS_CTX, ROWS_LAT // CHUNK + CHUNKS_CTX * b + (CHUNKS_CTX - 1 - s),
                         CHUNKS_LAT * b + (CHUNKS_SEQ - 1 - s))
    return jnp.where(s < CHUNKS_CTX, ROWS_LAT // CHUNK + CHUNKS_CTX * b + s,
                     CHUNKS_LAT * b + (s - CHUNKS_CTX))


def _expand_heads(v, e3_ref):
    hi = v.astype(BF16)
    r1 = v - hi.astype(F32)
    mid = r1.astype(BF16)
    lo = (r1 - mid.astype(F32)).astype(BF16)
    return jnp.dot(jnp.concatenate([hi, mid, lo], axis=1), e3_ref[...], preferred_element_type=F32)


def _ssd_chunk(xcol, dt_raw, dtb_ref, a_ref, e3_ref, state_ref, reverse):
    off = SSM_HEADS if reverse else 0
    heads = slice(off, off + SSM_HEADS)
    row = lax.broadcasted_iota(jnp.int32, (CHUNK, CHUNK), 0)
    colm = lax.broadcasted_iota(jnp.int32, (CHUNK, CHUNK), 1)
    keep = (colm >= row) if reverse else (colm <= row)
    tri = keep.astype(F32)
    dt = jax.nn.softplus(dt_raw + dtb_ref[...])
    a = dt * a_ref[...]
    cum = jnp.dot(tri, a, precision=HIGHEST, preferred_element_type=F32)
    cum2 = cum * LOG2E
    cum2_t = cum2.T
    last_row = 0 if reverse else CHUNK - 1
    dec_h = jnp.exp2(cum2)
    end_h = dt * jnp.exp2(cum2[last_row:last_row + 1, :] - cum2)
    dt_full = _expand_heads(dt[:, heads], e3_ref)
    end_full = _expand_heads(end_h[:, heads], e3_ref)
    dec_in = _expand_heads(dec_h[:, heads], e3_ref)
    chunk_dec = dec_in[last_row:last_row + 1, :]
    xs = xcol(0, D_INNER)
    xcb = (xs * dt_full).astype(BF16)
    xce = (xs * end_full).astype(BF16)
    gw = SSM_HEADS // SSM_GROUPS * SSM_HEAD_DIM
    lane = lax.broadcasted_iota(jnp.int32, (CHUNK, LANES), 1)
    ys = []
    for g in range(SSM_GROUPS):
        b32 = xcol(D_INNER + g * D_STATE, D_INNER + (g + 1) * D_STATE)
        bg = b32.astype(BF16)
        cg = xcol(D_INNER + (SSM_GROUPS + g) * D_STATE, D_INNER + (SSM_GROUPS + g + 1) * D_STATE).astype(BF16)
        cb = lax.dot_general(cg, bg, (((1,), (1,)), ((), ())), preferred_element_type=F32)
        st = state_ref[g]
        y_off = jnp.dot(cg, st.astype(BF16), preferred_element_type=F32) * dec_in[:, g * gw:(g + 1) * gw]
        upd = jnp.dot(b32.T.astype(BF16), xce[:, g * gw:(g + 1) * gw],
                      preferred_element_type=F32)
        state_ref[g] = st * chunk_dec[:, g * gw:(g + 1) * gw] + upd
        pairs = []
        for j in range(gw // LANES):
            h0 = off + g * (SSM_HEADS // SSM_GROUPS) + 2 * j
            ms = []
            for hh in (h0, h0 + 1):
                seg = jnp.where(keep, cum2[:, hh:hh + 1] - cum2_t[hh:hh + 1, :], -jnp.inf)
                ms.append((jnp.exp2(seg) * cb).astype(BF16))
            xp = xcb[:, g * gw + j * LANES:g * gw + (j + 1) * LANES]
            rhs = jnp.concatenate([jnp.where(lane < SSM_HEAD_DIM, xp, 0), jnp.where(lane >= SSM_HEAD_DIM, xp, 0)],
                                  axis=0)
            pairs.append(jnp.dot(jnp.concatenate(ms, axis=1), rhs, preferred_element_type=F32))
        ys.append(jnp.concatenate(pairs, axis=1) + y_off)
    return jnp.concatenate(ys, axis=1)


def _ssd_fwd_kernel(prev_ref, cur_ref, next_ref, dt_ref, shift_ref, cw_ref, cbias_ref, dtb_ref, a_ref, e3_ref,
                    y_ref, xbc_ref, state_ref, xs_ref):
    s = pl.program_id(1)

    @pl.when(s == 0)
    def _():
        state_ref[...] = jnp.zeros_like(state_ref)

    first = (s == 0) | (s == CHUNKS_CTX)
    final = (s == CHUNKS_CTX - 1) | (s == CHUNKS_SEQ - 1)
    halo_zeros = jnp.zeros(prev_ref.shape, prev_ref.dtype)
    prev = jnp.where(first, halo_zeros, prev_ref[...])
    nxt = jnp.where(final, halo_zeros, next_ref[...])
    ext = jnp.concatenate([prev, cur_ref[...], nxt], axis=0)
    for c0 in range(0, CONV_DIM, CONV_COL_TILE):
        cs = slice(c0, c0 + CONV_COL_TILE)
        shifted = jnp.dot(shift_ref[...], ext[:, cs], preferred_element_type=F32)
        acc = cbias_ref[:, cs] + cw_ref[CONV_K // 2:CONV_K // 2 + 1, cs] * cur_ref[:, cs].astype(F32)
        for j, k in enumerate(OFF_CENTRE_TAPS):
            acc = acc + cw_ref[k:k + 1, cs] * shifted[j * CHUNK:(j + 1) * CHUNK, :]
        xs = _silu(acc)
        xs_ref[:, cs] = xs
        xbc_ref[:, cs] = xs.astype(xbc_ref.dtype)
    y_ref[...] = _ssd_chunk(lambda a, b: xs_ref[:, a:b], dt_ref[...], dtb_ref, a_ref, e3_ref, state_ref,
                            reverse=False)


def _ssd_bwd_kernel(xbc_ref, dt_ref, yf_ref, z0_ref, z1_ref, dtb_ref, a_ref, e3_ref, dsum_ref, nw_ref,
                    u_ref, state_ref):
    @pl.when(pl.program_id(1) == 0)
    def _():
        state_ref[...] = jnp.zeros_like(state_ref)

    xcol = lambda a, b: xbc_ref[:, a:b].astype(F32)
    y = _ssd_chunk(xcol, dt_ref[...], dtb_ref, a_ref, e3_ref, state_ref, reverse=True)
    y = y + yf_ref[...] + dsum_ref[...] * xcol(0, D_INNER)
    z = jnp.concatenate([z0_ref[...], z1_ref[...]], axis=1).astype(F32)
    u = y * _silu(z)
    gw = D_INNER // SSM_GROUPS
    for g in range(SSM_GROUPS):
        ug = u[:, g * gw:(g + 1) * gw]
        u_ref[:, g * gw:(g + 1) * gw] = (_rms(ug) * nw_ref[:, g * gw:(g + 1) * gw]).astype(u_ref.dtype)


def ssd_mixer(p_main, dt_raw, conv_w, conv_b, dt_bias, a_log, d_skip, ssm_norm, e3):
    zb = COL_Z // (D_INNER // 2)
    dtb = dt_bias.reshape(1, 2 * SSM_HEADS).astype(F32)
    a_neg = (-jnp.exp(a_log.astype(F32))).reshape(1, 2 * SSM_HEADS)
    dsum = jnp.repeat(d_skip[0] + d_skip[1], SSM_HEAD_DIM).reshape(1, D_INNER)
    halo_per_chunk = CHUNK // HALO_ROWS
    n_halo = ROWS // HALO_ROWS
    shift_np = np.zeros((len(OFF_CENTRE_TAPS) * CHUNK, CHUNK + 2 * HALO_ROWS), np.float32)
    for j, k in enumerate(OFF_CENTRE_TAPS):
        shift_np[j * CHUNK + np.arange(CHUNK), HALO_ROWS + np.arange(CHUNK) + k - CONV_K // 2] = 1.0
    shift = jnp.asarray(shift_np, dtype=BF16)
    full = lambda shape: pl.BlockSpec(shape, lambda b, s: tuple(0 for _ in shape))

    def blk(reverse):
        return lambda b, s: (_chunk_block(b, s, reverse), 0)

    fwd_blk = blk(False)
    y_f, xbc = pl.pallas_call(
        _ssd_fwd_kernel,
        out_shape=(jax.ShapeDtypeStruct((ROWS, D_INNER), F32), jax.ShapeDtypeStruct((ROWS, CONV_DIM), BF16)),
        grid=(BATCH, CHUNKS_SEQ),
        in_specs=[
            pl.BlockSpec((HALO_ROWS, CONV_DIM),
                         lambda b, s: (jnp.maximum(_chunk_block(b, s, False) * halo_per_chunk - 1, 0), 0)),
            pl.BlockSpec((CHUNK, CONV_DIM), fwd_blk),
            pl.BlockSpec((HALO_ROWS, CONV_DIM),
                         lambda b, s: (jnp.minimum((_chunk_block(b, s, False) + 1) * halo_per_chunk, n_halo - 1), 0)),
            pl.BlockSpec((CHUNK, LANES), fwd_blk),
            full(shift.shape),
            full((CONV_K, CONV_DIM)), full((1, CONV_DIM)), full((1, LANES)), full((1, LANES)),
            full((3 * SSM_HEADS, D_INNER)),
        ],
        out_specs=(pl.BlockSpec((CHUNK, D_INNER), fwd_blk), pl.BlockSpec((CHUNK, CONV_DIM), fwd_blk)),
        scratch_shapes=[pltpu.VMEM((SSM_GROUPS, D_STATE, D_INNER // SSM_GROUPS), F32),
                        pltpu.VMEM((CHUNK, CONV_DIM), F32)],
        compiler_params=_params("parallel", "arbitrary"),
        name="ssd_forward",
    )(p_main, p_main, p_main, dt_raw, shift, conv_w, conv_b.reshape(1, CONV_DIM), dtb, a_neg, e3)
    bwd_blk = blk(True)
    return pl.pallas_call(
        _ssd_bwd_kernel,
        out_shape=jax.ShapeDtypeStruct((ROWS, D_INNER), BF16),
        grid=(BATCH, CHUNKS_SEQ),
        in_specs=[
            pl.BlockSpec((CHUNK, CONV_DIM), bwd_blk),
            pl.BlockSpec((CHUNK, LANES), bwd_blk),
            pl.BlockSpec((CHUNK, D_INNER), bwd_blk),
            pl.BlockSpec((CHUNK, D_INNER // 2), lambda b, s: (_chunk_block(b, s, True), zb)),
            pl.BlockSpec((CHUNK, D_INNER // 2), lambda b, s: (_chunk_block(b, s, True), zb + 1)),
            full((1, LANES)), full((1, LANES)), full((3 * SSM_HEADS, D_INNER)),
            full((1, D_INNER)), full((1, D_INNER)),
        ],
        out_specs=pl.BlockSpec((CHUNK, D_INNER), bwd_blk),
        scratch_shapes=[pltpu.VMEM((SSM_GROUPS, D_STATE, D_INNER // SSM_GROUPS), F32)],
        compiler_params=_params("parallel", "arbitrary"),
        name="ssd_backward",
    )(xbc, dt_raw, y_f, p_main, p_main, dtb, a_neg, e3, dsum, ssm_norm.reshape(1, D_INNER))


def _merge_kernel(attl_ref, attc_ref, u_ref, woa_ref, wob_ref, ga_ref, gb_ref, o_ref):
    att = jnp.where(pl.program_id(0) < N_LAT_TILES, attl_ref[...], attc_ref[...])
    o_a = jnp.dot(att, woa_ref[...], preferred_element_type=F32)
    o_b = jnp.dot(u_ref[...], wob_ref[...], preferred_element_type=F32)
    o_ref[...] = (jax.nn.sigmoid(ga_ref[...].astype(F32)) * o_a
                  + jax.nn.sigmoid(gb_ref[...].astype(F32)) * o_b).astype(o_ref.dtype)


def merge_branches(att_lat, att_ctx, u, p_main, w_oa, w_ob, rows):
    tn = 1024
    return pl.pallas_call(
        _merge_kernel,
        out_shape=jax.ShapeDtypeStruct((rows, D_MODEL), BF16),
        grid=(rows // TM, D_MODEL // tn),
        in_specs=[pl.BlockSpec((TM, D_MODEL), lambda i, j: (jnp.minimum(i, N_LAT_TILES - 1), 0)),
                  pl.BlockSpec((TM, D_MODEL), lambda i, j: (0, 0)),
                  pl.BlockSpec((TM, D_INNER), lambda i, j: (i, 0)),
                  pl.BlockSpec((D_MODEL, tn), lambda i, j: (0, j)),
                  pl.BlockSpec((D_INNER, tn), lambda i, j: (0, j)),
                  pl.BlockSpec((TM, tn), lambda i, j: (i, COL_GA // tn + j)),
                  pl.BlockSpec((TM, tn), lambda i, j: (i, COL_GB // tn + j))],
        out_specs=pl.BlockSpec((TM, tn), lambda i, j: (i, j)),
        compiler_params=_params("parallel", "parallel"),
        name="merge_branches",
    )(att_lat, att_ctx, u, w_oa, w_ob, p_main, p_main)


def _route(h, wr_ref):
    lane = lax.broadcasted_iota(jnp.int32, (h.shape[0], LANES), 1)
    lg = jnp.full((h.shape[0], LANES), -jnp.inf, F32)
    for e in range(N_EXPERTS):
        lg = jnp.where(lane == e, jnp.sum(h * wr_ref[e:e + 1, :], axis=-1, keepdims=True), lg)
    v1 = jnp.max(lg, axis=-1, keepdims=True)
    i1 = jnp.min(jnp.where(lg == v1, lane, LANES), axis=-1, keepdims=True)
    lg2 = jnp.where(lane == i1, -jnp.inf, lg)
    v2 = jnp.max(lg2, axis=-1, keepdims=True)
    i2 = jnp.min(jnp.where(lg2 == v2, lane, LANES), axis=-1, keepdims=True)
    e = jnp.exp(v2 - v1)
    g1 = 1.0 / (1.0 + e)
    g2 = e / (1.0 + e)
    return jnp.where(lane == 0, i1.astype(F32),
                     jnp.where(lane == 1, i2.astype(F32),
                               jnp.where(lane == 2, g1, jnp.where(lane == 3, g2, 0.0))))


def _out_res_kernel(m_ref, w_ref, x_ref, gate_ref, g2_ref, sh_ref, sc_ref, *rest, route):
    if route:
        wr_ref, xo_ref, ho_ref, ro_ref = rest
    else:
        xo_ref, ho_ref = rest
    acc = jnp.dot(m_ref[...], w_ref[...], preferred_element_type=F32)
    xn = x_ref[...] + gate_ref[...] * acc
    xo_ref[...] = xn
    h = (_rms(xn) * g2_ref[...]) * (1 + sc_ref[...]) + sh_ref[...]
    ho_ref[...] = h.astype(ho_ref.dtype)
    if route:
        ro_ref[...] = _route(h, wr_ref)


def out_proj_residual(mixed, w_out, x, g2, mods, layer, rows, w_router=None):
    route = w_router is not None
    h_dtype = F32 if route else BF16
    in_specs = [pl.BlockSpec((TM, D_MODEL), lambda i: (i, 0)),
                pl.BlockSpec((D_MODEL, D_MODEL), lambda i: (0, 0)),
                pl.BlockSpec((TM, D_MODEL), lambda i: (i, 0)),
                _mod_spec(layer, 2), pl.BlockSpec((1, D_MODEL), lambda i: (0, 0)),
                _mod_spec(layer, 3), _mod_spec(layer, 4)]
    args = [mixed, w_out, x, mods, g2.reshape(1, D_MODEL), mods, mods]
    out_shape = [jax.ShapeDtypeStruct((rows, D_MODEL), F32), jax.ShapeDtypeStruct((rows, D_MODEL), h_dtype)]
    out_specs = [pl.BlockSpec((TM, D_MODEL), lambda i: (i, 0)), pl.BlockSpec((TM, D_MODEL), lambda i: (i, 0))]
    if route:
        in_specs.append(pl.BlockSpec((N_EXPERTS, D_MODEL), lambda i: (0, 0)))
        args.append(w_router.T)
        out_shape.append(jax.ShapeDtypeStruct((rows, LANES), F32))
        out_specs.append(pl.BlockSpec((TM, LANES), lambda i: (i, 0)))
    return pl.pallas_call(
        functools.partial(_out_res_kernel, route=route),
        out_shape=tuple(out_shape),
        grid=(rows // TM,),
        in_specs=in_specs,
        out_specs=tuple(out_specs),
        compiler_params=_params("parallel"),
        name="out_proj_residual",
    )(*args)


def _ffn_up_kernel(h_ref, w1_ref, w3_ref, o_ref):
    h = h_ref[...]
    a = jnp.dot(h, w1_ref[...], preferred_element_type=F32)
    b = jnp.dot(h, w3_ref[...], preferred_element_type=F32)
    o_ref[...] = (_silu(a) * b).astype(o_ref.dtype)


def ffn_up(h, w1, w3):
    tn = D_FF // 4
    return pl.pallas_call(
        _ffn_up_kernel,
        out_shape=jax.ShapeDtypeStruct((ROWS, D_FF), BF16),
        grid=(ROWS // TM, D_FF // tn),
        in_specs=[pl.BlockSpec((TM, D_MODEL), lambda i, j: (i, 0)),
                  pl.BlockSpec((D_MODEL, tn), lambda i, j: (0, j)),
                  pl.BlockSpec((D_MODEL, tn), lambda i, j: (0, j))],
        out_specs=pl.BlockSpec((TM, tn), lambda i, j: (i, j)),
        compiler_params=_params("parallel", "parallel"),
        name="ffn_up",
    )(h, w1, w3)


def _ffn_down_kernel(g_ref, w_ref, x_ref, gate_ref, o_ref):
    acc = jnp.dot(g_ref[...], w_ref[...], preferred_element_type=F32)
    o_ref[...] = x_ref[...] + gate_ref[...] * acc


def ffn_down_residual(g, w2, x, mods, layer):
    tn = 1024
    gate_spec = pl.BlockSpec((None, None, None, 1, tn), lambda i, j: (layer, _mod_row(i), 5, 0, j))
    return pl.pallas_call(
        _ffn_down_kernel,
        out_shape=jax.ShapeDtypeStruct((ROWS, D_MODEL), F32),
        grid=(ROWS // TM, D_MODEL // tn),
        in_specs=[pl.BlockSpec((TM, D_FF), lambda i, j: (i, 0)),
                  pl.BlockSpec((D_FF, tn), lambda i, j: (0, j)),
                  pl.BlockSpec((TM, tn), lambda i, j: (i, j)),
                  gate_spec],
        out_specs=pl.BlockSpec((TM, tn), lambda i, j: (i, j)),
        compiler_params=_params("parallel", "parallel"),
        name="ffn_down_residual",
    )(g, w2, x, mods)


def _gather_rows_kernel(src_ref, nvalid_ref, h_hbm, o_ref, rows_ref, sem):
    step = pl.program_id(0)
    base = step * GATHER_ROWS

    @pl.when(step == 0)
    def _():
        rows_ref[...] = jnp.zeros_like(rows_ref)

    n_groups = lax.shift_right_logical(nvalid_ref[step] + (DMA_UNROLL - 1), DMA_UNROLL.bit_length() - 1)

    def row_copy(j):
        return pltpu.make_async_copy(h_hbm.at[pl.ds(src_ref[base + j], 1)], rows_ref.at[pl.ds(j, 1)], sem)

    def start(gi, c):
        for u in range(DMA_UNROLL):
            row_copy(gi * DMA_UNROLL + u).start(priority=u % 2)
        return c

    def wait(gi, c):
        for u in range(DMA_UNROLL):
            row_copy(gi * DMA_UNROLL + u).wait()
        return c

    lax.fori_loop(0, n_groups, start, 0)
    lax.fori_loop(0, n_groups, wait, 0)
    o_ref[...] = rows_ref[...].astype(o_ref.dtype)


def gather_rows(h, src, step_rows):
    n = src.shape[0]
    d = h.shape[1]
    grid_spec = pltpu.PrefetchScalarGridSpec(
        num_scalar_prefetch=2, grid=(n // GATHER_ROWS,),
        in_specs=[pl.BlockSpec(memory_space=pl.ANY)],
        out_specs=pl.BlockSpec((GATHER_ROWS, d), lambda i, s, nv: (i, 0)),
        scratch_shapes=[pltpu.VMEM((GATHER_ROWS, d), h.dtype), pltpu.SemaphoreType.DMA(())])
    return pl.pallas_call(
        _gather_rows_kernel,
        out_shape=jax.ShapeDtypeStruct((n, d), BF16),
        grid_spec=grid_spec,
        compiler_params=_params("arbitrary"),
        name="moe_gather",
    )(src, step_rows, h)


def _moe_kernel(be_ref, nused_ref, nquarter_ref, x_ref, w1_ref, w3_ref, w2_ref, o_ref):
    i = pl.program_id(0)
    f = pl.program_id(1)

    @pl.when(f == 0)
    def _():
        o_ref[...] = jnp.zeros_like(o_ref)

    def mlp(rows):
        x = x_ref[0:rows, :]
        h1 = jnp.dot(x, w1_ref[...].astype(BF16), preferred_element_type=F32)
        h3 = jnp.dot(x, w3_ref[...].astype(BF16), preferred_element_type=F32)
        g = _silu(h1) * h3
        o_ref[0:rows, :] += jnp.dot(g.astype(BF16), w2_ref[...].astype(BF16), preferred_element_type=F32)

    for nq in range(1, MOE_QUARTERS + 1):
        @pl.when(nquarter_ref[i] == nq)
        def _(nq=nq):
            mlp(nq * (MOE_ROW_TILE // MOE_QUARTERS))


def moe_experts(buf, block_expert, n_used, block_quarters, w1, w3, w2):
    r, d = buf.shape
    tm, tf = MOE_ROW_TILE, MOE_FF_TILE
    nf = D_FF_EXPERT // tf

    def f_idx(i, f, nu):
        return jnp.where(i < nu[0], f, nf - 1)

    grid_spec = pltpu.PrefetchScalarGridSpec(
        num_scalar_prefetch=3,
        grid=(r // tm, nf),
        in_specs=[
            pl.BlockSpec((tm, d), lambda i, f, be, nu, nh: (jnp.minimum(i, nu[0] - 1), 0)),
            pl.BlockSpec((None, d, tf), lambda i, f, be, nu, nh: (be[i], 0, f_idx(i, f, nu))),
            pl.BlockSpec((None, d, tf), lambda i, f, be, nu, nh: (be[i], 0, f_idx(i, f, nu))),
            pl.BlockSpec((None, tf, d), lambda i, f, be, nu, nh: (be[i], f_idx(i, f, nu), 0)),
        ],
        out_specs=pl.BlockSpec((tm, d), lambda i, f, be, nu, nh: (i, 0)),
    )
    return pl.pallas_call(
        _moe_kernel,
        out_shape=jax.ShapeDtypeStruct((r, d), F32),
        grid_spec=grid_spec,
        compiler_params=_params("arbitrary", "arbitrary"),
        name="moe_experts",
    )(block_expert, n_used, block_quarters, buf, w1, w3, w2)


def _combine_kernel(dest_ref, y_hbm, x_ref, route_ref, gate_ref, gfin_ref, o_ref, ybuf_ref, sem):
    base = pl.program_id(0) * GATHER_ROWS

    def row_copy(t, k):
        return pltpu.make_async_copy(y_hbm.at[pl.ds(dest_ref[TOP_K * (base + t) + k], 1)],
                                     ybuf_ref.at[k, pl.ds(t, 1)], sem)

    def start(gi, c):
        for u in range(DMA_UNROLL):
            for k in range(TOP_K):
                row_copy(gi * DMA_UNROLL + u, k).start(priority=k)
        return c

    def wait(gi, c):
        for u in range(DMA_UNROLL):
            for k in range(TOP_K):
                row_copy(gi * DMA_UNROLL + u, k).wait()
        return c

    lax.fori_loop(0, GATHER_ROWS // DMA_UNROLL, start, 0)
    lax.fori_loop(0, GATHER_ROWS // DMA_UNROLL, wait, 0)
    route = route_ref[...]
    f = ybuf_ref[0] * route[:, 2:3] + ybuf_ref[1] * route[:, 3:4]
    xn = x_ref[...] + gate_ref[...] * f
    o_ref[...] = _rms(xn) * gfin_ref[...]


def moe_combine_final(y, dest, x, route, mods, layer, final_norm):
    tiles_per_batch = SEQ // GATHER_ROWS
    grid_spec = pltpu.PrefetchScalarGridSpec(
        num_scalar_prefetch=1, grid=(ROWS_LAT // GATHER_ROWS,),
        in_specs=[pl.BlockSpec(memory_space=pl.ANY),
                  pl.BlockSpec((GATHER_ROWS, D_MODEL), lambda i, d: (i, 0)),
                  pl.BlockSpec((GATHER_ROWS, LANES), lambda i, d: (i, 0)),
                  pl.BlockSpec((None, None, None, 1, D_MODEL), lambda i, d: (layer, i // tiles_per_batch, 5, 0, 0)),
                  pl.BlockSpec((1, D_MODEL), lambda i, d: (0, 0))],
        out_specs=pl.BlockSpec((GATHER_ROWS, D_MODEL), lambda i, d: (i, 0)),
        scratch_shapes=[pltpu.VMEM((TOP_K, GATHER_ROWS, D_MODEL), F32), pltpu.SemaphoreType.DMA(())])
    return pl.pallas_call(
        _combine_kernel,
        out_shape=jax.ShapeDtypeStruct((ROWS_LAT, D_MODEL), F32),
        grid_spec=grid_spec,
        compiler_params=_params("arbitrary"),
        name="moe_combine_final",
    )(dest, y, x, route, mods, final_norm.reshape(1, D_MODEL))


def moe_dispatch(route):
    expert = route[:, :TOP_K].astype(jnp.int32).reshape(-1)
    onehot = (expert[:, None] == jnp.arange(N_EXPERTS)[None, :]).astype(jnp.int32)
    incl = jnp.cumsum(onehot, axis=0)
    rank = jnp.sum((incl - onehot) * onehot, axis=1)
    counts = incl[-1]
    tm = MOE_ROW_TILE
    padded = (counts + tm - 1) // tm * tm
    pad_end = jnp.cumsum(padded)
    dest = ((pad_end - padded)[expert] + rank).astype(jnp.int32)
    n_blocks = MOE_SLOTS // tm
    n_used = (pad_end[-1] // tm).astype(jnp.int32)
    blocks = jnp.arange(n_blocks)
    block_expert = jnp.minimum(jnp.searchsorted(pad_end, blocks * tm, side='right'), N_EXPERTS - 1).astype(jnp.int32)
    block_rows = jnp.clip(counts[block_expert] - (blocks * tm - (pad_end - padded)[block_expert]), 0, tm)
    block_rows = jnp.where(blocks < n_used, block_rows, 0)
    block_expert = jnp.where(blocks < n_used, block_expert, block_expert[jnp.maximum(n_used - 1, 0)])
    quarter = tm // MOE_QUARTERS
    block_quarters = ((block_rows + quarter - 1) // quarter).astype(jnp.int32)
    steps = jnp.arange(MOE_SLOTS // GATHER_ROWS)
    per_block = tm // GATHER_ROWS
    step_rows = jnp.clip(block_rows[steps // per_block] - (steps % per_block) * GATHER_ROWS, 0, GATHER_ROWS)
    token = jnp.repeat(jnp.arange(ROWS_LAT, dtype=jnp.int32), TOP_K)
    src = jnp.zeros((MOE_SLOTS,), jnp.int32).at[dest].set(token)
    return dest, src, step_rows.astype(jnp.int32), block_expert, n_used.reshape(1), block_quarters


def _rope_table(n):
    rows = n // GRID_W
    row = jnp.repeat(jnp.arange(rows, dtype=F32), GRID_W)
    col = jnp.tile(jnp.arange(GRID_W, dtype=F32), rows)
    axis_dim = ROPE_DIM // 2
    inv = ROPE_BASE ** (-jnp.arange(0, axis_dim, 2, dtype=F32) / axis_dim)
    ang_r, ang_c = row[:, None] * inv, col[:, None] * inv
    cr, sr, cc, sc = jnp.cos(ang_r), jnp.sin(ang_r), jnp.cos(ang_c), jnp.sin(ang_c)
    lat = jnp.concatenate([cr, cr, cc, cc, -sr, sr, -sc, sc], axis=1)
    ident = jnp.concatenate([jnp.ones((TM, ROPE_DIM), F32), jnp.zeros((TM, ROPE_DIM), F32)], axis=1)
    return jnp.concatenate([lat, ident], axis=0)


def _layer_weights(w_in, w_uq, w_ukv):
    o = np.cumsum((0,) + PROJ_SIZES)
    seg = lambda i: w_in[:, int(o[i]):int(o[i + 1])]
    w_main = jnp.concatenate([seg(4), seg(3), seg(6), seg(7), seg(0), seg(1)], axis=1).astype(BF16)
    kr_w = seg(2)
    w_small = jnp.concatenate([seg(5), kr_w, kr_w[:, ROPE_SWAP]], axis=1).astype(BF16)
    wq = w_uq.reshape(Q_LORA, N_HEADS, NOPE_DIM + ROPE_DIM)
    rope_w = wq[:, :, NOPE_DIM:]
    w_q = jnp.concatenate([wq[:, :, :NOPE_DIM], rope_w, rope_w[:, :, ROPE_SWAP]], axis=2)
    w_q = w_q.reshape(Q_LORA, N_HEADS * HEAD_Q).astype(BF16)
    wkv = w_ukv.reshape(KV_LORA, N_HEADS, NOPE_DIM + V_DIM)
    w_kv = jnp.concatenate([wkv[:, :, :NOPE_DIM].reshape(KV_LORA, -1), wkv[:, :, NOPE_DIM:].reshape(KV_LORA, -1)],
                           axis=1).astype(BF16)
    return w_main, w_small, w_q, w_kv


def kernel(x, c, ctx, c_ctx, norm_mix, norm_ffn, w_ada, b_ada, w_in, q_norm, w_uq, kv_norm, w_ukv,
           conv_w, conv_b, a_log, dt_bias, d_skip, ssm_norm, w_oa, w_ob, w_out, w1_dense, w3_dense,
           w2_dense, w_router, w1_moe, w3_moe, w2_moe, final_norm):
    tab = _rope_table(SEQ)
    cc = jnp.concatenate([c, c_ctx[None], jnp.zeros((8 - BATCH - 1, D_MODEL), F32)], axis=0)
    mods = adaln(cc, w_ada, b_ada)
    head_of = np.arange(D_INNER) // SSM_HEAD_DIM
    e1 = (np.arange(SSM_HEADS)[:, None] == head_of[None, :]).astype(np.float32)
    e3 = jnp.asarray(np.concatenate([e1, e1, e1], axis=0), dtype=BF16)
    xr = jnp.concatenate([x.reshape(ROWS_LAT, D_MODEL), ctx.reshape(ROWS_CTX, D_MODEL)], axis=0)
    out = None
    for l in range(DEPTH):
        last = l == DEPTH - 1
        rows = ROWS_LAT if last else ROWS
        w_main, w_small, w_q, w_kv = _layer_weights(w_in[l], w_uq[l], w_ukv[l])
        p_main, dt_raw, kr = norm_proj(xr, norm_mix[l], mods, l, w_main, w_small, tab)
        q = up_proj(p_main, COL_CQ, q_norm[l], w_q, tab)
        kv = up_proj(p_main, COL_CKV, kv_norm[l], w_kv, None)
        att = latent_attention(q, kv, kr)
        att_ctx = att if last else context_attention(q, kv, kr)
        u = ssd_mixer(p_main, dt_raw, conv_w[l], conv_b[l], dt_bias[l], a_log[l], d_skip[l], ssm_norm[l], e3)
        mixed = merge_branches(att, att_ctx, u, p_main, w_oa[l].astype(BF16), w_ob[l].astype(BF16), rows)
        if l % 2 == 0:
            xr, h2 = out_proj_residual(mixed, w_out[l].astype(BF16), xr, norm_ffn[l], mods, l, rows)
            g = ffn_up(h2, w1_dense[l // 2].astype(BF16), w3_dense[l // 2].astype(BF16))
            xr = ffn_down_residual(g, w2_dense[l // 2].astype(BF16), xr, mods, l)
        else:
            wr = w_router[l // 2]
            x_lat, h2, route = out_proj_residual(mixed, w_out[l].astype(BF16), xr, norm_ffn[l], mods, l, rows, wr)
            dest, src, step_rows, block_expert, n_used, block_quarters = moe_dispatch(route)
            buf = gather_rows(h2, src, step_rows)
            y = moe_experts(buf, block_expert, n_used, block_quarters, w1_moe[l // 2], w3_moe[l // 2], w2_moe[l // 2])
            out = moe_combine_final(y, dest, x_lat, route, mods, l, final_norm)
    return out.reshape(BATCH, SEQ, D_MODEL)
```

```python
import functools

import numpy as np
import jax
import jax.numpy as jnp
from jax import lax
from jax.experimental import pallas as pl
from jax.experimental.pallas import tpu as pltpu

D_MODEL = 2048
BATCH = 2
SEQ = 4096
DEPTH = 2
CTX_LEN = 256
GRID_W = 64
EPS = 1e-6

N_HEADS = D_MODEL // 128
Q_LORA = 512
KV_LORA = 512
NOPE_DIM = 128
ROPE_DIM = 64
V_DIM = 128
ROPE_BASE = 10000.0
SM_SCALE = (NOPE_DIM + ROPE_DIM) ** -0.5

D_INNER = 2 * D_MODEL
SSM_HEAD_DIM = 64
SSM_HEADS = D_INNER // SSM_HEAD_DIM
SSM_GROUPS = 8
D_STATE = 128
CONV_K = 5
CONV_DIM = D_INNER + 2 * SSM_GROUPS * D_STATE
CHUNK = 128

PROJ_SIZES = (Q_LORA, KV_LORA, ROPE_DIM, D_INNER, CONV_DIM, 2 * SSM_HEADS, D_MODEL, D_MODEL)

D_FF = 256 * ((8 * D_MODEL // 3 + 255) // 256)
N_EXPERTS = 8
TOP_K = 2
D_FF_EXPERT = 7 * D_MODEL // 2

F32 = jnp.float32
BF16 = jnp.bfloat16
HIGHEST = lax.Precision.HIGHEST

VMEM_LIMIT_BYTES = 56 * 1024 * 1024
LANES = 128
HALO_ROWS = 16
OFF_CENTRE_TAPS = (0, 1, 3, 4)
CONV_COL_TILE = 2048
LOG2E = 1.4426950408889634
QK_SCALE = SM_SCALE * LOG2E

ROWS_LAT = BATCH * SEQ
ROWS_CTX = BATCH * CTX_LEN
ROWS = ROWS_LAT + ROWS_CTX
TM = 512
PROJ_COL_TILE = 3072
TILES_PER_BATCH = SEQ // TM
N_LAT_TILES = ROWS_LAT // TM
CHUNKS_LAT = SEQ // CHUNK
CHUNKS_CTX = CTX_LEN // CHUNK
CHUNKS_SEQ = CHUNKS_LAT + CHUNKS_CTX

COL_XBC = 0
COL_Z = COL_XBC + CONV_DIM
COL_GA = COL_Z + D_INNER
COL_GB = COL_GA + D_MODEL
COL_CQ = COL_GB + D_MODEL
COL_CKV = COL_CQ + Q_LORA
N_MAIN = COL_CKV + KV_LORA
HEAD_Q = 2 * LANES

MOE_ROW_TILE = 1024
MOE_FF_TILE = 512
MOE_QUARTERS = 4
MOE_SLOTS = ROWS_LAT * TOP_K + N_EXPERTS * MOE_ROW_TILE
GATHER_ROWS = 256
DMA_UNROLL = 8
ATTN_Q_TILE = 2048
ATTN_KV_CHUNK = 256
ROPE_SWAP = np.concatenate([np.arange(16, 32), np.arange(0, 16), np.arange(48, 64), np.arange(32, 48)])


def _params(*sem):
    return pltpu.CompilerParams(dimension_semantics=sem, vmem_limit_bytes=VMEM_LIMIT_BYTES)


def _mod_row(i):
    return jnp.minimum(i // TILES_PER_BATCH, BATCH)


def _rope_row(i):
    return jnp.where(i < N_LAT_TILES, i % TILES_PER_BATCH, TILES_PER_BATCH)


def _mod_spec(layer, k):
    return pl.BlockSpec((None, None, None, 1, D_MODEL), lambda i, *_: (layer, _mod_row(i), k, 0, 0))


def _rms(x):
    return x * lax.rsqrt(jnp.mean(x * x, axis=-1, keepdims=True) + EPS)


def _silu(x):
    return x * jax.nn.sigmoid(x)


def _rope(x, tab):
    y = x * tab
    y = y + pltpu.roll(y, ROPE_DIM, axis=1)
    lane = lax.broadcasted_iota(jnp.int32, y.shape, 1)
    return jnp.where(lane < ROPE_DIM, y, 0.0)


def _adaln_kernel(c_ref, wa_ref, wb_ref, b_ref, o_ref):
    @pl.when(pl.program_id(1) == 0)
    def _():
        o_ref[...] = jnp.broadcast_to(b_ref[...], o_ref.shape)

    a = _silu(c_ref[...]).astype(BF16)
    half = wa_ref.shape[1]
    o_ref[:, :half] += jnp.dot(a, wa_ref[...].astype(BF16), preferred_element_type=F32)
    o_ref[:, half:] += jnp.dot(a, wb_ref[...].astype(BF16), preferred_element_type=F32)


def adaln(cc, w_ada, b_ada):
    tk = 256
    n = 6 * D_MODEL
    out = pl.pallas_call(
        _adaln_kernel,
        out_shape=jax.ShapeDtypeStruct((DEPTH, 8, n), F32),
        grid=(DEPTH, D_MODEL // tk),
        in_specs=[pl.BlockSpec((8, tk), lambda l, k: (0, k)),
                  pl.BlockSpec((None, tk, n // 2), lambda l, k: (l, k, 0)),
                  pl.BlockSpec((None, tk, n // 2), lambda l, k: (l, k, 1)),
                  pl.BlockSpec((None, 1, n), lambda l, k: (l, 0, 0))],
        out_specs=pl.BlockSpec((None, 8, n), lambda l, k: (l, 0, 0)),
        compiler_params=_params("parallel", "arbitrary"),
        name="adaln",
    )(cc, w_ada, w_ada, b_ada.reshape(DEPTH, 1, n))
    return out.reshape(DEPTH, 8, 6, 1, D_MODEL)


def _norm_proj_kernel(x_ref, g_ref, sh_ref, sc_ref, w_ref, ws_ref, tab_ref, o_ref, dt_ref, kr_ref, h_ref):
    @pl.when(pl.program_id(1) == 0)
    def _():
        y = _rms(x_ref[...]) * g_ref[...]
        h = (y * (1 + sc_ref[...]) + sh_ref[...]).astype(h_ref.dtype)
        h_ref[...] = h
        acc = jnp.dot(h, ws_ref[...], preferred_element_type=F32)
        dt_ref[...] = acc[:, :LANES]
        kr_ref[...] = _rope(acc[:, LANES:], tab_ref[...]).astype(kr_ref.dtype)

    o_ref[...] = jnp.dot(h_ref[...], w_ref[...], preferred_element_type=F32).astype(o_ref.dtype)


def norm_proj(x, g, mods, layer, w, w_small, tab):
    n = w.shape[1]
    tn = PROJ_COL_TILE
    return pl.pallas_call(
        _norm_proj_kernel,
        out_shape=(jax.ShapeDtypeStruct((ROWS, n), BF16),
                   jax.ShapeDtypeStruct((ROWS, LANES), F32), jax.ShapeDtypeStruct((ROWS, LANES), BF16)),
        grid=(ROWS // TM, n // tn),
        in_specs=[pl.BlockSpec((TM, D_MODEL), lambda i, j: (i, 0)),
                  pl.BlockSpec((1, D_MODEL), lambda i, j: (0, 0)),
                  _mod_spec(layer, 0), _mod_spec(layer, 1),
                  pl.BlockSpec((D_MODEL, tn), lambda i, j: (0, j)),
                  pl.BlockSpec((D_MODEL, 2 * LANES), lambda i, j: (0, 0)),
                  pl.BlockSpec((TM, LANES), lambda i, j: (_rope_row(i), 0))],
        out_specs=(pl.BlockSpec((TM, tn), lambda i, j: (i, j)),
                   pl.BlockSpec((TM, LANES), lambda i, j: (i, 0)),
                   pl.BlockSpec((TM, LANES), lambda i, j: (i, 0))),
        scratch_shapes=[pltpu.VMEM((TM, D_MODEL), BF16)],
        compiler_params=_params("parallel", "arbitrary"),
        name="norm_proj",
    )(x, g.reshape(1, D_MODEL), mods, mods, w, w_small, tab)


def _up_kernel(c_ref, g_ref, w_ref, *rest, rope, heads_per_tile):
    if rope:
        tab_ref, o_ref = rest
    else:
        (o_ref,) = rest
    c = _rms(c_ref[...].astype(F32)) * g_ref[...]
    acc = jnp.dot(c.astype(BF16), w_ref[...], preferred_element_type=F32)
    if not rope:
        o_ref[...] = acc.astype(o_ref.dtype)
        return
    tab = tab_ref[...]
    acc = acc * QK_SCALE
    for hh in range(heads_per_tile):
        base = hh * HEAD_Q
        o_ref[:, base:base + LANES] = acc[:, base:base + LANES].astype(o_ref.dtype)
        o_ref[:, base + LANES:base + HEAD_Q] = _rope(acc[:, base + LANES:base + HEAD_Q], tab).astype(o_ref.dtype)


def up_proj(p_main, col, g, w, tab):
    lora, n = w.shape
    tn = n
    rope = tab is not None
    in_specs = [pl.BlockSpec((TM, lora), lambda i, j: (i, col // lora)),
                pl.BlockSpec((1, lora), lambda i, j: (0, 0)),
                pl.BlockSpec((lora, tn), lambda i, j: (0, j))]
    args = [p_main, g.reshape(1, lora), w]
    if rope:
        in_specs.append(pl.BlockSpec((TM, LANES), lambda i, j: (_rope_row(i), 0)))
        args.append(tab)
    return pl.pallas_call(
        functools.partial(_up_kernel, rope=rope, heads_per_tile=tn // HEAD_Q),
        out_shape=jax.ShapeDtypeStruct((ROWS, n), BF16),
        grid=(ROWS // TM, n // tn),
        in_specs=in_specs,
        out_specs=pl.BlockSpec((TM, tn), lambda i, j: (i, j)),
        compiler_params=_params("parallel", "parallel"),
        name="up_proj_rope" if rope else "up_proj",
    )(*args)


def _softmax_chunks(q, kcat_ref, v_chunks):
    m = l = acc = None
    for start, size, v in v_chunks:
        k = kcat_ref[start:start + size, :]
        s = lax.dot_general(q, k, (((1,), (1,)), ((), ())), preferred_element_type=F32)
        m_cur = jnp.max(s, axis=-1, keepdims=True)
        if m is None:
            m_new = m_cur
            p = jnp.exp2(s - m_new)
            l = jnp.sum(p, axis=-1, keepdims=True)
            acc = jnp.dot(p.astype(BF16), v, preferred_element_type=F32)
        else:
            m_new = jnp.maximum(m, m_cur)
            alpha = jnp.exp2(m - m_new)
            p = jnp.exp2(s - m_new)
            l = alpha * l + jnp.sum(p, axis=-1, keepdims=True)
            acc = alpha * acc + jnp.dot(p.astype(BF16), v, preferred_element_type=F32)
        m = m_new
    return acc / l


def _lat_attn_kernel(q_ref, knl_ref, krl_ref, vl_ref, knc_ref, krc_ref, vc_ref, o_ref, kcat_ref):
    @pl.when(pl.program_id(2) == 0)
    def _():
        kcat_ref[0:CTX_LEN, 0:LANES] = knc_ref[...]
        kcat_ref[0:CTX_LEN, LANES:HEAD_Q] = krc_ref[...]
        kcat_ref[CTX_LEN:, 0:LANES] = knl_ref[...]
        kcat_ref[CTX_LEN:, LANES:HEAD_Q] = krl_ref[...]

    chunks = [(0, CTX_LEN, vc_ref[...])]
    for s in range(0, SEQ, ATTN_KV_CHUNK):
        chunks.append((CTX_LEN + s, ATTN_KV_CHUNK, vl_ref[s:s + ATTN_KV_CHUNK, :]))
    o_ref[...] = _softmax_chunks(q_ref[...], kcat_ref, chunks).astype(o_ref.dtype)


def latent_attention(q, kv, kr):
    tq = ATTN_Q_TILE
    qt = SEQ // tq
    ctx_blk = ROWS_LAT // CTX_LEN
    return pl.pallas_call(
        _lat_attn_kernel,
        out_shape=jax.ShapeDtypeStruct((ROWS_LAT, D_MODEL), BF16),
        grid=(BATCH, N_HEADS, qt),
        in_specs=[pl.BlockSpec((tq, HEAD_Q), lambda b, h, i: (b * qt + i, h)),
                  pl.BlockSpec((SEQ, LANES), lambda b, h, i: (b, h)),
                  pl.BlockSpec((SEQ, LANES), lambda b, h, i: (b, 0)),
                  pl.BlockSpec((SEQ, LANES), lambda b, h, i: (b, N_HEADS + h)),
                  pl.BlockSpec((CTX_LEN, LANES), lambda b, h, i: (ctx_blk + b, h)),
                  pl.BlockSpec((CTX_LEN, LANES), lambda b, h, i: (ctx_blk + b, 0)),
                  pl.BlockSpec((CTX_LEN, LANES), lambda b, h, i: (ctx_blk + b, N_HEADS + h))],
        out_specs=pl.BlockSpec((tq, LANES), lambda b, h, i: (b * qt + i, h)),
        scratch_shapes=[pltpu.VMEM((CTX_LEN + SEQ, HEAD_Q), BF16)],
        compiler_params=_params("parallel", "parallel", "arbitrary"),
        name="latent_attention",
    )(q, kv, kr, kv, kv, kr, kv)


def _ctx_attn_kernel(q_ref, kn_ref, kr_ref, v_ref, o_ref, kcat_ref):
    kcat_ref[:, 0:LANES] = kn_ref[...]
    kcat_ref[:, LANES:HEAD_Q] = kr_ref[...]
    o_ref[...] = _softmax_chunks(q_ref[...], kcat_ref, [(0, CTX_LEN, v_ref[...])]).astype(o_ref.dtype)


def context_attention(q, kv, kr):
    ctx_blk = ROWS_LAT // CTX_LEN
    return pl.pallas_call(
        _ctx_attn_kernel,
        out_shape=jax.ShapeDtypeStruct((ROWS_CTX, D_MODEL), BF16),
        grid=(BATCH, N_HEADS),
        in_specs=[pl.BlockSpec((CTX_LEN, HEAD_Q), lambda b, h: (ctx_blk + b, h)),
                  pl.BlockSpec((CTX_LEN, LANES), lambda b, h: (ctx_blk + b, h)),
                  pl.BlockSpec((CTX_LEN, LANES), lambda b, h: (ctx_blk + b, 0)),
                  pl.BlockSpec((CTX_LEN, LANES), lambda b, h: (ctx_blk + b, N_HEADS + h))],
        out_specs=pl.BlockSpec((CTX_LEN, LANES), lambda b, h: (b, h)),
        scratch_shapes=[pltpu.VMEM((CTX_LEN, HEAD_Q), BF16)],
        compiler_params=_params("parallel", "parallel"),
        name="context_attention",
    )(q, kv, kr, kv)


def _chunk_block(b, s, reverse):
    if reverse:
        return jnp.where(s < CHUNKS_CTX, ROWS_LAT // CHUNK + CHUNKS_CTX * b + (CHUNKS_CTX - 1 - s),
                         CHUNKS_LAT * b + (CHUNKS_SEQ - 1 - s))
    return jnp.where(s < CHUNKS_CTX, ROWS_LAT // CHUNK + CHUNKS_CTX * b + s,
                     CHUNKS_LAT * b + (s - CHUNKS_CTX))


def _expand_heads(v, e3_ref):
    hi = v.astype(BF16)
    r1 = v - hi.astype(F32)
    mid = r1.astype(BF16)
    lo = (r1 - mid.astype(F32)).astype(BF16)
    return jnp.dot(jnp.concatenate([hi, mid, lo], axis=1), e3_ref[...], preferred_element_type=F32)


def _ssd_chunk(xcol, dt_raw, dtb_ref, a_ref, e3_ref, state_ref, reverse):
    off = SSM_HEADS if reverse else 0
    heads = slice(off, off + SSM_HEADS)
    row = lax.broadcasted_iota(jnp.int32, (CHUNK, CHUNK), 0)
    colm = lax.broadcasted_iota(jnp.int32, (CHUNK, CHUNK), 1)
    keep = (colm >= row) if reverse else (colm <= row)
    tri = keep.astype(F32)
    dt = jax.nn.softplus(dt_raw + dtb_ref[...])
    a = dt * a_ref[...]
    cum = jnp.dot(tri, a, precision=HIGHEST, preferred_element_type=F32)
    cum2 = cum * LOG2E
    cum2_t = cum2.T
    last_row = 0 if reverse else CHUNK - 1
    dec_h = jnp.exp2(cum2)
    end_h = dt * jnp.exp2(cum2[last_row:last_row + 1, :] - cum2)
    dt_full = _expand_heads(dt[:, heads], e3_ref)
    end_full = _expand_heads(end_h[:, heads], e3_ref)
    dec_in = _expand_heads(dec_h[:, heads], e3_ref)
    chunk_dec = dec_in[last_row:last_row + 1, :]
    xs = xcol(0, D_INNER)
    xcb = (xs * dt_full).astype(BF16)
    xce = (xs * end_full).astype(BF16)
    gw = SSM_HEADS // SSM_GROUPS * SSM_HEAD_DIM
    lane = lax.broadcasted_iota(jnp.int32, (CHUNK, LANES), 1)
    ys = []
    for g in range(SSM_GROUPS):
        b32 = xcol(D_INNER + g * D_STATE, D_INNER + (g + 1) * D_STATE)
        bg = b32.astype(BF16)
        cg = xcol(D_INNER + (SSM_GROUPS + g) * D_STATE, D_INNER + (SSM_GROUPS + g + 1) * D_STATE).astype(BF16)
        cb = lax.dot_general(cg, bg, (((1,), (1,)), ((), ())), preferred_element_type=F32)
        st = state_ref[g]
        y_off = jnp.dot(cg, st.astype(BF16), preferred_element_type=F32) * dec_in[:, g * gw:(g + 1) * gw]
        upd = jnp.dot(b32.T.astype(BF16), xce[:, g * gw:(g + 1) * gw],
                      preferred_element_type=F32)
        state_ref[g] = st * chunk_dec[:, g * gw:(g + 1) * gw] + upd
        pairs = []
        for j in range(gw // LANES):
            h0 = off + g * (SSM_HEADS // SSM_GROUPS) + 2 * j
            ms = []
            for hh in (h0, h0 + 1):
                seg = jnp.where(keep, cum2[:, hh:hh + 1] - cum2_t[hh:hh + 1, :], -jnp.inf)
                ms.append((jnp.exp2(seg) * cb).astype(BF16))
            xp = xcb[:, g * gw + j * LANES:g * gw + (j + 1) * LANES]
            rhs = jnp.concatenate([jnp.where(lane < SSM_HEAD_DIM, xp, 0), jnp.where(lane >= SSM_HEAD_DIM, xp, 0)],
                                  axis=0)
            pairs.append(jnp.dot(jnp.concatenate(ms, axis=1), rhs, preferred_element_type=F32))
        ys.append(jnp.concatenate(pairs, axis=1) + y_off)
    return jnp.concatenate(ys, axis=1)


def _ssd_fwd_kernel(prev_ref, cur_ref, next_ref, dt_ref, shift_ref, cw_ref, cbias_ref, dtb_ref, a_ref, e3_ref,
                    y_ref, xbc_ref, state_ref, xs_ref):
    s = pl.program_id(1)

    @pl.when(s == 0)
    def _():
        state_ref[...] = jnp.zeros_like(state_ref)

    first = (s == 0) | (s == CHUNKS_CTX)
    final = (s == CHUNKS_CTX - 1) | (s == CHUNKS_SEQ - 1)
    halo_zeros = jnp.zeros(prev_ref.shape, prev_ref.dtype)
    prev = jnp.where(first, halo_zeros, prev_ref[...])
    nxt = jnp.where(final, halo_zeros, next_ref[...])
    ext = jnp.concatenate([prev, cur_ref[...], nxt], axis=0)
    for c0 in range(0, CONV_DIM, CONV_COL_TILE):
        cs = slice(c0, c0 + CONV_COL_TILE)
        shifted = jnp.dot(shift_ref[...], ext[:, cs], preferred_element_type=F32)
        acc = cbias_ref[:, cs] + cw_ref[CONV_K // 2:CONV_K // 2 + 1, cs] * cur_ref[:, cs].astype(F32)
        for j, k in enumerate(OFF_CENTRE_TAPS):
            acc = acc + cw_ref[k:k + 1, cs] * shifted[j * CHUNK:(j + 1) * CHUNK, :]
        xs = _silu(acc)
        xs_ref[:, cs] = xs
        xbc_ref[:, cs] = xs.astype(xbc_ref.dtype)
    y_ref[...] = _ssd_chunk(lambda a, b: xs_ref[:, a:b], dt_ref[...], dtb_ref, a_ref, e3_ref, state_ref,
                            reverse=False)


def _ssd_bwd_kernel(xbc_ref, dt_ref, yf_ref, z0_ref, z1_ref, dtb_ref, a_ref, e3_ref, dsum_ref, nw_ref,
                    u_ref, state_ref):
    @pl.when(pl.program_id(1) == 0)
    def _():
        state_ref[...] = jnp.zeros_like(state_ref)

    xcol = lambda a, b: xbc_ref[:, a:b].astype(F32)
    y = _ssd_chunk(xcol, dt_ref[...], dtb_ref, a_ref, e3_ref, state_ref, reverse=True)
    y = y + yf_ref[...] + dsum_ref[...] * xcol(0, D_INNER)
    z = jnp.concatenate([z0_ref[...], z1_ref[...]], axis=1).astype(F32)
    u = y * _silu(z)
    gw = D_INNER // SSM_GROUPS
    for g in range(SSM_GROUPS):
        ug = u[:, g * gw:(g + 1) * gw]
        u_ref[:, g * gw:(g + 1) * gw] = (_rms(ug) * nw_ref[:, g * gw:(g + 1) * gw]).astype(u_ref.dtype)


def ssd_mixer(p_main, dt_raw, conv_w, conv_b, dt_bias, a_log, d_skip, ssm_norm, e3):
    zb = COL_Z // (D_INNER // 2)
    dtb = dt_bias.reshape(1, 2 * SSM_HEADS).astype(F32)
    a_neg = (-jnp.exp(a_log.astype(F32))).reshape(1, 2 * SSM_HEADS)
    dsum = jnp.repeat(d_skip[0] + d_skip[1], SSM_HEAD_DIM).reshape(1, D_INNER)
    halo_per_chunk = CHUNK // HALO_ROWS
    n_halo = ROWS // HALO_ROWS
    shift_np = np.zeros((len(OFF_CENTRE_TAPS) * CHUNK, CHUNK + 2 * HALO_ROWS), np.float32)
    for j, k in enumerate(OFF_CENTRE_TAPS):
        shift_np[j * CHUNK + np.arange(CHUNK), HALO_ROWS + np.arange(CHUNK) + k - CONV_K // 2] = 1.0
    shift = jnp.asarray(shift_np, dtype=BF16)
    full = lambda shape: pl.BlockSpec(shape, lambda b, s: tuple(0 for _ in shape))

    def blk(reverse):
        return lambda b, s: (_chunk_block(b, s, reverse), 0)

    fwd_blk = blk(False)
    y_f, xbc = pl.pallas_call(
        _ssd_fwd_kernel,
        out_shape=(jax.ShapeDtypeStruct((ROWS, D_INNER), F32), jax.ShapeDtypeStruct((ROWS, CONV_DIM), BF16)),
        grid=(BATCH, CHUNKS_SEQ),
        in_specs=[
            pl.BlockSpec((HALO_ROWS, CONV_DIM),
                         lambda b, s: (jnp.maximum(_chunk_block(b, s, False) * halo_per_chunk - 1, 0), 0)),
            pl.BlockSpec((CHUNK, CONV_DIM), fwd_blk),
            pl.BlockSpec((HALO_ROWS, CONV_DIM),
                         lambda b, s: (jnp.minimum((_chunk_block(b, s, False) + 1) * halo_per_chunk, n_halo - 1), 0)),
            pl.BlockSpec((CHUNK, LANES), fwd_blk),
            full(shift.shape),
            full((CONV_K, CONV_DIM)), full((1, CONV_DIM)), full((1, LANES)), full((1, LANES)),
            full((3 * SSM_HEADS, D_INNER)),
        ],
        out_specs=(pl.BlockSpec((CHUNK, D_INNER), fwd_blk), pl.BlockSpec((CHUNK, CONV_DIM), fwd_blk)),
        scratch_shapes=[pltpu.VMEM((SSM_GROUPS, D_STATE, D_INNER // SSM_GROUPS), F32),
                        pltpu.VMEM((CHUNK, CONV_DIM), F32)],
        compiler_params=_params("parallel", "arbitrary"),
        name="ssd_forward",
    )(p_main, p_main, p_main, dt_raw, shift, conv_w, conv_b.reshape(1, CONV_DIM), dtb, a_neg, e3)
    bwd_blk = blk(True)
    return pl.pallas_call(
        _ssd_bwd_kernel,
        out_shape=jax.ShapeDtypeStruct((ROWS, D_INNER), BF16),
        grid=(BATCH, CHUNKS_SEQ),
        in_specs=[
            pl.BlockSpec((CHUNK, CONV_DIM), bwd_blk),
            pl.BlockSpec((CHUNK, LANES), bwd_blk),
            pl.BlockSpec((CHUNK, D_INNER), bwd_blk),
            pl.BlockSpec((CHUNK, D_INNER // 2), lambda b, s: (_chunk_block(b, s, True), zb)),
            pl.BlockSpec((CHUNK, D_INNER // 2), lambda b, s: (_chunk_block(b, s, True), zb + 1)),
            full((1, LANES)), full((1, LANES)), full((3 * SSM_HEADS, D_INNER)),
            full((1, D_INNER)), full((1, D_INNER)),
        ],
        out_specs=pl.BlockSpec((CHUNK, D_INNER), bwd_blk),
        scratch_shapes=[pltpu.VMEM((SSM_GROUPS, D_STATE, D_INNER // SSM_GROUPS), F32)],
        compiler_params=_params("parallel", "arbitrary"),
        name="ssd_backward",
    )(xbc, dt_raw, y_f, p_main, p_main, dtb, a_neg, e3, dsum, ssm_norm.reshape(1, D_INNER))


def _merge_kernel(attl_ref, attc_ref, u_ref, woa_ref, wob_ref, ga_ref, gb_ref, o_ref):
    att = jnp.where(pl.program_id(0) < N_LAT_TILES, attl_ref[...], attc_ref[...])
    o_a = jnp.dot(att, woa_ref[...], preferred_element_type=F32)
    o_b = jnp.dot(u_ref[...], wob_ref[...], preferred_element_type=F32)
    o_ref[...] = (jax.nn.sigmoid(ga_ref[...].astype(F32)) * o_a
                  + jax.nn.sigmoid(gb_ref[...].astype(F32)) * o_b).astype(o_ref.dtype)


def merge_branches(att_lat, att_ctx, u, p_main, w_oa, w_ob, rows):
    tn = 1024
    return pl.pallas_call(
        _merge_kernel,
        out_shape=jax.ShapeDtypeStruct((rows, D_MODEL), BF16),
        grid=(rows // TM, D_MODEL // tn),
        in_specs=[pl.BlockSpec((TM, D_MODEL), lambda i, j: (jnp.minimum(i, N_LAT_TILES - 1), 0)),
                  pl.BlockSpec((TM, D_MODEL), lambda i, j: (0, 0)),
                  pl.BlockSpec((TM, D_INNER), lambda i, j: (i, 0)),
                  pl.BlockSpec((D_MODEL, tn), lambda i, j: (0, j)),
                  pl.BlockSpec((D_INNER, tn), lambda i, j: (0, j)),
                  pl.BlockSpec((TM, tn), lambda i, j: (i, COL_GA // tn + j)),
                  pl.BlockSpec((TM, tn), lambda i, j: (i, COL_GB // tn + j))],
        out_specs=pl.BlockSpec((TM, tn), lambda i, j: (i, j)),
        compiler_params=_params("parallel", "parallel"),
        name="merge_branches",
    )(att_lat, att_ctx, u, w_oa, w_ob, p_main, p_main)


def _route(h, wr_ref):
    lane = lax.broadcasted_iota(jnp.int32, (h.shape[0], LANES), 1)
    lg = jnp.full((h.shape[0], LANES), -jnp.inf, F32)
    for e in range(N_EXPERTS):
        lg = jnp.where(lane == e, jnp.sum(h * wr_ref[e:e + 1, :], axis=-1, keepdims=True), lg)
    v1 = jnp.max(lg, axis=-1, keepdims=True)
    i1 = jnp.min(jnp.where(lg == v1, lane, LANES), axis=-1, keepdims=True)
    lg2 = jnp.where(lane == i1, -jnp.inf, lg)
    v2 = jnp.max(lg2, axis=-1, keepdims=True)
    i2 = jnp.min(jnp.where(lg2 == v2, lane, LANES), axis=-1, keepdims=True)
    e = jnp.exp(v2 - v1)
    g1 = 1.0 / (1.0 + e)
    g2 = e / (1.0 + e)
    return jnp.where(lane == 0, i1.astype(F32),
                     jnp.where(lane == 1, i2.astype(F32),
                               jnp.where(lane == 2, g1, jnp.where(lane == 3, g2, 0.0))))


def _out_res_kernel(m_ref, w_ref, x_ref, gate_ref, g2_ref, sh_ref, sc_ref, *rest, route):
    if route:
        wr_ref, xo_ref, ho_ref, ro_ref = rest
    else:
        xo_ref, ho_ref = rest
    acc = jnp.dot(m_ref[...], w_ref[...], preferred_element_type=F32)
    xn = x_ref[...] + gate_ref[...] * acc
    xo_ref[...] = xn
    h = (_rms(xn) * g2_ref[...]) * (1 + sc_ref[...]) + sh_ref[...]
    ho_ref[...] = h.astype(ho_ref.dtype)
    if route:
        ro_ref[...] = _route(h, wr_ref)


def out_proj_residual(mixed, w_out, x, g2, mods, layer, rows, w_router=None):
    route = w_router is not None
    h_dtype = F32 if route else BF16
    in_specs = [pl.BlockSpec((TM, D_MODEL), lambda i: (i, 0)),
                pl.BlockSpec((D_MODEL, D_MODEL), lambda i: (0, 0)),
                pl.BlockSpec((TM, D_MODEL), lambda i: (i, 0)),
                _mod_spec(layer, 2), pl.BlockSpec((1, D_MODEL), lambda i: (0, 0)),
                _mod_spec(layer, 3), _mod_spec(layer, 4)]
    args = [mixed, w_out, x, mods, g2.reshape(1, D_MODEL), mods, mods]
    out_shape = [jax.ShapeDtypeStruct((rows, D_MODEL), F32), jax.ShapeDtypeStruct((rows, D_MODEL), h_dtype)]
    out_specs = [pl.BlockSpec((TM, D_MODEL), lambda i: (i, 0)), pl.BlockSpec((TM, D_MODEL), lambda i: (i, 0))]
    if route:
        in_specs.append(pl.BlockSpec((N_EXPERTS, D_MODEL), lambda i: (0, 0)))
        args.append(w_router.T)
        out_shape.append(jax.ShapeDtypeStruct((rows, LANES), F32))
        out_specs.append(pl.BlockSpec((TM, LANES), lambda i: (i, 0)))
    return pl.pallas_call(
        functools.partial(_out_res_kernel, route=route),
        out_shape=tuple(out_shape),
        grid=(rows // TM,),
        in_specs=in_specs,
        out_specs=tuple(out_specs),
        compiler_params=_params("parallel"),
        name="out_proj_residual",
    )(*args)


def _ffn_up_kernel(h_ref, w1_ref, w3_ref, o_ref):
    h = h_ref[...]
    a = jnp.dot(h, w1_ref[...], preferred_element_type=F32)
    b = jnp.dot(h, w3_ref[...], preferred_element_type=F32)
    o_ref[...] = (_silu(a) * b).astype(o_ref.dtype)


def ffn_up(h, w1, w3):
    tn = D_FF // 4
    return pl.pallas_call(
        _ffn_up_kernel,
        out_shape=jax.ShapeDtypeStruct((ROWS, D_FF), BF16),
        grid=(ROWS // TM, D_FF // tn),
        in_specs=[pl.BlockSpec((TM, D_MODEL), lambda i, j: (i, 0)),
                  pl.BlockSpec((D_MODEL, tn), lambda i, j: (0, j)),
                  pl.BlockSpec((D_MODEL, tn), lambda i, j: (0, j))],
        out_specs=pl.BlockSpec((TM, tn), lambda i, j: (i, j)),
        compiler_params=_params("parallel", "parallel"),
        name="ffn_up",
    )(h, w1, w3)


def _ffn_down_kernel(g_ref, w_ref, x_ref, gate_ref, o_ref):
    acc = jnp.dot(g_ref[...], w_ref[...], preferred_element_type=F32)
    o_ref[...] = x_ref[...] + gate_ref[...] * acc


def ffn_down_residual(g, w2, x, mods, layer):
    tn = 1024
    gate_spec = pl.BlockSpec((None, None, None, 1, tn), lambda i, j: (layer, _mod_row(i), 5, 0, j))
    return pl.pallas_call(
        _ffn_down_kernel,
        out_shape=jax.ShapeDtypeStruct((ROWS, D_MODEL), F32),
        grid=(ROWS // TM, D_MODEL // tn),
        in_specs=[pl.BlockSpec((TM, D_FF), lambda i, j: (i, 0)),
                  pl.BlockSpec((D_FF, tn), lambda i, j: (0, j)),
                  pl.BlockSpec((TM, tn), lambda i, j: (i, j)),
                  gate_spec],
        out_specs=pl.BlockSpec((TM, tn), lambda i, j: (i, j)),
        compiler_params=_params("parallel", "parallel"),
        name="ffn_down_residual",
    )(g, w2, x, mods)


def _gather_rows_kernel(src_ref, nvalid_ref, h_hbm, o_ref, rows_ref, sem):
    step = pl.program_id(0)
    base = step * GATHER_ROWS

    @pl.when(step == 0)
    def _():
        rows_ref[...] = jnp.zeros_like(rows_ref)

    n_groups = lax.shift_right_logical(nvalid_ref[step] + (DMA_UNROLL - 1), DMA_UNROLL.bit_length() - 1)

    def row_copy(j):
        return pltpu.make_async_copy(h_hbm.at[pl.ds(src_ref[base + j], 1)], rows_ref.at[pl.ds(j, 1)], sem)

    def start(gi, c):
        for u in range(DMA_UNROLL):
            row_copy(gi * DMA_UNROLL + u).start(priority=u % 2)
        return c

    def wait(gi, c):
        for u in range(DMA_UNROLL):
            row_copy(gi * DMA_UNROLL + u).wait()
        return c

    lax.fori_loop(0, n_groups, start, 0)
    lax.fori_loop(0, n_groups, wait, 0)
    o_ref[...] = rows_ref[...].astype(o_ref.dtype)


def gather_rows(h, src, step_rows):
    n = src.shape[0]
    d = h.shape[1]
    grid_spec = pltpu.PrefetchScalarGridSpec(
        num_scalar_prefetch=2, grid=(n // GATHER_ROWS,),
        in_specs=[pl.BlockSpec(memory_space=pl.ANY)],
        out_specs=pl.BlockSpec((GATHER_ROWS, d), lambda i, s, nv: (i, 0)),
        scratch_shapes=[pltpu.VMEM((GATHER_ROWS, d), h.dtype), pltpu.SemaphoreType.DMA(())])
    return pl.pallas_call(
        _gather_rows_kernel,
        out_shape=jax.ShapeDtypeStruct((n, d), BF16),
        grid_spec=grid_spec,
        compiler_params=_params("arbitrary"),
        name="moe_gather",
    )(src, step_rows, h)


def _moe_kernel(be_ref, nused_ref, nquarter_ref, x_ref, w1_ref, w3_ref, w2_ref, o_ref):
    i = pl.program_id(0)
    f = pl.program_id(1)

    @pl.when(f == 0)
    def _():
        o_ref[...] = jnp.zeros_like(o_ref)

    def mlp(rows):
        x = x_ref[0:rows, :]
        h1 = jnp.dot(x, w1_ref[...].astype(BF16), preferred_element_type=F32)
        h3 = jnp.dot(x, w3_ref[...].astype(BF16), preferred_element_type=F32)
        g = _silu(h1) * h3
        o_ref[0:rows, :] += jnp.dot(g.astype(BF16), w2_ref[...].astype(BF16), preferred_element_type=F32)

    for nq in range(1, MOE_QUARTERS + 1):
        @pl.when(nquarter_ref[i] == nq)
        def _(nq=nq):
            mlp(nq * (MOE_ROW_TILE // MOE_QUARTERS))


def moe_experts(buf, block_expert, n_used, block_quarters, w1, w3, w2):
    r, d = buf.shape
    tm, tf = MOE_ROW_TILE, MOE_FF_TILE
    nf = D_FF_EXPERT // tf

    def f_idx(i, f, nu):
        return jnp.where(i < nu[0], f, nf - 1)

    grid_spec = pltpu.PrefetchScalarGridSpec(
        num_scalar_prefetch=3,
        grid=(r // tm, nf),
        in_specs=[
            pl.BlockSpec((tm, d), lambda i, f, be, nu, nh: (jnp.minimum(i, nu[0] - 1), 0)),
            pl.BlockSpec((None, d, tf), lambda i, f, be, nu, nh: (be[i], 0, f_idx(i, f, nu))),
            pl.BlockSpec((None, d, tf), lambda i, f, be, nu, nh: (be[i], 0, f_idx(i, f, nu))),
            pl.BlockSpec((None, tf, d), lambda i, f, be, nu, nh: (be[i], f_idx(i, f, nu), 0)),
        ],
        out_specs=pl.BlockSpec((tm, d), lambda i, f, be, nu, nh: (i, 0)),
    )
    return pl.pallas_call(
        _moe_kernel,
        out_shape=jax.ShapeDtypeStruct((r, d), F32),
        grid_spec=grid_spec,
        compiler_params=_params("arbitrary", "arbitrary"),
        name="moe_experts",
    )(block_expert, n_used, block_quarters, buf, w1, w3, w2)


def _combine_kernel(dest_ref, y_hbm, x_ref, route_ref, gate_ref, gfin_ref, o_ref, ybuf_ref, sem):
    base = pl.program_id(0) * GATHER_ROWS

    def row_copy(t, k):
        return pltpu.make_async_copy(y_hbm.at[pl.ds(dest_ref[TOP_K * (base + t) + k], 1)],
                                     ybuf_ref.at[k, pl.ds(t, 1)], sem)

    def start(gi, c):
        for u in range(DMA_UNROLL):
            for k in range(TOP_K):
                row_copy(gi * DMA_UNROLL + u, k).start(priority=k)
        return c

    def wait(gi, c):
        for u in range(DMA_UNROLL):
            for k in range(TOP_K):
                row_copy(gi * DMA_UNROLL + u, k).wait()
        return c

    lax.fori_loop(0, GATHER_ROWS // DMA_UNROLL, start, 0)
    lax.fori_loop(0, GATHER_ROWS // DMA_UNROLL, wait, 0)
    route = route_ref[...]
    f = ybuf_ref[0] * route[:, 2:3] + ybuf_ref[1] * route[:, 3:4]
    xn = x_ref[...] + gate_ref[...] * f
    o_ref[...] = _rms(xn) * gfin_ref[...]


def moe_combine_final(y, dest, x, route, mods, layer, final_norm):
    tiles_per_batch = SEQ // GATHER_ROWS
    grid_spec = pltpu.PrefetchScalarGridSpec(
        num_scalar_prefetch=1, grid=(ROWS_LAT // GATHER_ROWS,),
        in_specs=[pl.BlockSpec(memory_space=pl.ANY),
                  pl.BlockSpec((GATHER_ROWS, D_MODEL), lambda i, d: (i, 0)),
                  pl.BlockSpec((GATHER_ROWS, LANES), lambda i, d: (i, 0)),
                  pl.BlockSpec((None, None, None, 1, D_MODEL), lambda i, d: (layer, i // tiles_per_batch, 5, 0, 0)),
                  pl.BlockSpec((1, D_MODEL), lambda i, d: (0, 0))],
        out_specs=pl.BlockSpec((GATHER_ROWS, D_MODEL), lambda i, d: (i, 0)),
        scratch_shapes=[pltpu.VMEM((TOP_K, GATHER_ROWS, D_MODEL), F32), pltpu.SemaphoreType.DMA(())])
    return pl.pallas_call(
        _combine_kernel,
        out_shape=jax.ShapeDtypeStruct((ROWS_LAT, D_MODEL), F32),
        grid_spec=grid_spec,
        compiler_params=_params("arbitrary"),
        name="moe_combine_final",
    )(dest, y, x, route, mods, final_norm.reshape(1, D_MODEL))


def moe_dispatch(route):
    expert = route[:, :TOP_K].astype(jnp.int32).reshape(-1)
    onehot = (expert[:, None] == jnp.arange(N_EXPERTS)[None, :]).astype(jnp.int32)
    incl = jnp.cumsum(onehot, axis=0)
    rank = jnp.sum((incl - onehot) * onehot, axis=1)
    counts = incl[-1]
    tm = MOE_ROW_TILE
    padded = (counts + tm - 1) // tm * tm
    pad_end = jnp.cumsum(padded)
    dest = ((pad_end - padded)[expert] + rank).astype(jnp.int32)
    n_blocks = MOE_SLOTS // tm
    n_used = (pad_end[-1] // tm).astype(jnp.int32)
    blocks = jnp.arange(n_blocks)
    block_expert = jnp.minimum(jnp.searchsorted(pad_end, blocks * tm, side='right'), N_EXPERTS - 1).astype(jnp.int32)
    block_rows = jnp.clip(counts[block_expert] - (blocks * tm - (pad_end - padded)[block_expert]), 0, tm)
    block_rows = jnp.where(blocks < n_used, block_rows, 0)
    block_expert = jnp.where(blocks < n_used, block_expert, block_expert[jnp.maximum(n_used - 1, 0)])
    quarter = tm // MOE_QUARTERS
    block_quarters = ((block_rows + quarter - 1) // quarter).astype(jnp.int32)
    steps = jnp.arange(MOE_SLOTS // GATHER_ROWS)
    per_block = tm // GATHER_ROWS
    step_rows = jnp.clip(block_rows[steps // per_block] - (steps % per_block) * GATHER_ROWS, 0, GATHER_ROWS)
    token = jnp.repeat(jnp.arange(ROWS_LAT, dtype=jnp.int32), TOP_K)
    src = jnp.zeros((MOE_SLOTS,), jnp.int32).at[dest].set(token)
    return dest, src, step_rows.astype(jnp.int32), block_expert, n_used.reshape(1), block_quarters


def _rope_table(n):
    rows = n // GRID_W
    row = jnp.repeat(jnp.arange(rows, dtype=F32), GRID_W)
    col = jnp.tile(jnp.arange(GRID_W, dtype=F32), rows)
    axis_dim = ROPE_DIM // 2
    inv = ROPE_BASE ** (-jnp.arange(0, axis_dim, 2, dtype=F32) / axis_dim)
    ang_r, ang_c = row[:, None] * inv, col[:, None] * inv
    cr, sr, cc, sc = jnp.cos(ang_r), jnp.sin(ang_r), jnp.cos(ang_c), jnp.sin(ang_c)
    lat = jnp.concatenate([cr, cr, cc, cc, -sr, sr, -sc, sc], axis=1)
    ident = jnp.concatenate([jnp.ones((TM, ROPE_DIM), F32), jnp.zeros((TM, ROPE_DIM), F32)], axis=1)
    return jnp.concatenate([lat, ident], axis=0)


def _layer_weights(w_in, w_uq, w_ukv):
    o = np.cumsum((0,) + PROJ_SIZES)
    seg = lambda i: w_in[:, int(o[i]):int(o[i + 1])]
    w_main = jnp.concatenate([seg(4), seg(3), seg(6), seg(7), seg(0), seg(1)], axis=1).astype(BF16)
    kr_w = seg(2)
    w_small = jnp.concatenate([seg(5), kr_w, kr_w[:, ROPE_SWAP]], axis=1).astype(BF16)
    wq = w_uq.reshape(Q_LORA, N_HEADS, NOPE_DIM + ROPE_DIM)
    rope_w = wq[:, :, NOPE_DIM:]
    w_q = jnp.concatenate([wq[:, :, :NOPE_DIM], rope_w, rope_w[:, :, ROPE_SWAP]], axis=2)
    w_q = w_q.reshape(Q_LORA, N_HEADS * HEAD_Q).astype(BF16)
    wkv = w_ukv.reshape(KV_LORA, N_HEADS, NOPE_DIM + V_DIM)
    w_kv = jnp.concatenate([wkv[:, :, :NOPE_DIM].reshape(KV_LORA, -1), wkv[:, :, NOPE_DIM:].reshape(KV_LORA, -1)],
                           axis=1).astype(BF16)
    return w_main, w_small, w_q, w_kv


def kernel(x, c, ctx, c_ctx, norm_mix, norm_ffn, w_ada, b_ada, w_in, q_norm, w_uq, kv_norm, w_ukv,
           conv_w, conv_b, a_log, dt_bias, d_skip, ssm_norm, w_oa, w_ob, w_out, w1_dense, w3_dense,
           w2_dense, w_router, w1_moe, w3_moe, w2_moe, final_norm):
    tab = _rope_table(SEQ)
    cc = jnp.concatenate([c, c_ctx[None], jnp.zeros((8 - BATCH - 1, D_MODEL), F32)], axis=0)
    mods = adaln(cc, w_ada, b_ada)
    head_of = np.arange(D_INNER) // SSM_HEAD_DIM
    e1 = (np.arange(SSM_HEADS)[:, None] == head_of[None, :]).astype(np.float32)
    e3 = jnp.asarray(np.concatenate([e1, e1, e1], axis=0), dtype=BF16)
    xr = jnp.concatenate([x.reshape(ROWS_LAT, D_MODEL), ctx.reshape(ROWS_CTX, D_MODEL)], axis=0)
    out = None
    for l in range(DEPTH):
        last = l == DEPTH - 1
        rows = ROWS_LAT if last else ROWS
        w_main, w_small, w_q, w_kv = _layer_weights(w_in[l], w_uq[l], w_ukv[l])
        p_main, dt_raw, kr = norm_proj(xr, norm_mix[l], mods, l, w_main, w_small, tab)
        q = up_proj(p_main, COL_CQ, q_norm[l], w_q, tab)
        kv = up_proj(p_main, COL_CKV, kv_norm[l], w_kv, None)
        att = latent_attention(q, kv, kr)
        att_ctx = att if last else context_attention(q, kv, kr)
        u = ssd_mixer(p_main, dt_raw, conv_w[l], conv_b[l], dt_bias[l], a_log[l], d_skip[l], ssm_norm[l], e3)
        mixed = merge_branches(att, att_ctx, u, p_main, w_oa[l].astype(BF16), w_ob[l].astype(BF16), rows)
        if l % 2 == 0:
            xr, h2 = out_proj_residual(mixed, w_out[l].astype(BF16), xr, norm_ffn[l], mods, l, rows)
            g = ffn_up(h2, w1_dense[l // 2].astype(BF16), w3_dense[l // 2].astype(BF16))
            xr = ffn_down_residual(g, w2_dense[l // 2].astype(BF16), xr, mods, l)
        else:
            wr = w_router[l // 2]
            x_lat, h2, route = out_proj_residual(mixed, w_out[l].astype(BF16), xr, norm_ffn[l], mods, l, rows, wr)
            dest, src, step_rows, block_expert, n_used, block_quarters = moe_dispatch(route)
            buf = gather_rows(h2, src, step_rows)
            y = moe_experts(buf, block_expert, n_used, block_quarters, w1_moe[l // 2], w3_moe[l // 2], w2_moe[l // 2])
            out = moe_combine_final(y, dest, x_lat, route, mods, l, final_norm)
    return out.reshape(BATCH, SEQ, D_MODEL)
```

```python
import functools

import numpy as np
import jax
import jax.numpy as jnp
from jax import lax
from jax.experimental import pallas as pl
from jax.experimental.pallas import tpu as pltpu

D_MODEL = 2048
BATCH = 2
SEQ = 4096
DEPTH = 2
CTX_LEN = 256
GRID_W = 64
EPS = 1e-6

N_HEADS = D_MODEL // 128
Q_LORA = 512
KV_LORA = 512
NOPE_DIM = 128
ROPE_DIM = 64
V_DIM = 128
ROPE_BASE = 10000.0
SM_SCALE = (NOPE_DIM + ROPE_DIM) ** -0.5

D_INNER = 2 * D_MODEL
SSM_HEAD_DIM = 64
SSM_HEADS = D_INNER // SSM_HEAD_DIM
SSM_GROUPS = 8
D_STATE = 128
CONV_K = 5
CONV_DIM = D_INNER + 2 * SSM_GROUPS * D_STATE
CHUNK = 128

PROJ_SIZES = (Q_LORA, KV_LORA, ROPE_DIM, D_INNER, CONV_DIM, 2 * SSM_HEADS, D_MODEL, D_MODEL)

D_FF = 256 * ((8 * D_MODEL // 3 + 255) // 256)
N_EXPERTS = 8
TOP_K = 2
D_FF_EXPERT = 7 * D_MODEL // 2

F32 = jnp.float32
BF16 = jnp.bfloat16
HIGHEST = lax.Precision.HIGHEST

VMEM_LIMIT_BYTES = 56 * 1024 * 1024
LANES = 128
HALO_ROWS = 16
OFF_CENTRE_TAPS = (0, 1, 3, 4)
CONV_COL_TILE = 2048
LOG2E = 1.4426950408889634
QK_SCALE = SM_SCALE * LOG2E

ROWS_LAT = BATCH * SEQ
ROWS_CTX = BATCH * CTX_LEN
ROWS = ROWS_LAT + ROWS_CTX
TM = 512
PROJ_COL_TILE = 3072
TILES_PER_BATCH = SEQ // TM
N_LAT_TILES = ROWS_LAT // TM
CHUNKS_LAT = SEQ // CHUNK
CHUNKS_CTX = CTX_LEN // CHUNK
CHUNKS_SEQ = CHUNKS_LAT + CHUNKS_CTX

COL_XBC = 0
COL_Z = COL_XBC + CONV_DIM
COL_GA = COL_Z + D_INNER
COL_GB = COL_GA + D_MODEL
COL_CQ = COL_GB + D_MODEL
COL_CKV = COL_CQ + Q_LORA
N_MAIN = COL_CKV + KV_LORA
HEAD_Q = 2 * LANES

MOE_ROW_TILE = 1024
MOE_FF_TILE = 512
MOE_QUARTERS = 4
MOE_SLOTS = ROWS_LAT * TOP_K + N_EXPERTS * MOE_ROW_TILE
GATHER_ROWS = 256
ADALN_SLAB = 128
ADALN_BUFS = 4
DMA_UNROLL = 8
ATTN_Q_TILE = 2048
ATTN_KV_CHUNK = 256
ROPE_SWAP = np.concatenate([np.arange(16, 32), np.arange(0, 16), np.arange(48, 64), np.arange(32, 48)])


def _params(*sem):
    return pltpu.CompilerParams(dimension_semantics=sem, vmem_limit_bytes=VMEM_LIMIT_BYTES)


def _mod_row(i):
    return jnp.minimum(i // TILES_PER_BATCH, BATCH)


def _rope_row(i):
    return jnp.where(i < N_LAT_TILES, i % TILES_PER_BATCH, TILES_PER_BATCH)


def _mod_spec(layer, k):
    return pl.BlockSpec((None, None, None, 1, D_MODEL), lambda i, *_: (layer, _mod_row(i), k, 0, 0))


def _rms(x):
    return x * lax.rsqrt(jnp.mean(x * x, axis=-1, keepdims=True) + EPS)


def _silu(x):
    return x * jax.nn.sigmoid(x)


def _rope(x, tab):
    y = x * tab
    y = y + pltpu.roll(y, ROPE_DIM, axis=1)
    lane = lax.broadcasted_iota(jnp.int32, y.shape, 1)
    return jnp.where(lane < ROPE_DIM, y, 0.0)


def _adaln_kernel(c_ref, b_ref, w_hbm, o_ref, wbuf_ref, sem):
    layer = pl.program_id(0)
    n_slabs = D_MODEL // ADALN_SLAB

    def slab_copy(k):
        slot = k % ADALN_BUFS
        return pltpu.make_async_copy(w_hbm.at[layer, pl.ds(k * ADALN_SLAB, ADALN_SLAB)], wbuf_ref.at[slot],
                                     sem.at[slot])

    for k in range(ADALN_BUFS - 1):
        slab_copy(k).start()
    a = _silu(c_ref[...]).astype(BF16)
    acc = jnp.broadcast_to(b_ref[...], o_ref.shape)
    for k in range(n_slabs):
        if k + ADALN_BUFS - 1 < n_slabs:
            slab_copy(k + ADALN_BUFS - 1).start()
        slab_copy(k).wait()
        w = wbuf_ref[k % ADALN_BUFS].astype(BF16)
        acc = acc + jnp.dot(a[:, k * ADALN_SLAB:(k + 1) * ADALN_SLAB], w, preferred_element_type=F32)
    o_ref[...] = acc


def adaln(cc, w_ada, b_ada):
    n = 6 * D_MODEL
    out = pl.pallas_call(
        _adaln_kernel,
        out_shape=jax.ShapeDtypeStruct((DEPTH, 8, n), F32),
        grid=(DEPTH,),
        in_specs=[pl.BlockSpec((8, D_MODEL), lambda l: (0, 0)),
                  pl.BlockSpec((None, 1, n), lambda l: (l, 0, 0)),
                  pl.BlockSpec(memory_space=pl.ANY)],
        out_specs=pl.BlockSpec((None, 8, n), lambda l: (l, 0, 0)),
        scratch_shapes=[pltpu.VMEM((ADALN_BUFS, ADALN_SLAB, n), F32), pltpu.SemaphoreType.DMA((ADALN_BUFS,))],
        compiler_params=_params("arbitrary"),
        name="adaln",
    )(cc, b_ada.reshape(DEPTH, 1, n), w_ada)
    return out.reshape(DEPTH, 8, 6, 1, D_MODEL)


def _norm_proj_kernel(x_ref, g_ref, sh_ref, sc_ref, w_ref, ws_ref, tab_ref, o_ref, dt_ref, kr_ref, h_ref):
    @pl.when(pl.program_id(1) == 0)
    def _():
        y = _rms(x_ref[...]) * g_ref[...]
        h = (y * (1 + sc_ref[...]) + sh_ref[...]).astype(h_ref.dtype)
        h_ref[...] = h
        acc = jnp.dot(h, ws_ref[...], preferred_element_type=F32)
        dt_ref[...] = acc[:, :LANES]
        kr_ref[...] = _rope(acc[:, LANES:], tab_ref[...]).astype(kr_ref.dtype)

    o_ref[...] = jnp.dot(h_ref[...], w_ref[...], preferred_element_type=F32).astype(o_ref.dtype)


def norm_proj(x, g, mods, layer, w, w_small, tab):
    n = w.shape[1]
    tn = PROJ_COL_TILE
    return pl.pallas_call(
        _norm_proj_kernel,
        out_shape=(jax.ShapeDtypeStruct((ROWS, n), BF16),
                   jax.ShapeDtypeStruct((ROWS, LANES), F32), jax.ShapeDtypeStruct((ROWS, LANES), BF16)),
        grid=(ROWS // TM, n // tn),
        in_specs=[pl.BlockSpec((TM, D_MODEL), lambda i, j: (i, 0)),
                  pl.BlockSpec((1, D_MODEL), lambda i, j: (0, 0)),
                  _mod_spec(layer, 0), _mod_spec(layer, 1),
                  pl.BlockSpec((D_MODEL, tn), lambda i, j: (0, j)),
                  pl.BlockSpec((D_MODEL, 2 * LANES), lambda i, j: (0, 0)),
                  pl.BlockSpec((TM, LANES), lambda i, j: (_rope_row(i), 0))],
        out_specs=(pl.BlockSpec((TM, tn), lambda i, j: (i, j)),
                   pl.BlockSpec((TM, LANES), lambda i, j: (i, 0)),
                   pl.BlockSpec((TM, LANES), lambda i, j: (i, 0))),
        scratch_shapes=[pltpu.VMEM((TM, D_MODEL), BF16)],
        compiler_params=_params("parallel", "arbitrary"),
        name="norm_proj",
    )(x, g.reshape(1, D_MODEL), mods, mods, w, w_small, tab)


def _up_kernel(c_ref, g_ref, w_ref, *rest, rope, heads_per_tile):
    if rope:
        tab_ref, o_ref = rest
    else:
        (o_ref,) = rest
    c = _rms(c_ref[...].astype(F32)) * g_ref[...]
    acc = jnp.dot(c.astype(BF16), w_ref[...], preferred_element_type=F32)
    if not rope:
        o_ref[...] = acc.astype(o_ref.dtype)
        return
    tab = tab_ref[...]
    acc = acc * QK_SCALE
    for hh in range(heads_per_tile):
        base = hh * HEAD_Q
        o_ref[:, base:base + LANES] = acc[:, base:base + LANES].astype(o_ref.dtype)
        o_ref[:, base + LANES:base + HEAD_Q] = _rope(acc[:, base + LANES:base + HEAD_Q], tab).astype(o_ref.dtype)


def up_proj(p_main, col, g, w, tab):
    lora, n = w.shape
    tn = n
    rope = tab is not None
    in_specs = [pl.BlockSpec((TM, lora), lambda i, j: (i, col // lora)),
                pl.BlockSpec((1, lora), lambda i, j: (0, 0)),
                pl.BlockSpec((lora, tn), lambda i, j: (0, j))]
    args = [p_main, g.reshape(1, lora), w]
    if rope:
        in_specs.append(pl.BlockSpec((TM, LANES), lambda i, j: (_rope_row(i), 0)))
        args.append(tab)
    return pl.pallas_call(
        functools.partial(_up_kernel, rope=rope, heads_per_tile=tn // HEAD_Q),
        out_shape=jax.ShapeDtypeStruct((ROWS, n), BF16),
        grid=(ROWS // TM, n // tn),
        in_specs=in_specs,
        out_specs=pl.BlockSpec((TM, tn), lambda i, j: (i, j)),
        compiler_params=_params("parallel", "parallel"),
        name="up_proj_rope" if rope else "up_proj",
    )(*args)


def _softmax_chunks(q, kcat_ref, v_chunks):
    m = l = acc = None
    for start, size, v in v_chunks:
        k = kcat_ref[start:start + size, :]
        s = lax.dot_general(q, k, (((1,), (1,)), ((), ())), preferred_element_type=F32)
        m_cur = jnp.max(s, axis=-1, keepdims=True)
        if m is None:
            m_new = m_cur
            p = jnp.exp2(s - m_new)
            l = jnp.sum(p, axis=-1, keepdims=True)
            acc = jnp.dot(p.astype(BF16), v, preferred_element_type=F32)
        else:
            m_new = jnp.maximum(m, m_cur)
            alpha = jnp.exp2(m - m_new)
            p = jnp.exp2(s - m_new)
            l = alpha * l + jnp.sum(p, axis=-1, keepdims=True)
            acc = alpha * acc + jnp.dot(p.astype(BF16), v, preferred_element_type=F32)
        m = m_new
    return acc / l


def _lat_attn_kernel(q_ref, knl_ref, krl_ref, vl_ref, knc_ref, krc_ref, vc_ref, o_ref, kcat_ref):
    @pl.when(pl.program_id(2) == 0)
    def _():
        kcat_ref[0:CTX_LEN, 0:LANES] = knc_ref[...]
        kcat_ref[0:CTX_LEN, LANES:HEAD_Q] = krc_ref[...]
        kcat_ref[CTX_LEN:, 0:LANES] = knl_ref[...]
        kcat_ref[CTX_LEN:, LANES:HEAD_Q] = krl_ref[...]

    chunks = [(0, CTX_LEN, vc_ref[...])]
    for s in range(0, SEQ, ATTN_KV_CHUNK):
        chunks.append((CTX_LEN + s, ATTN_KV_CHUNK, vl_ref[s:s + ATTN_KV_CHUNK, :]))
    o_ref[...] = _softmax_chunks(q_ref[...], kcat_ref, chunks).astype(o_ref.dtype)


def latent_attention(q, kv, kr):
    tq = ATTN_Q_TILE
    qt = SEQ // tq
    ctx_blk = ROWS_LAT // CTX_LEN
    return pl.pallas_call(
        _lat_attn_kernel,
        out_shape=jax.ShapeDtypeStruct((ROWS_LAT, D_MODEL), BF16),
        grid=(BATCH, N_HEADS, qt),
        in_specs=[pl.BlockSpec((tq, HEAD_Q), lambda b, h, i: (b * qt + i, h)),
                  pl.BlockSpec((SEQ, LANES), lambda b, h, i: (b, h)),
                  pl.BlockSpec((SEQ, LANES), lambda b, h, i: (b, 0)),
                  pl.BlockSpec((SEQ, LANES), lambda b, h, i: (b, N_HEADS + h)),
                  pl.BlockSpec((CTX_LEN, LANES), lambda b, h, i: (ctx_blk + b, h)),
                  pl.BlockSpec((CTX_LEN, LANES), lambda b, h, i: (ctx_blk + b, 0)),
                  pl.BlockSpec((CTX_LEN, LANES), lambda b, h, i: (ctx_blk + b, N_HEADS + h))],
        out_specs=pl.BlockSpec((tq, LANES), lambda b, h, i: (b * qt + i, h)),
        scratch_shapes=[pltpu.VMEM((CTX_LEN + SEQ, HEAD_Q), BF16)],
        compiler_params=_params("parallel", "parallel", "arbitrary"),
        name="latent_attention",
    )(q, kv, kr, kv, kv, kr, kv)


def _ctx_attn_kernel(q_ref, kn_ref, kr_ref, v_ref, o_ref, kcat_ref):
    kcat_ref[:, 0:LANES] = kn_ref[...]
    kcat_ref[:, LANES:HEAD_Q] = kr_ref[...]
    o_ref[...] = _softmax_chunks(q_ref[...], kcat_ref, [(0, CTX_LEN, v_ref[...])]).astype(o_ref.dtype)


def context_attention(q, kv, kr):
    ctx_blk = ROWS_LAT // CTX_LEN
    return pl.pallas_call(
        _ctx_attn_kernel,
        out_shape=jax.ShapeDtypeStruct((ROWS_CTX, D_MODEL), BF16),
        grid=(BATCH, N_HEADS),
        in_specs=[pl.BlockSpec((CTX_LEN, HEAD_Q), lambda b, h: (ctx_blk + b, h)),
                  pl.BlockSpec((CTX_LEN, LANES), lambda b, h: (ctx_blk + b, h)),
                  pl.BlockSpec((CTX_LEN, LANES), lambda b, h: (ctx_blk + b, 0)),
                  pl.BlockSpec((CTX_LEN, LANES), lambda b, h: (ctx_blk + b, N_HEADS + h))],
        out_specs=pl.BlockSpec((CTX_LEN, LANES), lambda b, h: (b, h)),
        scratch_shapes=[pltpu.VMEM((CTX_LEN, HEAD_Q), BF16)],
        compiler_params=_params("parallel", "parallel"),
        name="context_attention",
    )(q, kv, kr, kv)


def _chunk_block(b, s, reverse):
    if reverse:
        return jnp.where(s < CHUNKS_CTX, ROWS_LAT // CHUNK + CHUNKS_CTX * b + (CHUNKS_CTX - 1 - s),
                         CHUNKS_LAT * b + (CHUNKS_SEQ - 1 - s))
    return jnp.where(s < CHUNKS_CTX, ROWS_LAT // CHUNK + CHUNKS_CTX * b + s,
                     CHUNKS_LAT * b + (s - CHUNKS_CTX))


def _expand_heads(v, e3_ref):
    hi = v.astype(BF16)
    r1 = v - hi.astype(F32)
    mid = r1.astype(BF16)
    lo = (r1 - mid.astype(F32)).astype(BF16)
    return jnp.dot(jnp.concatenate([hi, mid, lo], axis=1), e3_ref[...], preferred_element_type=F32)


def _ssd_chunk(xcol, dt_raw, dtb_ref, a_ref, e3_ref, state_ref, reverse):
    off = SSM_HEADS if reverse else 0
    heads = slice(off, off + SSM_HEADS)
    row = lax.broadcasted_iota(jnp.int32, (CHUNK, CHUNK), 0)
    colm = lax.broadcasted_iota(jnp.int32, (CHUNK, CHUNK), 1)
    keep = (colm >= row) if reverse else (colm <= row)
    tri = keep.astype(F32)
    dt = jax.nn.softplus(dt_raw + dtb_ref[...])
    a = dt * a_ref[...]
    cum = jnp.dot(tri, a, precision=HIGHEST, preferred_element_type=F32)
    cum2 = cum * LOG2E
    cum2_t = cum2.T
    last_row = 0 if reverse else CHUNK - 1
    dec_h = jnp.exp2(cum2)
    end_h = dt * jnp.exp2(cum2[last_row:last_row + 1, :] - cum2)
    dt_full = _expand_heads(dt[:, heads], e3_ref)
    end_full = _expand_heads(end_h[:, heads], e3_ref)
    dec_in = _expand_heads(dec_h[:, heads], e3_ref)
    chunk_dec = dec_in[last_row:last_row + 1, :]
    xs = xcol(0, D_INNER)
    xcb = (xs * dt_full).astype(BF16)
    xce = (xs * end_full).astype(BF16)
    gw = SSM_HEADS // SSM_GROUPS * SSM_HEAD_DIM
    lane = lax.broadcasted_iota(jnp.int32, (CHUNK, LANES), 1)
    ys = []
    for g in range(SSM_GROUPS):
        b32 = xcol(D_INNER + g * D_STATE, D_INNER + (g + 1) * D_STATE)
        bg = b32.astype(BF16)
        cg = xcol(D_INNER + (SSM_GROUPS + g) * D_STATE, D_INNER + (SSM_GROUPS + g + 1) * D_STATE).astype(BF16)
        cb = lax.dot_general(cg, bg, (((1,), (1,)), ((), ())), preferred_element_type=F32)
        st = state_ref[g]
        y_off = jnp.dot(cg, st.astype(BF16), preferred_element_type=F32) * dec_in[:, g * gw:(g + 1) * gw]
        upd = jnp.dot(b32.T.astype(BF16), xce[:, g * gw:(g + 1) * gw],
                      preferred_element_type=F32)
        state_ref[g] = st * chunk_dec[:, g * gw:(g + 1) * gw] + upd
        pairs = []
        for j in range(gw // LANES):
            h0 = off + g * (SSM_HEADS // SSM_GROUPS) + 2 * j
            ms = []
            for hh in (h0, h0 + 1):
                seg = jnp.where(keep, cum2[:, hh:hh + 1] - cum2_t[hh:hh + 1, :], -jnp.inf)
                ms.append((jnp.exp2(seg) * cb).astype(BF16))
            xp = xcb[:, g * gw + j * LANES:g * gw + (j + 1) * LANES]
            rhs = jnp.concatenate([jnp.where(lane < SSM_HEAD_DIM, xp, 0), jnp.where(lane >= SSM_HEAD_DIM, xp, 0)],
                                  axis=0)
            pairs.append(jnp.dot(jnp.concatenate(ms, axis=1), rhs, preferred_element_type=F32))
        ys.append(jnp.concatenate(pairs, axis=1) + y_off)
    return jnp.concatenate(ys, axis=1)


def _ssd_fwd_kernel(prev_ref, cur_ref, next_ref, dt_ref, shift_ref, cw_ref, cbias_ref, dtb_ref, a_ref, e3_ref,
                    y_ref, xbc_ref, state_ref, xs_ref):
    s = pl.program_id(1)

    @pl.when(s == 0)
    def _():
        state_ref[...] = jnp.zeros_like(state_ref)

    first = (s == 0) | (s == CHUNKS_CTX)
    final = (s == CHUNKS_CTX - 1) | (s == CHUNKS_SEQ - 1)
    halo_zeros = jnp.zeros(prev_ref.shape, prev_ref.dtype)
    prev = jnp.where(first, halo_zeros, prev_ref[...])
    nxt = jnp.where(final, halo_zeros, next_ref[...])
    ext = jnp.concatenate([prev, cur_ref[...], nxt], axis=0)
    for c0 in range(0, CONV_DIM, CONV_COL_TILE):
        cs = slice(c0, c0 + CONV_COL_TILE)
        shifted = jnp.dot(shift_ref[...], ext[:, cs], preferred_element_type=F32)
        acc = cbias_ref[:, cs] + cw_ref[CONV_K // 2:CONV_K // 2 + 1, cs] * cur_ref[:, cs].astype(F32)
        for j, k in enumerate(OFF_CENTRE_TAPS):
            acc = acc + cw_ref[k:k + 1, cs] * shifted[j * CHUNK:(j + 1) * CHUNK, :]
        xs = _silu(acc)
        xs_ref[:, cs] = xs
        xbc_ref[:, cs] = xs.astype(xbc_ref.dtype)
    y_ref[...] = _ssd_chunk(lambda a, b: xs_ref[:, a:b], dt_ref[...], dtb_ref, a_ref, e3_ref, state_ref,
                            reverse=False)


def _ssd_bwd_kernel(xbc_ref, dt_ref, yf_ref, z0_ref, z1_ref, dtb_ref, a_ref, e3_ref, dsum_ref, nw_ref,
                    u_ref, state_ref):
    @pl.when(pl.program_id(1) == 0)
    def _():
        state_ref[...] = jnp.zeros_like(state_ref)

    xcol = lambda a, b: xbc_ref[:, a:b].astype(F32)
    y = _ssd_chunk(xcol, dt_ref[...], dtb_ref, a_ref, e3_ref, state_ref, reverse=True)
    y = y + yf_ref[...] + dsum_ref[...] * xcol(0, D_INNER)
    z = jnp.concatenate([z0_ref[...], z1_ref[...]], axis=1).astype(F32)
    u = y * _silu(z)
    gw = D_INNER // SSM_GROUPS
    for g in range(SSM_GROUPS):
        ug = u[:, g * gw:(g + 1) * gw]
        u_ref[:, g * gw:(g + 1) * gw] = (_rms(ug) * nw_ref[:, g * gw:(g + 1) * gw]).astype(u_ref.dtype)


def ssd_mixer(p_main, dt_raw, conv_w, conv_b, dt_bias, a_log, d_skip, ssm_norm, e3):
    zb = COL_Z // (D_INNER // 2)
    dtb = dt_bias.reshape(1, 2 * SSM_HEADS).astype(F32)
    a_neg = (-jnp.exp(a_log.astype(F32))).reshape(1, 2 * SSM_HEADS)
    dsum = jnp.repeat(d_skip[0] + d_skip[1], SSM_HEAD_DIM).reshape(1, D_INNER)
    halo_per_chunk = CHUNK // HALO_ROWS
    n_halo = ROWS // HALO_ROWS
    shift_np = np.zeros((len(OFF_CENTRE_TAPS) * CHUNK, CHUNK + 2 * HALO_ROWS), np.float32)
    for j, k in enumerate(OFF_CENTRE_TAPS):
        shift_np[j * CHUNK + np.arange(CHUNK), HALO_ROWS + np.arange(CHUNK) + k - CONV_K // 2] = 1.0
    shift = jnp.asarray(shift_np, dtype=BF16)
    full = lambda shape: pl.BlockSpec(shape, lambda b, s: tuple(0 for _ in shape))

    def blk(reverse):
        return lambda b, s: (_chunk_block(b, s, reverse), 0)

    fwd_blk = blk(False)
    y_f, xbc = pl.pallas_call(
        _ssd_fwd_kernel,
        out_shape=(jax.ShapeDtypeStruct((ROWS, D_INNER), F32), jax.ShapeDtypeStruct((ROWS, CONV_DIM), BF16)),
        grid=(BATCH, CHUNKS_SEQ),
        in_specs=[
            pl.BlockSpec((HALO_ROWS, CONV_DIM),
                         lambda b, s: (jnp.maximum(_chunk_block(b, s, False) * halo_per_chunk - 1, 0), 0)),
            pl.BlockSpec((CHUNK, CONV_DIM), fwd_blk),
            pl.BlockSpec((HALO_ROWS, CONV_DIM),
                         lambda b, s: (jnp.minimum((_chunk_block(b, s, False) + 1) * halo_per_chunk, n_halo - 1), 0)),
            pl.BlockSpec((CHUNK, LANES), fwd_blk),
            full(shift.shape),
            full((CONV_K, CONV_DIM)), full((1, CONV_DIM)), full((1, LANES)), full((1, LANES)),
            full((3 * SSM_HEADS, D_INNER)),
        ],
        out_specs=(pl.BlockSpec((CHUNK, D_INNER), fwd_blk), pl.BlockSpec((CHUNK, CONV_DIM), fwd_blk)),
        scratch_shapes=[pltpu.VMEM((SSM_GROUPS, D_STATE, D_INNER // SSM_GROUPS), F32),
                        pltpu.VMEM((CHUNK, CONV_DIM), F32)],
        compiler_params=_params("parallel", "arbitrary"),
        name="ssd_forward",
    )(p_main, p_main, p_main, dt_raw, shift, conv_w, conv_b.reshape(1, CONV_DIM), dtb, a_neg, e3)
    bwd_blk = blk(True)
    return pl.pallas_call(
        _ssd_bwd_kernel,
        out_shape=jax.ShapeDtypeStruct((ROWS, D_INNER), BF16),
        grid=(BATCH, CHUNKS_SEQ),
        in_specs=[
            pl.BlockSpec((CHUNK, CONV_DIM), bwd_blk),
            pl.BlockSpec((CHUNK, LANES), bwd_blk),
            pl.BlockSpec((CHUNK, D_INNER), bwd_blk),
            pl.BlockSpec((CHUNK, D_INNER // 2), lambda b, s: (_chunk_block(b, s, True), zb)),
            pl.BlockSpec((CHUNK, D_INNER // 2), lambda b, s: (_chunk_block(b, s, True), zb + 1)),
            full((1, LANES)), full((1, LANES)), full((3 * SSM_HEADS, D_INNER)),
            full((1, D_INNER)), full((1, D_INNER)),
        ],
        out_specs=pl.BlockSpec((CHUNK, D_INNER), bwd_blk),
        scratch_shapes=[pltpu.VMEM((SSM_GROUPS, D_STATE, D_INNER // SSM_GROUPS), F32)],
        compiler_params=_params("parallel", "arbitrary"),
        name="ssd_backward",
    )(xbc, dt_raw, y_f, p_main, p_main, dtb, a_neg, e3, dsum, ssm_norm.reshape(1, D_INNER))


def _merge_kernel(attl_ref, attc_ref, u_ref, woa_ref, wob_ref, ga_ref, gb_ref, o_ref):
    att = jnp.where(pl.program_id(0) < N_LAT_TILES, attl_ref[...], attc_ref[...])
    o_a = jnp.dot(att, woa_ref[...], preferred_element_type=F32)
    o_b = jnp.dot(u_ref[...], wob_ref[...], preferred_element_type=F32)
    o_ref[...] = (jax.nn.sigmoid(ga_ref[...].astype(F32)) * o_a
                  + jax.nn.sigmoid(gb_ref[...].astype(F32)) * o_b).astype(o_ref.dtype)


def merge_branches(att_lat, att_ctx, u, p_main, w_oa, w_ob, rows):
    tn = 1024
    return pl.pallas_call(
        _merge_kernel,
        out_shape=jax.ShapeDtypeStruct((rows, D_MODEL), BF16),
        grid=(rows // TM, D_MODEL // tn),
        in_specs=[pl.BlockSpec((TM, D_MODEL), lambda i, j: (jnp.minimum(i, N_LAT_TILES - 1), 0)),
                  pl.BlockSpec((TM, D_MODEL), lambda i, j: (0, 0)),
                  pl.BlockSpec((TM, D_INNER), lambda i, j: (i, 0)),
                  pl.BlockSpec((D_MODEL, tn), lambda i, j: (0, j)),
                  pl.BlockSpec((D_INNER, tn), lambda i, j: (0, j)),
                  pl.BlockSpec((TM, tn), lambda i, j: (i, COL_GA // tn + j)),
                  pl.BlockSpec((TM, tn), lambda i, j: (i, COL_GB // tn + j))],
        out_specs=pl.BlockSpec((TM, tn), lambda i, j: (i, j)),
        compiler_params=_params("parallel", "parallel"),
        name="merge_branches",
    )(att_lat, att_ctx, u, w_oa, w_ob, p_main, p_main)


def _route(h, wr_ref):
    lane = lax.broadcasted_iota(jnp.int32, (h.shape[0], LANES), 1)
    lg = jnp.full((h.shape[0], LANES), -jnp.inf, F32)
    for e in range(N_EXPERTS):
        lg = jnp.where(lane == e, jnp.sum(h * wr_ref[e:e + 1, :], axis=-1, keepdims=True), lg)
    v1 = jnp.max(lg, axis=-1, keepdims=True)
    i1 = jnp.min(jnp.where(lg == v1, lane, LANES), axis=-1, keepdims=True)
    lg2 = jnp.where(lane == i1, -jnp.inf, lg)
    v2 = jnp.max(lg2, axis=-1, keepdims=True)
    i2 = jnp.min(jnp.where(lg2 == v2, lane, LANES), axis=-1, keepdims=True)
    e = jnp.exp(v2 - v1)
    g1 = 1.0 / (1.0 + e)
    g2 = e / (1.0 + e)
    return jnp.where(lane == 0, i1.astype(F32),
                     jnp.where(lane == 1, i2.astype(F32),
                               jnp.where(lane == 2, g1, jnp.where(lane == 3, g2, 0.0))))


def _out_res_kernel(m_ref, w_ref, x_ref, gate_ref, g2_ref, sh_ref, sc_ref, *rest, route):
    if route:
        wr_ref, xo_ref, ho_ref, ro_ref = rest
    else:
        xo_ref, ho_ref = rest
    acc = jnp.dot(m_ref[...], w_ref[...], preferred_element_type=F32)
    xn = x_ref[...] + gate_ref[...] * acc
    xo_ref[...] = xn
    h = (_rms(xn) * g2_ref[...]) * (1 + sc_ref[...]) + sh_ref[...]
    ho_ref[...] = h.astype(ho_ref.dtype)
    if route:
        ro_ref[...] = _route(h, wr_ref)


def out_proj_residual(mixed, w_out, x, g2, mods, layer, rows, w_router=None):
    route = w_router is not None
    h_dtype = F32 if route else BF16
    in_specs = [pl.BlockSpec((TM, D_MODEL), lambda i: (i, 0)),
                pl.BlockSpec((D_MODEL, D_MODEL), lambda i: (0, 0)),
                pl.BlockSpec((TM, D_MODEL), lambda i: (i, 0)),
                _mod_spec(layer, 2), pl.BlockSpec((1, D_MODEL), lambda i: (0, 0)),
                _mod_spec(layer, 3), _mod_spec(layer, 4)]
    args = [mixed, w_out, x, mods, g2.reshape(1, D_MODEL), mods, mods]
    out_shape = [jax.ShapeDtypeStruct((rows, D_MODEL), F32), jax.ShapeDtypeStruct((rows, D_MODEL), h_dtype)]
    out_specs = [pl.BlockSpec((TM, D_MODEL), lambda i: (i, 0)), pl.BlockSpec((TM, D_MODEL), lambda i: (i, 0))]
    if route:
        in_specs.append(pl.BlockSpec((N_EXPERTS, D_MODEL), lambda i: (0, 0)))
        args.append(w_router.T)
        out_shape.append(jax.ShapeDtypeStruct((rows, LANES), F32))
        out_specs.append(pl.BlockSpec((TM, LANES), lambda i: (i, 0)))
    return pl.pallas_call(
        functools.partial(_out_res_kernel, route=route),
        out_shape=tuple(out_shape),
        grid=(rows // TM,),
        in_specs=in_specs,
        out_specs=tuple(out_specs),
        compiler_params=_params("parallel"),
        name="out_proj_residual",
    )(*args)


def _ffn_up_kernel(h_ref, w1_ref, w3_ref, o_ref):
    h = h_ref[...]
    a = jnp.dot(h, w1_ref[...], preferred_element_type=F32)
    b = jnp.dot(h, w3_ref[...], preferred_element_type=F32)
    o_ref[...] = (_silu(a) * b).astype(o_ref.dtype)


def ffn_up(h, w1, w3):
    tn = D_FF // 4
    return pl.pallas_call(
        _ffn_up_kernel,
        out_shape=jax.ShapeDtypeStruct((ROWS, D_FF), BF16),
        grid=(ROWS // TM, D_FF // tn),
        in_specs=[pl.BlockSpec((TM, D_MODEL), lambda i, j: (i, 0)),
                  pl.BlockSpec((D_MODEL, tn), lambda i, j: (0, j)),
                  pl.BlockSpec((D_MODEL, tn), lambda i, j: (0, j))],
        out_specs=pl.BlockSpec((TM, tn), lambda i, j: (i, j)),
        compiler_params=_params("parallel", "parallel"),
        name="ffn_up",
    )(h, w1, w3)


def _ffn_down_kernel(g_ref, w_ref, x_ref, gate_ref, o_ref):
    acc = jnp.dot(g_ref[...], w_ref[...], preferred_element_type=F32)
    o_ref[...] = x_ref[...] + gate_ref[...] * acc


def ffn_down_residual(g, w2, x, mods, layer):
    tn = 1024
    gate_spec = pl.BlockSpec((None, None, None, 1, tn), lambda i, j: (layer, _mod_row(i), 5, 0, j))
    return pl.pallas_call(
        _ffn_down_kernel,
        out_shape=jax.ShapeDtypeStruct((ROWS, D_MODEL), F32),
        grid=(ROWS // TM, D_MODEL // tn),
        in_specs=[pl.BlockSpec((TM, D_FF), lambda i, j: (i, 0)),
                  pl.BlockSpec((D_FF, tn), lambda i, j: (0, j)),
                  pl.BlockSpec((TM, tn), lambda i, j: (i, j)),
                  gate_spec],
        out_specs=pl.BlockSpec((TM, tn), lambda i, j: (i, j)),
        compiler_params=_params("parallel", "parallel"),
        name="ffn_down_residual",
    )(g, w2, x, mods)


def _gather_rows_kernel(src_ref, nvalid_ref, h_hbm, o_ref, rows_ref, sem):
    step = pl.program_id(0)
    base = step * GATHER_ROWS

    @pl.when(step == 0)
    def _():
        rows_ref[...] = jnp.zeros_like(rows_ref)

    n_groups = lax.shift_right_logical(nvalid_ref[step] + (DMA_UNROLL - 1), DMA_UNROLL.bit_length() - 1)

    def row_copy(j):
        return pltpu.make_async_copy(h_hbm.at[pl.ds(src_ref[base + j], 1)], rows_ref.at[pl.ds(j, 1)], sem)

    def start(gi, c):
        for u in range(DMA_UNROLL):
            row_copy(gi * DMA_UNROLL + u).start(priority=u % 2)
        return c

    def wait(gi, c):
        for u in range(DMA_UNROLL):
            row_copy(gi * DMA_UNROLL + u).wait()
        return c

    lax.fori_loop(0, n_groups, start, 0)
    lax.fori_loop(0, n_groups, wait, 0)
    o_ref[...] = rows_ref[...].astype(o_ref.dtype)


def gather_rows(h, src, step_rows):
    n = src.shape[0]
    d = h.shape[1]
    grid_spec = pltpu.PrefetchScalarGridSpec(
        num_scalar_prefetch=2, grid=(n // GATHER_ROWS,),
        in_specs=[pl.BlockSpec(memory_space=pl.ANY)],
        out_specs=pl.BlockSpec((GATHER_ROWS, d), lambda i, s, nv: (i, 0)),
        scratch_shapes=[pltpu.VMEM((GATHER_ROWS, d), h.dtype), pltpu.SemaphoreType.DMA(())])
    return pl.pallas_call(
        _gather_rows_kernel,
        out_shape=jax.ShapeDtypeStruct((n, d), BF16),
        grid_spec=grid_spec,
        compiler_params=_params("arbitrary"),
        name="moe_gather",
    )(src, step_rows, h)


def _moe_kernel(be_ref, nused_ref, nquarter_ref, x_ref, w1_ref, w3_ref, w2_ref, o_ref):
    i = pl.program_id(0)
    f = pl.program_id(1)

    @pl.when(f == 0)
    def _():
        o_ref[...] = jnp.zeros_like(o_ref)

    def mlp(rows):
        x = x_ref[0:rows, :]
        h1 = jnp.dot(x, w1_ref[...].astype(BF16), preferred_element_type=F32)
        h3 = jnp.dot(x, w3_ref[...].astype(BF16), preferred_element_type=F32)
        g = _silu(h1) * h3
        o_ref[0:rows, :] += jnp.dot(g.astype(BF16), w2_ref[...].astype(BF16), preferred_element_type=F32)

    for nq in range(1, MOE_QUARTERS + 1):
        @pl.when(nquarter_ref[i] == nq)
        def _(nq=nq):
            mlp(nq * (MOE_ROW_TILE // MOE_QUARTERS))


def moe_experts(buf, block_expert, n_used, block_quarters, w1, w3, w2):
    r, d = buf.shape
    tm, tf = MOE_ROW_TILE, MOE_FF_TILE
    nf = D_FF_EXPERT // tf

    def f_idx(i, f, nu):
        return jnp.where(i < nu[0], f, nf - 1)

    grid_spec = pltpu.PrefetchScalarGridSpec(
        num_scalar_prefetch=3,
        grid=(r // tm, nf),
        in_specs=[
            pl.BlockSpec((tm, d), lambda i, f, be, nu, nh: (jnp.minimum(i, nu[0] - 1), 0)),
            pl.BlockSpec((None, d, tf), lambda i, f, be, nu, nh: (be[i], 0, f_idx(i, f, nu))),
            pl.BlockSpec((None, d, tf), lambda i, f, be, nu, nh: (be[i], 0, f_idx(i, f, nu))),
            pl.BlockSpec((None, tf, d), lambda i, f, be, nu, nh: (be[i], f_idx(i, f, nu), 0)),
        ],
        out_specs=pl.BlockSpec((tm, d), lambda i, f, be, nu, nh: (i, 0)),
    )
    return pl.pallas_call(
        _moe_kernel,
        out_shape=jax.ShapeDtypeStruct((r, d), F32),
        grid_spec=grid_spec,
        compiler_params=_params("arbitrary", "arbitrary"),
        name="moe_experts",
    )(block_expert, n_used, block_quarters, buf, w1, w3, w2)


def _combine_kernel(dest_ref, y_hbm, x_ref, route_ref, gate_ref, gfin_ref, o_ref, ybuf_ref, sem):
    base = pl.program_id(0) * GATHER_ROWS

    def row_copy(t, k):
        return pltpu.make_async_copy(y_hbm.at[pl.ds(dest_ref[TOP_K * (base + t) + k], 1)],
                                     ybuf_ref.at[k, pl.ds(t, 1)], sem)

    def start(gi, c):
        for u in range(DMA_UNROLL):
            for k in range(TOP_K):
                row_copy(gi * DMA_UNROLL + u, k).start(priority=k)
        return c

    def wait(gi, c):
        for u in range(DMA_UNROLL):
            for k in range(TOP_K):
                row_copy(gi * DMA_UNROLL + u, k).wait()
        return c

    lax.fori_loop(0, GATHER_ROWS // DMA_UNROLL, start, 0)
    lax.fori_loop(0, GATHER_ROWS // DMA_UNROLL, wait, 0)
    route = route_ref[...]
    f = ybuf_ref[0] * route[:, 2:3] + ybuf_ref[1] * route[:, 3:4]
    xn = x_ref[...] + gate_ref[...] * f
    o_ref[...] = _rms(xn) * gfin_ref[...]


def moe_combine_final(y, dest, x, route, mods, layer, final_norm):
    tiles_per_batch = SEQ // GATHER_ROWS
    grid_spec = pltpu.PrefetchScalarGridSpec(
        num_scalar_prefetch=1, grid=(ROWS_LAT // GATHER_ROWS,),
        in_specs=[pl.BlockSpec(memory_space=pl.ANY),
                  pl.BlockSpec((GATHER_ROWS, D_MODEL), lambda i, d: (i, 0)),
                  pl.BlockSpec((GATHER_ROWS, LANES), lambda i, d: (i, 0)),
                  pl.BlockSpec((None, None, None, 1, D_MODEL), lambda i, d: (layer, i // tiles_per_batch, 5, 0, 0)),
                  pl.BlockSpec((1, D_MODEL), lambda i, d: (0, 0))],
        out_specs=pl.BlockSpec((GATHER_ROWS, D_MODEL), lambda i, d: (i, 0)),
        scratch_shapes=[pltpu.VMEM((TOP_K, GATHER_ROWS, D_MODEL), F32), pltpu.SemaphoreType.DMA(())])
    return pl.pallas_call(
        _combine_kernel,
        out_shape=jax.ShapeDtypeStruct((ROWS_LAT, D_MODEL), F32),
        grid_spec=grid_spec,
        compiler_params=_params("arbitrary"),
        name="moe_combine_final",
    )(dest, y, x, route, mods, final_norm.reshape(1, D_MODEL))


def moe_dispatch(route):
    expert = route[:, :TOP_K].astype(jnp.int32).reshape(-1)
    onehot = (expert[:, None] == jnp.arange(N_EXPERTS)[None, :]).astype(jnp.int32)
    incl = jnp.cumsum(onehot, axis=0)
    rank = jnp.sum((incl - onehot) * onehot, axis=1)
    counts = incl[-1]
    tm = MOE_ROW_TILE
    padded = (counts + tm - 1) // tm * tm
    pad_end = jnp.cumsum(padded)
    dest = ((pad_end - padded)[expert] + rank).astype(jnp.int32)
    n_blocks = MOE_SLOTS // tm
    n_used = (pad_end[-1] // tm).astype(jnp.int32)
    blocks = jnp.arange(n_blocks)
    block_expert = jnp.minimum(jnp.searchsorted(pad_end, blocks * tm, side='right'), N_EXPERTS - 1).astype(jnp.int32)
    block_rows = jnp.clip(counts[block_expert] - (blocks * tm - (pad_end - padded)[block_expert]), 0, tm)
    block_rows = jnp.where(blocks < n_used, block_rows, 0)
    block_expert = jnp.where(blocks < n_used, block_expert, block_expert[jnp.maximum(n_used - 1, 0)])
    quarter = tm // MOE_QUARTERS
    block_quarters = ((block_rows + quarter - 1) // quarter).astype(jnp.int32)
    steps = jnp.arange(MOE_SLOTS // GATHER_ROWS)
    per_block = tm // GATHER_ROWS
    step_rows = jnp.clip(block_rows[steps // per_block] - (steps % per_block) * GATHER_ROWS, 0, GATHER_ROWS)
    token = jnp.repeat(jnp.arange(ROWS_LAT, dtype=jnp.int32), TOP_K)
    src = jnp.zeros((MOE_SLOTS,), jnp.int32).at[dest].set(token)
    return dest, src, step_rows.astype(jnp.int32), block_expert, n_used.reshape(1), block_quarters


def _rope_table(n):
    rows = n // GRID_W
    row = jnp.repeat(jnp.arange(rows, dtype=F32), GRID_W)
    col = jnp.tile(jnp.arange(GRID_W, dtype=F32), rows)
    axis_dim = ROPE_DIM // 2
    inv = ROPE_BASE ** (-jnp.arange(0, axis_dim, 2, dtype=F32) / axis_dim)
    ang_r, ang_c = row[:, None] * inv, col[:, None] * inv
    cr, sr, cc, sc = jnp.cos(ang_r), jnp.sin(ang_r), jnp.cos(ang_c), jnp.sin(ang_c)
    lat = jnp.concatenate([cr, cr, cc, cc, -sr, sr, -sc, sc], axis=1)
    ident = jnp.concatenate([jnp.ones((TM, ROPE_DIM), F32), jnp.zeros((TM, ROPE_DIM), F32)], axis=1)
    return jnp.concatenate([lat, ident], axis=0)


def _layer_weights(w_in, w_uq, w_ukv):
    o = np.cumsum((0,) + PROJ_SIZES)
    seg = lambda i: w_in[:, int(o[i]):int(o[i + 1])]
    w_main = jnp.concatenate([seg(4), seg(3), seg(6), seg(7), seg(0), seg(1)], axis=1).astype(BF16)
    kr_w = seg(2)
    w_small = jnp.concatenate([seg(5), kr_w, kr_w[:, ROPE_SWAP]], axis=1).astype(BF16)
    wq = w_uq.reshape(Q_LORA, N_HEADS, NOPE_DIM + ROPE_DIM)
    rope_w = wq[:, :, NOPE_DIM:]
    w_q = jnp.concatenate([wq[:, :, :NOPE_DIM], rope_w, rope_w[:, :, ROPE_SWAP]], axis=2)
    w_q = w_q.reshape(Q_LORA, N_HEADS * HEAD_Q).astype(BF16)
    wkv = w_ukv.reshape(KV_LORA, N_HEADS, NOPE_DIM + V_DIM)
    w_kv = jnp.concatenate([wkv[:, :, :NOPE_DIM].reshape(KV_LORA, -1), wkv[:, :, NOPE_DIM:].reshape(KV_LORA, -1)],
                           axis=1).astype(BF16)
    return w_main, w_small, w_q, w_kv


def kernel(x, c, ctx, c_ctx, norm_mix, norm_ffn, w_ada, b_ada, w_in, q_norm, w_uq, kv_norm, w_ukv,
           conv_w, conv_b, a_log, dt_bias, d_skip, ssm_norm, w_oa, w_ob, w_out, w1_dense, w3_dense,
           w2_dense, w_router, w1_moe, w3_moe, w2_moe, final_norm):
    tab = _rope_table(SEQ)
    cc = jnp.concatenate([c, c_ctx[None], jnp.zeros((8 - BATCH - 1, D_MODEL), F32)], axis=0)
    mods = adaln(cc, w_ada, b_ada)
    head_of = np.arange(D_INNER) // SSM_HEAD_DIM
    e1 = (np.arange(SSM_HEADS)[:, None] == head_of[None, :]).astype(np.float32)
    e3 = jnp.asarray(np.concatenate([e1, e1, e1], axis=0), dtype=BF16)
    xr = jnp.concatenate([x.reshape(ROWS_LAT, D_MODEL), ctx.reshape(ROWS_CTX, D_MODEL)], axis=0)
    out = None
    for l in range(DEPTH):
        last = l == DEPTH - 1
        rows = ROWS_LAT if last else ROWS
        w_main, w_small, w_q, w_kv = _layer_weights(w_in[l], w_uq[l], w_ukv[l])
        p_main, dt_raw, kr = norm_proj(xr, norm_mix[l], mods, l, w_main, w_small, tab)
        q = up_proj(p_main, COL_CQ, q_norm[l], w_q, tab)
        kv = up_proj(p_main, COL_CKV, kv_norm[l], w_kv, None)
        att = latent_attention(q, kv, kr)
        att_ctx = att if last else context_attention(q, kv, kr)
        u = ssd_mixer(p_main, dt_raw, conv_w[l], conv_b[l], dt_bias[l], a_log[l], d_skip[l], ssm_norm[l], e3)
        mixed = merge_branches(att, att_ctx, u, p_main, w_oa[l].astype(BF16), w_ob[l].astype(BF16), rows)
        if l % 2 == 0:
            xr, h2 = out_proj_residual(mixed, w_out[l].astype(BF16), xr, norm_ffn[l], mods, l, rows)
            g = ffn_up(h2, w1_dense[l // 2].astype(BF16), w3_dense[l // 2].astype(BF16))
            xr = ffn_down_residual(g, w2_dense[l // 2].astype(BF16), xr, mods, l)
        else:
            wr = w_router[l // 2]
            x_lat, h2, route = out_proj_residual(mixed, w_out[l].astype(BF16), xr, norm_ffn[l], mods, l, rows, wr)
            dest, src, step_rows, block_expert, n_used, block_quarters = moe_dispatch(route)
            buf = gather_rows(h2, src, step_rows)
            y = moe_experts(buf, block_expert, n_used, block_quarters, w1_moe[l // 2], w3_moe[l // 2], w2_moe[l // 2])
            out = moe_combine_final(y, dest, x_lat, route, mods, l, final_norm)
    return out.reshape(BATCH, SEQ, D_MODEL)
```
